```python
import jax, jax.numpy as jnp
from jax import lax
import numpy as np

D_MODEL = 2048
BATCH = 4
SEQ = 8192
DEPTH = 1
DEC_BATCH = 32
DEC_SEQ = 32
PAST_LEN = 4096

CHUNK = 64
EPS = 1e-6
RET_HEADS = 8
RET_DK = 256
RET_DV = 512
RET_QK = RET_HEADS * RET_DK
RET_V = RET_HEADS * RET_DV
ROPE_THETA = 10000.0
SSM_INNER = 2 * D_MODEL
SSM_HEADDIM = 64
SSM_HEADS = SSM_INNER // SSM_HEADDIM
SSM_GROUPS = 8
SSM_STATE = 128
SSM_CONV = 4
SSM_CONV_DIM = SSM_INNER + 2 * SSM_GROUPS * SSM_STATE
D_FF = 5632
PLE_DIM = 256
IN_SPLITS = (RET_QK, RET_QK, RET_V, RET_V, SSM_INNER, SSM_CONV_DIM, SSM_HEADS, D_MODEL, D_MODEL)
N_IN = sum(IN_SPLITS)

kernel_name = "hybrid_retention_ssd_macaron_stream_step"


def _rmsnorm(x, g):
    xf = x.astype(jnp.float32)
    y = xf * lax.rsqrt(jnp.mean(xf * xf, axis=-1, keepdims=True) + EPS)
    return (y * g.astype(jnp.float32)).astype(x.dtype)


def _swiglu(h, w_gu, w_down):
    gate, up = jnp.split(h @ w_gu, 2, axis=-1)
    return (jax.nn.silu(gate) * up) @ w_down


def _rope(x, pos):
    half = x.shape[-1] // 2
    inv = ROPE_THETA ** (-jnp.arange(half, dtype=jnp.float32) / half)
    ang = pos.astype(jnp.float32)[:, None] * inv[None, :]
    cos = jnp.cos(ang)[None, :, None, :]
    sin = jnp.sin(ang)[None, :, None, :]
    xf = x.astype(jnp.float32)
    x1, x2 = xf[..., :half], xf[..., half:]
    return jnp.concatenate([x1 * cos - x2 * sin, x2 * cos + x1 * sin], axis=-1)


def _to_chunks(a, c):
    b, L = a.shape[:2]
    a = a.reshape((b, L // c, c) + a.shape[2:])
    return jnp.moveaxis(a, 1, 0)


def _from_chunks(a):
    a = jnp.moveaxis(a, 0, 1)
    return a.reshape((a.shape[0], a.shape[1] * a.shape[2]) + a.shape[3:])


def _retention(q, k, v, s0):
    L = q.shape[1]
    c = min(CHUNK, L)
    log_g = jnp.log1p(-jnp.exp2(-5.0 - jnp.arange(RET_HEADS, dtype=jnp.float32)))
    idx = jnp.arange(c, dtype=jnp.float32)
    diff = idx[:, None] - idx[None, :]
    dmat = jnp.where(diff[None] >= 0, jnp.exp(jnp.maximum(diff, 0.0)[None] * log_g[:, None, None]), 0.0)
    q_dec = jnp.exp((idx[:, None] + 1.0) * log_g[None, :])
    k_dec = jnp.exp((c - 1.0 - idx[:, None]) * log_g[None, :])
    c_dec = jnp.exp(c * log_g)

    def step(S, inp):
        qc, kc, vc = inp
        att = jnp.einsum('bihd,bjhd->bhij', qc, kc) * dmat
        o = (jnp.einsum('bhij,bjhe->bihe', att, vc)
             + jnp.einsum('bihd,bhde->bihe', qc * q_dec[None, :, :, None], S))
        S = S * c_dec[None, :, None, None] + jnp.einsum('bjhd,bjhe->bhde', kc * k_dec[None, :, :, None], vc)
        return S, o

    S, o = lax.scan(step, s0, (_to_chunks(q, c), _to_chunks(k, c), _to_chunks(v, c)))
    return _from_chunks(o), S


def _ssd(x, dt, a, bm, cm, s0):
    b, L = x.shape[:2]
    c = min(CHUNK, L)
    r = SSM_HEADS // SSM_GROUPS
    x = x.reshape(b, L, SSM_GROUPS, r, SSM_HEADDIM)
    dt = dt.reshape(b, L, SSM_GROUPS, r)
    a = a.reshape(SSM_GROUPS, r)
    ar = jnp.arange(c)
    mask = (ar[:, None] >= ar[None, :])[None, :, :, None, None]

    def step(S, inp):
        xc, dtc, bc, cc = inp
        acum = jnp.cumsum(dtc * a, axis=1)
        seg = acum[:, :, None] - acum[:, None, :]
        lmat = jnp.where(mask, jnp.exp(jnp.where(mask, seg, 0.0)), 0.0)
        cb = jnp.einsum('bign,bjgn->bijg', cc, bc)
        w = lmat * cb[..., None] * dtc[:, None]
        y = jnp.einsum('bijgr,bjgrp->bigrp', w, xc)
        y = y + jnp.einsum('bign,bgrpn->bigrp', cc, S) * jnp.exp(acum)[..., None]
        wend = jnp.exp(acum[:, -1:] - acum) * dtc
        S = (S * jnp.exp(acum[:, -1])[..., None, None]
             + jnp.einsum('bjgn,bjgr,bjgrp->bgrpn', bc, wend, xc))
        return S, y

    S0 = s0.reshape(b, SSM_GROUPS, r, SSM_HEADDIM, SSM_STATE)
    S, y = lax.scan(step, S0, (_to_chunks(x, c), _to_chunks(dt, c), _to_chunks(bm, c), _to_chunks(cm, c)))
    y = _from_chunks(y).reshape(b, L, SSM_HEADS, SSM_HEADDIM)
    return y, S.reshape(b, SSM_HEADS, SSM_HEADDIM, SSM_STATE)


def _causal_dwconv(xpad, w, bias):
    y = lax.conv_general_dilated(xpad, w.astype(xpad.dtype)[:, None, :], window_strides=(1,), padding='VALID',
                                 dimension_numbers=('NWC', 'WIO', 'NWC'), feature_group_count=xpad.shape[-1])
    return y + bias.astype(xpad.dtype)


def _layer(x, p, pos, s_ret, s_ssm, s_conv,
           g_ffn1, w1_gu, w1_down, g_mix, w_in, ret_gn_g, ret_gn_b, conv_w, conv_b, dt_bias, a_log, d_skip,
           ssm_norm_g, w_br_ret, w_br_ssm, w_out, g_ffn2, w2_gu, w2_down, g_ple, w_ple, w_ple_gate):
    f32 = jnp.float32
    b, L = x.shape[:2]
    x = x + 0.5 * _swiglu(_rmsnorm(x, g_ffn1), w1_gu, w1_down)

    h = _rmsnorm(x, g_mix)
    split_idx = np.cumsum(IN_SPLITS)[:-1].tolist()
    q, k, v, rg, z, xbc, dt_raw, gate_ret, gate_ssm = jnp.split(h @ w_in, split_idx, axis=-1)

    q = _rope(q.reshape(b, L, RET_HEADS, RET_DK), pos)
    k = _rope(k.reshape(b, L, RET_HEADS, RET_DK), pos) * (RET_DK ** -0.5)
    v = v.reshape(b, L, RET_HEADS, RET_DV).astype(f32)
    o, s_ret_new = _retention(q, k, v, s_ret.astype(f32))
    mu = jnp.mean(o, axis=-1, keepdims=True)
    var = jnp.mean(jnp.square(o - mu), axis=-1, keepdims=True)
    on = ((o - mu) * lax.rsqrt(var + EPS)).reshape(b, L, RET_V)
    ret = (jax.nn.silu(rg.astype(f32)) * (on * ret_gn_g.astype(f32) + ret_gn_b.astype(f32))).astype(x.dtype)

    xpad = jnp.concatenate([s_conv.astype(xbc.dtype), xbc], axis=1)
    s_conv_new = xpad[:, -(SSM_CONV - 1):]
    xbc = jax.nn.silu(_causal_dwconv(xpad, conv_w, conv_b))
    xs, bm, cm = jnp.split(xbc, [SSM_INNER, SSM_INNER + SSM_GROUPS * SSM_STATE], axis=-1)
    dt = jax.nn.softplus(dt_raw.astype(f32) + dt_bias.astype(f32))
    a = -jnp.exp(a_log.astype(f32))
    xs4 = xs.reshape(b, L, SSM_HEADS, SSM_HEADDIM).astype(f32)
    y, s_ssm_new = _ssd(xs4, dt, a,
                        bm.reshape(b, L, SSM_GROUPS, SSM_STATE).astype(f32),
                        cm.reshape(b, L, SSM_GROUPS, SSM_STATE).astype(f32),
                        s_ssm.astype(f32))
    y = (y + d_skip.astype(f32)[:, None] * xs4).reshape(b, L, SSM_INNER) * jax.nn.silu(z.astype(f32))
    yg = y.reshape(b, L, SSM_GROUPS, SSM_INNER // SSM_GROUPS)
    yg = yg * lax.rsqrt(jnp.mean(yg * yg, axis=-1, keepdims=True) + EPS)
    ssm = (yg.reshape(b, L, SSM_INNER) * ssm_norm_g.astype(f32)).astype(x.dtype)

    merged = jax.nn.sigmoid(gate_ret) * (ret @ w_br_ret) + jax.nn.sigmoid(gate_ssm) * (ssm @ w_br_ssm)
    x = x + merged @ w_out

    x = x + 0.5 * _swiglu(_rmsnorm(x, g_ffn2), w2_gu, w2_down)

    x = x + (p.astype(x.dtype) @ w_ple) * jax.nn.sigmoid(_rmsnorm(x, g_ple) @ w_ple_gate)
    return x, s_ret_new, s_ssm_new, s_conv_new


def _trunk(x, p, pos, s_ret, s_ssm, s_conv, layer_params, g_final):
    rets, ssms, convs = [], [], []
    for i in range(DEPTH):
        lw = [w[i] for w in layer_params]
        x, r_new, s_new, c_new = _layer(x, p[i], pos, s_ret[i], s_ssm[i], s_conv[i], *lw)
        rets.append(r_new)
        ssms.append(s_new)
        convs.append(c_new)
    return _rmsnorm(x, g_final), jnp.stack(rets), jnp.stack(ssms), jnp.stack(convs)


def setup_inputs(seed: int = 0) -> dict:
    key = jax.random.key(seed)
    ks = iter(jax.random.split(key, 40))
    f32 = jnp.float32

    def nrm(shape, scale):
        return jax.random.normal(next(ks), shape, f32) * scale

    def gain(shape):
        return 1.0 + nrm(shape, 0.02)

    dt0 = jnp.exp(jax.random.uniform(next(ks), (DEPTH, SSM_HEADS), f32, np.log(1e-3), np.log(1e-1)))
    return {
        "x_prompt": nrm((BATCH, SEQ, D_MODEL), 1.0),
        "x_sample": nrm((DEC_BATCH, DEC_SEQ, D_MODEL), 1.0),
        "state_ret": nrm((DEPTH, DEC_BATCH, RET_HEADS, RET_DK, RET_DV), 0.1),
        "state_ssm": nrm((DEPTH, DEC_BATCH, SSM_HEADS, SSM_HEADDIM, SSM_STATE), 0.1),
        "state_conv": nrm((DEPTH, DEC_BATCH, SSM_CONV - 1, SSM_CONV_DIM), 1.0),
        "p_prompt": nrm((DEPTH, BATCH, SEQ, PLE_DIM), 1.0),
        "p_sample": nrm((DEPTH, DEC_BATCH, DEC_SEQ, PLE_DIM), 1.0),
        "g_ffn1": gain((DEPTH, D_MODEL)),
        "w1_gu": nrm((DEPTH, D_MODEL, 2 * D_FF), D_MODEL ** -0.5),
        "w1_down": nrm((DEPTH, D_FF, D_MODEL), D_FF ** -0.5),
        "g_mix": gain((DEPTH, D_MODEL)),
        "w_in": nrm((DEPTH, D_MODEL, N_IN), D_MODEL ** -0.5),
        "ret_gn_g": gain((DEPTH, RET_V)),
        "ret_gn_b": nrm((DEPTH, RET_V), 0.02),
        "conv_w": nrm((DEPTH, SSM_CONV, SSM_CONV_DIM), SSM_CONV ** -0.5),
        "conv_b": nrm((DEPTH, SSM_CONV_DIM), 0.02),
        "dt_bias": dt0 + jnp.log(-jnp.expm1(-dt0)),
        "a_log": jnp.log(jax.random.uniform(next(ks), (DEPTH, SSM_HEADS), f32, 1.0, 16.0)),
        "d_skip": 1.0 + nrm((DEPTH, SSM_HEADS), 0.1),
        "ssm_norm_g": gain((DEPTH, SSM_INNER)),
        "w_br_ret": nrm((DEPTH, RET_V, D_MODEL), RET_V ** -0.5),
        "w_br_ssm": nrm((DEPTH, SSM_INNER, D_MODEL), SSM_INNER ** -0.5),
        "w_out": nrm((DEPTH, D_MODEL, D_MODEL), D_MODEL ** -0.5),
        "g_ffn2": gain((DEPTH, D_MODEL)),
        "w2_gu": nrm((DEPTH, D_MODEL, 2 * D_FF), D_MODEL ** -0.5),
        "w2_down": nrm((DEPTH, D_FF, D_MODEL), D_FF ** -0.5),
        "g_ple": gain((DEPTH, D_MODEL)),
        "w_ple": nrm((DEPTH, PLE_DIM, D_MODEL), PLE_DIM ** -0.5),
        "w_ple_gate": nrm((DEPTH, D_MODEL, D_MODEL), D_MODEL ** -0.5),
        "g_final": gain((D_MODEL,)),
    }


def reference(x_prompt, x_sample, state_ret, state_ssm, state_conv, p_prompt, p_sample,
              g_ffn1, w1_gu, w1_down, g_mix, w_in, ret_gn_g, ret_gn_b, conv_w, conv_b, dt_bias, a_log, d_skip,
              ssm_norm_g, w_br_ret, w_br_ssm, w_out, g_ffn2, w2_gu, w2_down, g_ple, w_ple, w_ple_gate, g_final):
    layer_params = (g_ffn1, w1_gu, w1_down, g_mix, w_in, ret_gn_g, ret_gn_b, conv_w, conv_b, dt_bias, a_log,
                    d_skip, ssm_norm_g, w_br_ret, w_br_ssm, w_out, g_ffn2, w2_gu, w2_down, g_ple, w_ple, w_ple_gate)
    bp, lp = x_prompt.shape[:2]
    bs, ls = x_sample.shape[:2]
    pos_p = jnp.arange(lp, dtype=jnp.int32)
    zr = jnp.zeros((DEPTH, bp, RET_HEADS, RET_DK, RET_DV), jnp.float32)
    zs = jnp.zeros((DEPTH, bp, SSM_HEADS, SSM_HEADDIM, SSM_STATE), jnp.float32)
    zc = jnp.zeros((DEPTH, bp, SSM_CONV - 1, SSM_CONV_DIM), x_prompt.dtype)
    y_prompt, ret_p, ssm_p, conv_p = _trunk(x_prompt, p_prompt, pos_p, zr, zs, zc, layer_params, g_final)
    pos_s = PAST_LEN + jnp.arange(ls, dtype=jnp.int32)
    y_sample, ret_s, ssm_s, conv_s = _trunk(x_sample, p_sample, pos_s, state_ret, state_ssm, state_conv,
                                            layer_params, g_final)
    return (y_prompt, y_sample, ret_p, ssm_p, conv_p, ret_s, ssm_s, conv_s)
```

```python
import functools

import numpy as np
import jax
import jax.numpy as jnp
from jax import lax
from jax.experimental import pallas as pl
from jax.experimental.pallas import tpu as pltpu

F32 = jnp.float32
BF16 = jnp.bfloat16

D_MODEL = 2048
PAST_LEN = 4096
EPS = 1e-6
RET_HEADS = 8
RET_DK = 256
RET_DV = 512
RET_QK = RET_HEADS * RET_DK
RET_V = RET_HEADS * RET_DV
ROPE_THETA = 10000.0
SSM_INNER = 2 * D_MODEL
SSM_HEADDIM = 64
SSM_HEADS = SSM_INNER // SSM_HEADDIM
SSM_GROUPS = 8
SSM_HPG = SSM_HEADS // SSM_GROUPS
SSM_GW = SSM_HPG * SSM_HEADDIM
SSM_STATE = 128
SSM_CONV = 4
SSM_CONV_DIM = SSM_INNER + 2 * SSM_GROUPS * SSM_STATE
PLE_DIM = 256

COL_Q = 0
COL_K = COL_Q + RET_QK
COL_V = COL_K + RET_QK
COL_RG = COL_V + RET_V
COL_Z = COL_RG + RET_V
COL_XBC = COL_Z + SSM_INNER
N_BIG = COL_XBC + SSM_CONV_DIM
COL_DT = N_BIG
COL_GR = COL_DT + SSM_HEADS
COL_GS = COL_GR + D_MODEL

SUBLANES = 8
HALF_LANES = 64
VMEM_LIMIT = 56 * 1024 * 1024


def _cparams(sem):
    return pltpu.CompilerParams(dimension_semantics=sem, vmem_limit_bytes=VMEM_LIMIT)


def _rms_bf16(x, g):
    ms = jnp.mean(x * x, axis=-1, keepdims=True)
    return (x * lax.rsqrt(ms + EPS) * g).astype(BF16)


def _silu(x):
    return x * jax.nn.sigmoid(x)


def _dot(a, b):
    return jnp.dot(a, b, preferred_element_type=F32)


def _dot_nt(a, b):
    return lax.dot_general(a, b, (((1,), (1,)), ((), ())), preferred_element_type=F32)


def _dot_tn(a, b):
    return lax.dot_general(a, b, (((0,), (0,)), ((), ())), preferred_element_type=F32)


def _ffn_kernel(x_ref, g_ref, gn_ref, wg_ref, wu_ref, wd_ref, o_ref, hn_ref, h_scr, *, nj):
    j = pl.program_id(1)

    @pl.when(j == 0)
    def _():
        h_scr[...] = _rms_bf16(x_ref[...], g_ref[...])
        o_ref[...] = jnp.zeros_like(o_ref)

    h = h_scr[...]
    gate = _dot(h, wg_ref[...])
    up = _dot(h, wu_ref[...])
    act = (_silu(gate) * up).astype(BF16)
    o_ref[...] += _dot(act, wd_ref[...])

    @pl.when(j == nj - 1)
    def _():
        xn = x_ref[...] + 0.5 * o_ref[...]
        o_ref[...] = xn
        hn_ref[...] = _rms_bf16(xn, gn_ref[...])


def _ffn(x, g, g_next, w_gu, w_down, *, tm, tf):
    n, d = x.shape
    f = w_down.shape[0]
    nj = f // tf
    return pl.pallas_call(
        functools.partial(_ffn_kernel, nj=nj),
        out_shape=(jax.ShapeDtypeStruct((n, d), F32), jax.ShapeDtypeStruct((n, d), BF16)),
        grid=(n // tm, nj),
        in_specs=[
            pl.BlockSpec((tm, d), lambda i, j: (i, 0)),
            pl.BlockSpec((1, d), lambda i, j: (0, 0)),
            pl.BlockSpec((1, d), lambda i, j: (0, 0)),
            pl.BlockSpec((d, tf), lambda i, j: (0, j)),
            pl.BlockSpec((d, tf), lambda i, j: (0, j + nj)),
            pl.BlockSpec((tf, d), lambda i, j: (j, 0)),
        ],
        out_specs=(pl.BlockSpec((tm, d), lambda i, j: (i, 0)),
                   pl.BlockSpec((tm, d), lambda i, j: (i, 0))),
        scratch_shapes=[pltpu.VMEM((tm, d), BF16)],
        compiler_params=_cparams(("parallel", "arbitrary")),
        name="ffn",
    )(x, g, g_next, w_gu, w_gu, w_down)


def _inproj_kernel(h_ref, w_ref, cos_ref, sin_ref, o_ref, *, tn, nq, nrope):
    j = pl.program_id(1)
    acc = _dot(h_ref[...], w_ref[...])

    @pl.when(j >= nrope)
    def _():
        o_ref[...] = acc.astype(BF16)

    @pl.when(j < nrope)
    def _():
        scale = jnp.where(j >= nq, RET_DK ** -0.5, 1.0).astype(F32)
        c = cos_ref[...] * scale
        s = sin_ref[...] * scale
        half = RET_DK // 2
        for hh in range(tn // RET_DK):
            lo = hh * RET_DK
            x1 = acc[:, lo:lo + half]
            x2 = acc[:, lo + half:lo + RET_DK]
            o_ref[:, lo:lo + half] = (x1 * c - x2 * s).astype(BF16)
            o_ref[:, lo + half:lo + RET_DK] = (x2 * c + x1 * s).astype(BF16)


def _inproj(h, w_big, cos, sin, *, tm, tn):
    n, d = h.shape
    nb = w_big.shape[1]
    nrow = cos.shape[0] // tm
    return pl.pallas_call(
        functools.partial(_inproj_kernel, tn=tn, nq=RET_QK // tn, nrope=2 * RET_QK // tn),
        out_shape=jax.ShapeDtypeStruct((n, nb), BF16),
        grid=(n // tm, nb // tn),
        in_specs=[
            pl.BlockSpec((tm, d), lambda i, j: (i, 0)),
            pl.BlockSpec((d, tn), lambda i, j: (0, j)),
            pl.BlockSpec((tm, RET_DK // 2), lambda i, j: (i % nrow, 0)),
            pl.BlockSpec((tm, RET_DK // 2), lambda i, j: (i % nrow, 0)),
        ],
        out_specs=pl.BlockSpec((tm, tn), lambda i, j: (i, j)),
        compiler_params=_cparams(("parallel", "arbitrary")),
        name="inproj",
    )(h, w_big, cos, sin)


def _softplus(x):
    return jnp.maximum(x, 0.0) + jnp.log1p(jnp.exp(-jnp.abs(x)))


def _dt_kernel(h_ref, w_ref, wt_ref, br_ref, bc_ref, ar_ref, ac_ref,
               dtc_ref, acc_ref, dtr_ref, acr_ref, *, q, valid):
    h = h_ref[...]
    tm = h.shape[0]
    raw_c = _dot(h, w_ref[...])
    raw_r = _dot_nt(wt_ref[...], h)
    dt_c = _softplus(raw_c + br_ref[...])
    dt_r = _softplus(raw_r + bc_ref[...])
    if valid < q:
        row = lax.broadcasted_iota(jnp.int32, dt_c.shape, 0) % q
        col = lax.broadcasted_iota(jnp.int32, dt_r.shape, 1) % q
        dt_c = jnp.where(row < valid, dt_c, 0.0)
        dt_r = jnp.where(col < valid, dt_r, 0.0)
    dta_c = dt_c * (-jnp.exp(ar_ref[...]))
    dta_r = dt_r * (-jnp.exp(ac_ref[...]))
    i = lax.broadcasted_iota(jnp.int32, (tm, tm), 0)
    j = lax.broadcasted_iota(jnp.int32, (tm, tm), 1)
    same = (i // q) == (j // q)
    tri = jnp.where(same & (j <= i), 1.0, 0.0).astype(F32)
    trit = jnp.where(same & (i <= j), 1.0, 0.0).astype(F32)
    ac_c = jnp.dot(tri, dta_c, preferred_element_type=F32, precision=lax.Precision.HIGHEST)
    ac_r = jnp.dot(dta_r, trit, preferred_element_type=F32, precision=lax.Precision.HIGHEST)
    for g in range(SSM_GROUPS):
        lo = g * SSM_HPG
        dtc_ref[g] = dt_c[:, lo:lo + SSM_HPG]
        acc_ref[g] = ac_c[:, lo:lo + SSM_HPG]
        dtr_ref[g] = dt_r[lo:lo + SSM_HPG, :]
        acr_ref[g] = ac_r[lo:lo + SSM_HPG, :]


def _dt_proj(h, w_dt, w_dt_t, dt_bias, a_log, *, tm, q, valid):
    n, d = h.shape
    hh = SSM_HEADS
    col = jax.ShapeDtypeStruct((SSM_GROUPS, n, SSM_HPG), F32)
    row = jax.ShapeDtypeStruct((SSM_GROUPS, SSM_HPG, n), F32)
    return pl.pallas_call(
        functools.partial(_dt_kernel, q=q, valid=valid),
        out_shape=(col, col, row, row),
        grid=(n // tm,),
        in_specs=[
            pl.BlockSpec((tm, d), lambda i: (i, 0)),
            pl.BlockSpec((d, hh), lambda i: (0, 0)),
            pl.BlockSpec((hh, d), lambda i: (0, 0)),
            pl.BlockSpec((1, hh), lambda i: (0, 0)),
            pl.BlockSpec((hh, 1), lambda i: (0, 0)),
            pl.BlockSpec((1, hh), lambda i: (0, 0)),
            pl.BlockSpec((hh, 1), lambda i: (0, 0)),
        ],
        out_specs=(
            pl.BlockSpec((SSM_GROUPS, tm, SSM_HPG), lambda i: (0, i, 0)),
            pl.BlockSpec((SSM_GROUPS, tm, SSM_HPG), lambda i: (0, i, 0)),
            pl.BlockSpec((SSM_GROUPS, SSM_HPG, tm), lambda i: (0, 0, i)),
            pl.BlockSpec((SSM_GROUPS, SSM_HPG, tm), lambda i: (0, 0, i)),
        ),
        compiler_params=_cparams(("parallel",)),
        name="dt_proj",
    )(h, w_dt, w_dt_t, dt_bias.reshape(1, hh), dt_bias.reshape(hh, 1),
      a_log.reshape(1, hh), a_log.reshape(hh, 1))


def _ret_kernel(*refs, c, nchunk, has_s0):
    if has_s0:
        (q_ref, k_ref, v_ref, rg_ref, dm_ref, qd_ref, kd_ref, cd_ref, gg_ref, gb_ref, s0_ref,
         o_ref, s_ref) = refs
    else:
        (q_ref, k_ref, v_ref, rg_ref, dm_ref, qd_ref, kd_ref, cd_ref, gg_ref, gb_ref,
         o_ref, s_ref) = refs
    t = pl.program_id(2)

    @pl.when(t == 0)
    def _():
        if has_s0:
            s_ref[...] = s0_ref[...]
        else:
            s_ref[...] = jnp.zeros_like(s_ref)

    dm = dm_ref[0]
    qd = qd_ref[0]
    kd = kd_ref[0]
    cd = cd_ref[0]
    gg = gg_ref[...]
    gb = gb_ref[...]
    for ci in range(nchunk):
        sl = pl.ds(ci * c, c)
        q = q_ref[sl, :]
        k = k_ref[sl, :]
        v = v_ref[sl, :]
        s = s_ref[0, 0]
        att = _dot_nt(q, k) * dm
        o = _dot(att.astype(BF16), v) + _dot(q, s.astype(BF16)) * qd
        kdk = (k.astype(F32) * kd).astype(BF16)
        s_ref[0, 0] = s * cd + _dot_tn(kdk, v)
        mu = jnp.mean(o, axis=-1, keepdims=True)
        dev = o - mu
        var = jnp.mean(dev * dev, axis=-1, keepdims=True)
        on = dev * lax.rsqrt(var + EPS)
        rg = rg_ref[sl, :].astype(F32)
        o_ref[sl, :] = (_silu(rg) * (on * gg + gb)).astype(BF16)


def _ret_tables(c_true, c):
    hs = np.arange(RET_HEADS, dtype=np.float64)
    log_g = np.log1p(-np.exp2(-5.0 - hs))
    idx = np.arange(c, dtype=np.float64)
    diff = idx[:, None] - idx[None, :]
    dmat = np.where(diff[None] >= 0, np.exp(np.maximum(diff, 0.0)[None] * log_g[:, None, None]), 0.0)
    q_dec = np.exp((idx[None, :] + 1.0) * log_g[:, None])
    k_dec = np.where(idx[None, :] < c_true,
                     np.exp(np.maximum(c_true - 1.0 - idx[None, :], 0.0) * log_g[:, None]), 0.0)
    c_dec = np.exp(c_true * log_g)
    qd = np.broadcast_to(q_dec[:, :, None], (RET_HEADS, c, RET_DV))
    kd = np.broadcast_to(k_dec[:, :, None], (RET_HEADS, c, RET_DK))
    cd = np.broadcast_to(c_dec[:, None, None], (RET_HEADS, 1, RET_DV))
    f = lambda a: jnp.asarray(np.ascontiguousarray(a), dtype=F32)
    return f(dmat), f(qd), f(kd), f(cd)


def _retention(big, gn_g, gn_b, s0, *, nseq, seqlen, c_true, c, tb):
    n = big.shape[0]
    nt = seqlen // tb
    nchunk = tb // c
    dm, qd, kd, cd = _ret_tables(c_true, c)
    has_s0 = s0 is not None
    kq, kk = COL_Q // RET_DK, COL_K // RET_DK
    kv, kr = COL_V // RET_DV, COL_RG // RET_DV
    row = lambda b, h, t: b * nt + t
    in_specs = [
        pl.BlockSpec((tb, RET_DK), lambda b, h, t: (row(b, h, t), kq + h)),
        pl.BlockSpec((tb, RET_DK), lambda b, h, t: (row(b, h, t), kk + h)),
        pl.BlockSpec((tb, RET_DV), lambda b, h, t: (row(b, h, t), kv + h)),
        pl.BlockSpec((tb, RET_DV), lambda b, h, t: (row(b, h, t), kr + h)),
        pl.BlockSpec((1, c, c), lambda b, h, t: (h, 0, 0)),
        pl.BlockSpec((1, c, RET_DV), lambda b, h, t: (h, 0, 0)),
        pl.BlockSpec((1, c, RET_DK), lambda b, h, t: (h, 0, 0)),
        pl.BlockSpec((1, 1, RET_DV), lambda b, h, t: (h, 0, 0)),
        pl.BlockSpec((1, RET_DV), lambda b, h, t: (0, h)),
        pl.BlockSpec((1, RET_DV), lambda b, h, t: (0, h)),
    ]
    args = [big, big, big, big, dm, qd, kd, cd, gn_g.reshape(1, RET_V), gn_b.reshape(1, RET_V)]
    st_spec = pl.BlockSpec((1, 1, RET_DK, RET_DV), lambda b, h, t: (b, h, 0, 0))
    if has_s0:
        in_specs.append(st_spec)
        args.append(s0)
    return pl.pallas_call(
        functools.partial(_ret_kernel, c=c, nchunk=nchunk, has_s0=has_s0),
        out_shape=(jax.ShapeDtypeStruct((n, RET_V), BF16),
                   jax.ShapeDtypeStruct((nseq, RET_HEADS, RET_DK, RET_DV), F32)),
        grid=(nseq, RET_HEADS, nt),
        in_specs=in_specs,
        out_specs=(pl.BlockSpec((tb, RET_DV), lambda b, h, t: (row(b, h, t), h)), st_spec),
        compiler_params=_cparams(("parallel", "parallel", "arbitrary")),
        name="retention",
    )(*args)


def _conv_silu(x, hist, cw, bias):
    taps = SSM_CONV
    acc = x * cw[taps - 1:taps, :] + bias
    x8 = x[0:SUBLANES, :]
    fix = x8 * cw[taps - 1:taps, :] + bias
    rows = lax.broadcasted_iota(jnp.int32, x8.shape, 0)
    for s in range(1, taps):
        w_s = cw[taps - 1 - s:taps - s, :]
        acc = acc + pltpu.roll(x, s, 0) * w_s
        fix = fix + jnp.where(rows < s, pltpu.roll(hist, s, 0), pltpu.roll(x8, s, 0)) * w_s
    y = jnp.concatenate([fix, acc[SUBLANES:, :]], axis=0)
    return _silu(y)


def _lane_expand(v, width):
    rows = v.shape[0]
    lane = lax.broadcasted_iota(jnp.int32, (rows, 2 * width), 1)
    parts = []
    for m in range(v.shape[1] // 2):
        a = jnp.broadcast_to(v[:, 2 * m:2 * m + 1], (rows, 2 * width))
        b = jnp.broadcast_to(v[:, 2 * m + 1:2 * m + 2], (rows, 2 * width))
        parts.append(jnp.where(lane < width, a, b))
    return jnp.concatenate(parts, axis=1)


def _ssd_kernel(*refs, q, nchunk, has_s0):
    if has_s0:
        (xs_ref, b_ref, c_ref, z_ref, dtc_ref, acc_ref, dtr_ref, acr_ref,
         cwx_ref, cwb_ref, cwc_ref, cbx_ref, cbb_ref, cbc_ref, dsk_ref, ng_ref,
         s0_ref, hx0_ref, hb0_ref, hc0_ref,
         y_ref, s_ref, hx, hb, hc, xc_scr, bc_scr, cc_scr) = refs
    else:
        (xs_ref, b_ref, c_ref, z_ref, dtc_ref, acc_ref, dtr_ref, acr_ref,
         cwx_ref, cwb_ref, cwc_ref, cbx_ref, cbb_ref, cbc_ref, dsk_ref, ng_ref,
         y_ref, s_ref, hx, hb, hc, xc_scr, bc_scr, cc_scr) = refs
    t = pl.program_id(2)

    @pl.when(t == 0)
    def _():
        if has_s0:
            s_ref[...] = s0_ref[...]
            hx[...] = hx0_ref[0]
            hb[...] = hb0_ref[0]
            hc[...] = hc0_ref[0]
        else:
            s_ref[...] = jnp.zeros_like(s_ref)
            hx[...] = jnp.zeros_like(hx)
            hb[...] = jnp.zeros_like(hb)
            hc[...] = jnp.zeros_like(hc)

    tb = xs_ref.shape[0]
    for raw_ref, hist, cw_ref, cb_ref, dst in ((xs_ref, hx, cwx_ref, cbx_ref, xc_scr),
                                               (b_ref, hb, cwb_ref, cbb_ref, bc_scr),
                                               (c_ref, hc, cwc_ref, cbc_ref, cc_scr)):
        raw = raw_ref[...].astype(F32)
        dst[...] = _conv_silu(raw, hist[...], cw_ref[...], cb_ref[...])
        hist[...] = raw[tb - SUBLANES:tb, :]

    ii = lax.broadcasted_iota(jnp.int32, (q, q), 0)
    jj = lax.broadcasted_iota(jnp.int32, (q, q), 1)
    causal = ii >= jj
    lane = lax.broadcasted_iota(jnp.int32, (q, 2 * HALF_LANES), 1)
    dsk = dsk_ref[...]
    ng = ng_ref[...]
    for ci in range(nchunk):
        sl = pl.ds(ci * q, q)
        x = xc_scr[sl, :]
        bq = bc_scr[sl, :].astype(BF16)
        cq = cc_scr[sl, :].astype(BF16)
        dtc = dtc_ref[0, sl, :]
        acc = acc_ref[0, sl, :]
        dtr = dtr_ref[0, :, sl]
        acr = acr_ref[0, :, sl]
        state = s_ref[0, 0]

        cb = _dot_nt(cq, bq)
        pairs = []
        for m in range(SSM_HPG // 2):
            ws = []
            for r in (2 * m, 2 * m + 1):
                seg = acc[:, r:r + 1] - acr[r:r + 1, :]
                lmat = jnp.where(causal, jnp.exp(jnp.where(causal, seg, 0.0)), 0.0)
                ws.append((lmat * cb * dtr[r:r + 1, :]).astype(BF16))
            xp = x[:, m * 2 * HALF_LANES:(m + 1) * 2 * HALF_LANES]
            x_lo = jnp.where(lane < HALF_LANES, xp, 0.0).astype(BF16)
            x_hi = jnp.where(lane >= HALF_LANES, xp, 0.0).astype(BF16)
            pairs.append(_dot(jnp.concatenate(ws, axis=1), jnp.concatenate([x_lo, x_hi], axis=0)))
        y = jnp.concatenate(pairs, axis=1)

        ea = _lane_expand(jnp.exp(acc), SSM_HEADDIM)
        y = y + _dot(cq, state.astype(BF16)) * ea
        wend = jnp.exp(acc[q - 1:q, :] - acc) * dtc
        xw = (x * _lane_expand(wend, SSM_HEADDIM)).astype(BF16)
        s_ref[0, 0] = state * ea[q - 1:q, :] + _dot_tn(bq, xw)

        y = y + dsk * x
        y = y * _silu(z_ref[sl, :].astype(F32))
        ms = jnp.mean(y * y, axis=-1, keepdims=True)
        y_ref[sl, :] = (y * lax.rsqrt(ms + EPS) * ng).astype(BF16)


def _ssd(big, dts, conv_w, conv_b, d_skip_x, norm_g, s0, hist0, *, nseq, seqlen, q, tb):
    n = big.shape[0]
    nt = seqlen // tb
    nchunk = tb // q
    has_s0 = s0 is not None
    dtc, acc, dtr, acr = dts
    gw, ns = SSM_GW, SSM_STATE
    kz, kx = COL_Z // gw, COL_XBC // gw
    kb = (COL_XBC + SSM_INNER) // ns
    kc = kb + SSM_GROUPS
    cb_x, cb_b = 0, SSM_INNER // ns
    cb_c = cb_b + SSM_GROUPS
    row = lambda b, g, t: b * nt + t
    in_specs = [
        pl.BlockSpec((tb, gw), lambda b, g, t: (row(b, g, t), kx + g)),
        pl.BlockSpec((tb, ns), lambda b, g, t: (row(b, g, t), kb + g)),
        pl.BlockSpec((tb, ns), lambda b, g, t: (row(b, g, t), kc + g)),
        pl.BlockSpec((tb, gw), lambda b, g, t: (row(b, g, t), kz + g)),
        pl.BlockSpec((1, tb, SSM_HPG), lambda b, g, t: (g, row(b, g, t), 0)),
        pl.BlockSpec((1, tb, SSM_HPG), lambda b, g, t: (g, row(b, g, t), 0)),
        pl.BlockSpec((1, SSM_HPG, tb), lambda b, g, t: (g, 0, row(b, g, t))),
        pl.BlockSpec((1, SSM_HPG, tb), lambda b, g, t: (g, 0, row(b, g, t))),
        pl.BlockSpec((SSM_CONV, gw), lambda b, g, t: (0, g)),
        pl.BlockSpec((SSM_CONV, ns), lambda b, g, t: (0, cb_b + g)),
        pl.BlockSpec((SSM_CONV, ns), lambda b, g, t: (0, cb_c + g)),
        pl.BlockSpec((1, gw), lambda b, g, t: (0, g)),
        pl.BlockSpec((1, ns), lambda b, g, t: (0, cb_b + g)),
        pl.BlockSpec((1, ns), lambda b, g, t: (0, cb_c + g)),
        pl.BlockSpec((1, gw), lambda b, g, t: (0, g)),
        pl.BlockSpec((1, gw), lambda b, g, t: (0, g)),
    ]
    cbias = conv_b.reshape(1, SSM_CONV_DIM)
    args = [big, big, big, big, dtc, acc, dtr, acr, conv_w, conv_w, conv_w, cbias, cbias, cbias,
            d_skip_x, norm_g.reshape(1, SSM_INNER)]
    st_spec = pl.BlockSpec((1, 1, ns, gw), lambda b, g, t: (b, g, 0, 0))
    if has_s0:
        in_specs += [
            st_spec,
            pl.BlockSpec((1, SUBLANES, gw), lambda b, g, t: (b, 0, g)),
            pl.BlockSpec((1, SUBLANES, ns), lambda b, g, t: (b, 0, cb_b + g)),
            pl.BlockSpec((1, SUBLANES, ns), lambda b, g, t: (b, 0, cb_c + g)),
        ]
        args += [s0, hist0, hist0, hist0]
    return pl.pallas_call(
        functools.partial(_ssd_kernel, q=q, nchunk=nchunk, has_s0=has_s0),
        out_shape=(jax.ShapeDtypeStruct((n, SSM_INNER), BF16),
                   jax.ShapeDtypeStruct((nseq, SSM_GROUPS, ns, gw), F32)),
        grid=(nseq, SSM_GROUPS, nt),
        in_specs=in_specs,
        out_specs=(pl.BlockSpec((tb, gw), lambda b, g, t: (row(b, g, t), g)), st_spec),
        scratch_shapes=[pltpu.VMEM((SUBLANES, gw), F32), pltpu.VMEM((SUBLANES, ns), F32),
                        pltpu.VMEM((SUBLANES, ns), F32), pltpu.VMEM((tb, gw), F32),
                        pltpu.VMEM((tb, ns), F32), pltpu.VMEM((tb, ns), F32)],
        compiler_params=_cparams(("parallel", "parallel", "arbitrary")),
        name="ssd",
    )(*args)


def _merge_kernel(h_ref, ret_ref, ssm_ref, wgr_ref, wgs_ref, wr_ref, ws_ref, o_ref):
    h = h_ref[...]
    gr = jax.nn.sigmoid(_dot(h, wgr_ref[...]))
    gs = jax.nn.sigmoid(_dot(h, wgs_ref[...]))
    a = _dot(ret_ref[...], wr_ref[...])
    b = _dot(ssm_ref[...], ws_ref[...])
    o_ref[...] = (gr * a + gs * b).astype(BF16)


def _merge(h, ret, ssm, w_gr, w_gs, w_r, w_s, *, tm, tn):
    n, d = h.shape
    kr, ks = ret.shape[1], ssm.shape[1]
    return pl.pallas_call(
        _merge_kernel,
        out_shape=jax.ShapeDtypeStruct((n, d), BF16),
        grid=(n // tm, d // tn),
        in_specs=[
            pl.BlockSpec((tm, d), lambda i, j: (i, 0)),
            pl.BlockSpec((tm, kr), lambda i, j: (i, 0)),
            pl.BlockSpec((tm, ks), lambda i, j: (i, 0)),
            pl.BlockSpec((d, tn), lambda i, j: (0, j)),
            pl.BlockSpec((d, tn), lambda i, j: (0, j)),
            pl.BlockSpec((kr, tn), lambda i, j: (0, j)),
            pl.BlockSpec((ks, tn), lambda i, j: (0, j)),
        ],
        out_specs=pl.BlockSpec((tm, tn), lambda i, j: (i, j)),
        compiler_params=_cparams(("parallel", "arbitrary")),
        name="merge",
    )(h, ret, ssm, w_gr, w_gs, w_r, w_s)


def _outproj_kernel(m_ref, w_ref, x_ref, o_ref):
    o_ref[...] = x_ref[...] + _dot(m_ref[...], w_ref[...])


def _outproj(m, w, x, *, tm, tn):
    n, d = x.shape
    k = m.shape[1]
    return pl.pallas_call(
        _outproj_kernel,
        out_shape=jax.ShapeDtypeStruct((n, d), F32),
        grid=(n // tm, d // tn),
        in_specs=[
            pl.BlockSpec((tm, k), lambda i, j: (i, 0)),
            pl.BlockSpec((k, tn), lambda i, j: (0, j)),
            pl.BlockSpec((tm, tn), lambda i, j: (i, j)),
        ],
        out_specs=pl.BlockSpec((tm, tn), lambda i, j: (i, j)),
        compiler_params=_cparams(("parallel", "arbitrary")),
        name="outproj",
    )(m, w, x)


def _ple_kernel(x_ref, h_ref, p_ref, wp_ref, wg_ref, gf_ref, o_ref):
    pe = _dot(p_ref[...].astype(BF16), wp_ref[...])
    gt = jax.nn.sigmoid(_dot(h_ref[...], wg_ref[...]))
    x = x_ref[...] + pe * gt
    ms = jnp.mean(x * x, axis=-1, keepdims=True)
    o_ref[...] = x * lax.rsqrt(ms + EPS) * gf_ref[...]


def _ple(x, h, p, w_ple, w_gate, g_final, *, tm):
    n, d = x.shape
    pd = p.shape[1]
    return pl.pallas_call(
        _ple_kernel,
        out_shape=jax.ShapeDtypeStruct((n, d), F32),
        grid=(n // tm,),
        in_specs=[
            pl.BlockSpec((tm, d), lambda i: (i, 0)),
            pl.BlockSpec((tm, d), lambda i: (i, 0)),
            pl.BlockSpec((tm, pd), lambda i: (i, 0)),
            pl.BlockSpec((pd, d), lambda i: (0, 0)),
            pl.BlockSpec((d, d), lambda i: (0, 0)),
            pl.BlockSpec((1, d), lambda i: (0, 0)),
        ],
        out_specs=pl.BlockSpec((tm, d), lambda i: (i, 0)),
        compiler_params=_cparams(("parallel",)),
        name="ple",
    )(x, h, p, w_ple, w_gate, g_final)


def _rope_tables(pos0, seqlen, rows):
    half = RET_DK // 2
    inv = ROPE_THETA ** (-jnp.arange(half, dtype=F32) / half)
    pos = (pos0 + jnp.arange(seqlen, dtype=jnp.int32)).astype(F32)
    ang = pos[:, None] * inv[None, :]
    reps = max(rows // seqlen, 1)
    return jnp.tile(jnp.cos(ang), (reps, 1)), jnp.tile(jnp.sin(ang), (reps, 1))


def _pick(n, pref):
    t = pref
    while n % t:
        t //= 2
    return t


def _trunk(x, p, pos0, s_ret, s_ssm, s_conv, w, cfg):
    nseq, seqlen, d = x.shape
    n = nseq * seqlen
    x = x.reshape(n, d)
    p = p.reshape(n, p.shape[-1])
    tm = _pick(n, cfg["tm"])

    x1, h = _ffn(x, w["g_ffn1"], w["g_mix"], w["w1_gu"], w["w1_down"], tm=_pick(n, cfg["tm_ffn"]), tf=cfg["tf"])

    cos, sin = _rope_tables(pos0, seqlen, tm)
    big = _inproj(h, w["w_big"], cos, sin, tm=tm, tn=cfg["tn_in"])

    q = cfg["q_ssd"]
    lpad = -(-seqlen // q) * q
    if lpad != seqlen:
        pad = lambda a: jnp.pad(a.reshape(nseq, seqlen, -1), ((0, 0), (0, lpad - seqlen), (0, 0))).reshape(nseq * lpad, -1)
        big_m, h_m = pad(big), pad(h)
    else:
        big_m, h_m = big, h
    npad = nseq * lpad
    valid = min(seqlen, q)

    dts = _dt_proj(h_m, w["w_dt"], w["w_dt_t"], w["dt_bias"], w["a_log"], tm=_pick(npad, 512), q=q, valid=valid)

    c = cfg["c_ret"] if lpad % cfg["c_ret"] == 0 else q
    ret, s_ret_new = _retention(big_m, w["ret_gn_g"], w["ret_gn_b"], s_ret, nseq=nseq, seqlen=lpad,
                                c_true=min(seqlen, c), c=c, tb=_pick(lpad, cfg["tb_ret"]))

    if s_ssm is not None:
        s0 = s_ssm.reshape(nseq, SSM_GROUPS, SSM_HPG, SSM_HEADDIM, SSM_STATE)
        s0 = s0.transpose(0, 1, 4, 2, 3).reshape(nseq, SSM_GROUPS, SSM_STATE, SSM_GW)
        hist0 = jnp.pad(s_conv, ((0, 0), (SUBLANES - (SSM_CONV - 1), 0), (0, 0)))
    else:
        s0, hist0 = None, None
    ssm, s_ssm_new = _ssd(big_m, dts, w["conv_w"], w["conv_b"], w["d_skip_x"], w["ssm_norm_g"], s0, hist0,
                          nseq=nseq, seqlen=lpad, q=q, tb=_pick(lpad, cfg["tb_ssd"]))
    s_ssm_new = s_ssm_new.reshape(nseq, SSM_GROUPS, SSM_STATE, SSM_HPG, SSM_HEADDIM)
    s_ssm_new = s_ssm_new.transpose(0, 1, 3, 4, 2).reshape(nseq, SSM_HEADS, SSM_HEADDIM, SSM_STATE)

    if lpad != seqlen:
        unpad = lambda a: a.reshape(nseq, lpad, -1)[:, :seqlen].reshape(n, -1)
        ret, ssm = unpad(ret), unpad(ssm)

    keep = min(seqlen, SSM_CONV - 1)
    xbc_tail = big.reshape(nseq, seqlen, N_BIG)[:, seqlen - keep:, COL_XBC:].astype(F32)
    if keep < SSM_CONV - 1:
        prev = jnp.zeros((nseq, SSM_CONV - 1, SSM_CONV_DIM), F32) if s_conv is None else s_conv.astype(F32)
        xbc_tail = jnp.concatenate([prev, xbc_tail], axis=1)[:, -(SSM_CONV - 1):]

    merged = _merge(h, ret, ssm, w["w_gr"], w["w_gs"], w["w_br_ret"], w["w_br_ssm"], tm=_pick(n, cfg["tm_mg"]), tn=cfg["tn_mg"])
    x2 = _outproj(merged, w["w_out"], x1, tm=tm, tn=cfg["tn_out"])
    x3, h3 = _ffn(x2, w["g_ffn2"], w["g_ple"], w["w2_gu"], w["w2_down"], tm=_pick(n, cfg["tm_ffn"]), tf=cfg["tf"])
    y = _ple(x3, h3, p, w["w_ple"], w["w_ple_gate"], w["g_final"], tm=_pick(n, cfg["tm_ple"]))
    return y.reshape(nseq, seqlen, d), s_ret_new, s_ssm_new, xbc_tail


CFG = dict(tm=1024, tm_ffn=512, tf=512, tn_in=1024, q_ssd=128, c_ret=256, tb_ret=1024, tb_ssd=256,
           tm_mg=512, tn_mg=256, tn_out=1024, tm_ple=256)


def _prep_weights(g_ffn1, w1_gu, w1_down, g_mix, w_in, ret_gn_g, ret_gn_b, conv_w, conv_b, dt_bias, a_log,
                  d_skip, ssm_norm_g, w_br_ret, w_br_ssm, w_out, g_ffn2, w2_gu, w2_down, g_ple, w_ple,
                  w_ple_gate, g_final):
    b = lambda a: a[0].astype(BF16)
    r = lambda a: a[0].reshape(1, -1).astype(F32)
    win = w_in[0]
    w_dt = win[:, COL_DT:COL_DT + SSM_HEADS].astype(BF16)
    return dict(
        g_ffn1=r(g_ffn1), w1_gu=b(w1_gu), w1_down=b(w1_down), g_mix=r(g_mix),
        w_big=win[:, :N_BIG].astype(BF16), w_dt=w_dt, w_dt_t=w_dt.T,
        w_gr=win[:, COL_GR:COL_GR + D_MODEL].astype(BF16), w_gs=win[:, COL_GS:COL_GS + D_MODEL].astype(BF16),
        ret_gn_g=ret_gn_g[0].astype(F32), ret_gn_b=ret_gn_b[0].astype(F32),
        conv_w=conv_w[0].astype(F32), conv_b=conv_b[0].astype(F32),
        dt_bias=dt_bias[0].astype(F32), a_log=a_log[0].astype(F32),
        d_skip_x=jnp.repeat(d_skip[0].astype(F32), SSM_HEADDIM).reshape(1, SSM_INNER),
        ssm_norm_g=ssm_norm_g[0].astype(F32),
        w_br_ret=b(w_br_ret), w_br_ssm=b(w_br_ssm), w_out=b(w_out),
        g_ffn2=r(g_ffn2), w2_gu=b(w2_gu), w2_down=b(w2_down), g_ple=r(g_ple),
        w_ple=b(w_ple), w_ple_gate=b(w_ple_gate), g_final=g_final.reshape(1, -1).astype(F32),
    )


def kernel(x_prompt, x_sample, state_ret, state_ssm, state_conv, p_prompt, p_sample, g_ffn1, w1_gu, w1_down, g_mix, w_in, ret_gn_g, ret_gn_b, conv_w, conv_b, dt_bias, a_log, d_skip, ssm_norm_g, w_br_ret, w_br_ssm, w_out, g_ffn2, w2_gu, w2_down, g_ple, w_ple, w_ple_gate, g_final):
    assert g_ffn1.shape[0] == 1, "single-layer trunk"
    w = _prep_weights(g_ffn1, w1_gu, w1_down, g_mix, w_in, ret_gn_g, ret_gn_b, conv_w, conv_b, dt_bias, a_log,
                      d_skip, ssm_norm_g, w_br_ret, w_br_ssm, w_out, g_ffn2, w2_gu, w2_down, g_ple, w_ple,
                      w_ple_gate, g_final)
    y_p, ret_p, ssm_p, conv_p = _trunk(x_prompt, p_prompt[0], 0, None, None, None, w, CFG)
    y_s, ret_s, ssm_s, conv_s = _trunk(x_sample, p_sample[0], PAST_LEN, state_ret[0], state_ssm[0],
                                       state_conv[0], w, CFG)
    e = lambda a: a[None]
    return (y_p, y_s, e(ret_p), e(ssm_p), e(conv_p), e(ret_s), e(ssm_s), e(conv_s))
```

```python
import functools

import numpy as np
import jax
import jax.numpy as jnp
from jax import lax
from jax.experimental import pallas as pl
from jax.experimental.pallas import tpu as pltpu

F32 = jnp.float32
BF16 = jnp.bfloat16

D_MODEL = 2048
PAST_LEN = 4096
EPS = 1e-6
RET_HEADS = 8
RET_DK = 256
RET_DV = 512
RET_QK = RET_HEADS * RET_DK
RET_V = RET_HEADS * RET_DV
ROPE_THETA = 10000.0
SSM_INNER = 2 * D_MODEL
SSM_HEADDIM = 64
SSM_HEADS = SSM_INNER // SSM_HEADDIM
SSM_GROUPS = 8
SSM_HPG = SSM_HEADS // SSM_GROUPS
SSM_GW = SSM_HPG * SSM_HEADDIM
SSM_STATE = 128
SSM_CONV = 4
SSM_CONV_DIM = SSM_INNER + 2 * SSM_GROUPS * SSM_STATE
PLE_DIM = 256

COL_Q = 0
COL_K = COL_Q + RET_QK
COL_V = COL_K + RET_QK
COL_RG = COL_V + RET_V
COL_Z = COL_RG + RET_V
COL_XBC = COL_Z + SSM_INNER
N_BIG = COL_XBC + SSM_CONV_DIM
COL_DT = N_BIG
COL_GR = COL_DT + SSM_HEADS
COL_GS = COL_GR + D_MODEL

SUBLANES = 8
HALF_LANES = 64
VMEM_LIMIT = 56 * 1024 * 1024
ROW_CHUNK = 256


def _cparams(sem):
    return pltpu.CompilerParams(dimension_semantics=sem, vmem_limit_bytes=VMEM_LIMIT)


def _rms_bf16(x, g):
    ms = jnp.mean(x * x, axis=-1, keepdims=True)
    return (x * lax.rsqrt(ms + EPS) * g).astype(BF16)


def _silu(x):
    return x * jax.nn.sigmoid(x)


def _dot(a, b):
    return jnp.dot(a, b, preferred_element_type=F32)


def _dot_nt(a, b):
    return lax.dot_general(a, b, (((1,), (1,)), ((), ())), preferred_element_type=F32)


def _dot_tn(a, b):
    return lax.dot_general(a, b, (((0,), (0,)), ((), ())), preferred_element_type=F32)


def _ffn_kernel(x_ref, g_ref, gn_ref, wg_ref, wu_ref, wd_ref, o_ref, hn_ref, h_scr, *, nj, rc):
    j = pl.program_id(1)

    @pl.when(j == 0)
    def _():
        h_scr[...] = _rms_bf16(x_ref[...], g_ref[...])
        o_ref[...] = jnp.zeros_like(o_ref)

    for r in range(x_ref.shape[0] // rc):
        rows = pl.ds(r * rc, rc)
        h = h_scr[rows, :]
        gate = _dot(h, wg_ref[...])
        up = _dot(h, wu_ref[...])
        act = (_silu(gate) * up).astype(BF16)
        o_ref[rows, :] += _dot(act, wd_ref[...])

    @pl.when(j == nj - 1)
    def _():
        xn = x_ref[...] + 0.5 * o_ref[...]
        o_ref[...] = xn
        hn_ref[...] = _rms_bf16(xn, gn_ref[...])


def _ffn(x, g, g_next, w_gu, w_down, *, tm, tf):
    n, d = x.shape
    f = w_down.shape[0]
    nj = f // tf
    return pl.pallas_call(
        functools.partial(_ffn_kernel, nj=nj, rc=min(ROW_CHUNK, tm)),
        out_shape=(jax.ShapeDtypeStruct((n, d), F32), jax.ShapeDtypeStruct((n, d), BF16)),
        grid=(n // tm, nj),
        in_specs=[
            pl.BlockSpec((tm, d), lambda i, j: (i, 0)),
            pl.BlockSpec((1, d), lambda i, j: (0, 0)),
            pl.BlockSpec((1, d), lambda i, j: (0, 0)),
            pl.BlockSpec((d, tf), lambda i, j: (0, j)),
            pl.BlockSpec((d, tf), lambda i, j: (0, j + nj)),
            pl.BlockSpec((tf, d), lambda i, j: (j, 0)),
        ],
        out_specs=(pl.BlockSpec((tm, d), lambda i, j: (i, 0)),
                   pl.BlockSpec((tm, d), lambda i, j: (i, 0))),
        scratch_shapes=[pltpu.VMEM((tm, d), BF16)],
        compiler_params=_cparams(("parallel", "arbitrary")),
        name="ffn",
    )(x, g, g_next, w_gu, w_gu, w_down)


def _inproj_kernel(h_ref, w_ref, cos_ref, sin_ref, o_ref, *, tn, nq, nrope, rc):
    j = pl.program_id(1)
    nr = h_ref.shape[0] // rc

    @pl.when(j >= nrope)
    def _():
        for r in range(nr):
            rows = pl.ds(r * rc, rc)
            o_ref[rows, :] = _dot(h_ref[rows, :], w_ref[...]).astype(BF16)

    @pl.when(j < nrope)
    def _():
        scale = jnp.where(j >= nq, RET_DK ** -0.5, 1.0).astype(F32)
        half = RET_DK // 2
        for r in range(nr):
            rows = pl.ds(r * rc, rc)
            acc = _dot(h_ref[rows, :], w_ref[...])
            c = cos_ref[rows, :] * scale
            s = sin_ref[rows, :] * scale
            for hh in range(tn // RET_DK):
                lo = hh * RET_DK
                x1 = acc[:, lo:lo + half]
                x2 = acc[:, lo + half:lo + RET_DK]
                o_ref[rows, lo:lo + half] = (x1 * c - x2 * s).astype(BF16)
                o_ref[rows, lo + half:lo + RET_DK] = (x2 * c + x1 * s).astype(BF16)


def _inproj(h, w_big, cos, sin, *, tm, tn):
    n, d = h.shape
    nb = w_big.shape[1]
    nrow = cos.shape[0] // tm
    return pl.pallas_call(
        functools.partial(_inproj_kernel, tn=tn, nq=RET_QK // tn, nrope=2 * RET_QK // tn, rc=min(ROW_CHUNK, tm)),
        out_shape=jax.ShapeDtypeStruct((n, nb), BF16),
        grid=(n // tm, nb // tn),
        in_specs=[
            pl.BlockSpec((tm, d), lambda i, j: (i, 0)),
            pl.BlockSpec((d, tn), lambda i, j: (0, j)),
            pl.BlockSpec((tm, RET_DK // 2), lambda i, j: (i % nrow, 0)),
            pl.BlockSpec((tm, RET_DK // 2), lambda i, j: (i % nrow, 0)),
        ],
        out_specs=pl.BlockSpec((tm, tn), lambda i, j: (i, j)),
        compiler_params=_cparams(("parallel", "arbitrary")),
        name="inproj",
    )(h, w_big, cos, sin)


def _softplus(x):
    return jnp.maximum(x, 0.0) + jnp.log1p(jnp.exp(-jnp.abs(x)))


def _dt_kernel(h_ref, w_ref, wt_ref, br_ref, bc_ref, ar_ref, ac_ref,
               dtc_ref, acc_ref, dtr_ref, acr_ref, *, q, valid):
    h = h_ref[...]
    tm = h.shape[0]
    raw_c = _dot(h, w_ref[...])
    raw_r = _dot_nt(wt_ref[...], h)
    dt_c = _softplus(raw_c + br_ref[...])
    dt_r = _softplus(raw_r + bc_ref[...])
    if valid < q:
        row = lax.broadcasted_iota(jnp.int32, dt_c.shape, 0) % q
        col = lax.broadcasted_iota(jnp.int32, dt_r.shape, 1) % q
        dt_c = jnp.where(row < valid, dt_c, 0.0)
        dt_r = jnp.where(col < valid, dt_r, 0.0)
    dta_c = dt_c * (-jnp.exp(ar_ref[...]))
    dta_r = dt_r * (-jnp.exp(ac_ref[...]))
    i = lax.broadcasted_iota(jnp.int32, (tm, tm), 0)
    j = lax.broadcasted_iota(jnp.int32, (tm, tm), 1)
    same = (i // q) == (j // q)
    tri = jnp.where(same & (j <= i), 1.0, 0.0).astype(F32)
    trit = jnp.where(same & (i <= j), 1.0, 0.0).astype(F32)
    ac_c = jnp.dot(tri, dta_c, preferred_element_type=F32, precision=lax.Precision.HIGHEST)
    ac_r = jnp.dot(dta_r, trit, preferred_element_type=F32, precision=lax.Precision.HIGHEST)
    for g in range(SSM_GROUPS):
        lo = g * SSM_HPG
        dtc_ref[g] = dt_c[:, lo:lo + SSM_HPG]
        acc_ref[g] = ac_c[:, lo:lo + SSM_HPG]
        dtr_ref[g] = dt_r[lo:lo + SSM_HPG, :]
        acr_ref[g] = ac_r[lo:lo + SSM_HPG, :]


def _dt_proj(h, w_dt, w_dt_t, dt_bias, a_log, *, tm, q, valid):
    n, d = h.shape
    hh = SSM_HEADS
    col = jax.ShapeDtypeStruct((SSM_GROUPS, n, SSM_HPG), F32)
    row = jax.ShapeDtypeStruct((SSM_GROUPS, SSM_HPG, n), F32)
    return pl.pallas_call(
        functools.partial(_dt_kernel, q=q, valid=valid),
        out_shape=(col, col, row, row),
        grid=(n // tm,),
        in_specs=[
            pl.BlockSpec((tm, d), lambda i: (i, 0)),
            pl.BlockSpec((d, hh), lambda i: (0, 0)),
            pl.BlockSpec((hh, d), lambda i: (0, 0)),
            pl.BlockSpec((1, hh), lambda i: (0, 0)),
            pl.BlockSpec((hh, 1), lambda i: (0, 0)),
            pl.BlockSpec((1, hh), lambda i: (0, 0)),
            pl.BlockSpec((hh, 1), lambda i: (0, 0)),
        ],
        out_specs=(
            pl.BlockSpec((SSM_GROUPS, tm, SSM_HPG), lambda i: (0, i, 0)),
            pl.BlockSpec((SSM_GROUPS, tm, SSM_HPG), lambda i: (0, i, 0)),
            pl.BlockSpec((SSM_GROUPS, SSM_HPG, tm), lambda i: (0, 0, i)),
            pl.BlockSpec((SSM_GROUPS, SSM_HPG, tm), lambda i: (0, 0, i)),
        ),
        compiler_params=_cparams(("parallel",)),
        name="dt_proj",
    )(h, w_dt, w_dt_t, dt_bias.reshape(1, hh), dt_bias.reshape(hh, 1),
      a_log.reshape(1, hh), a_log.reshape(hh, 1))


def _ret_kernel(*refs, c, nchunk, has_s0):
    if has_s0:
        (q_ref, k_ref, v_ref, rg_ref, dm_ref, qd_ref, kd_ref, cd_ref, gg_ref, gb_ref, s0_ref,
         o_ref, s_ref) = refs
    else:
        (q_ref, k_ref, v_ref, rg_ref, dm_ref, qd_ref, kd_ref, cd_ref, gg_ref, gb_ref,
         o_ref, s_ref) = refs
    t = pl.program_id(2)

    @pl.when(t == 0)
    def _():
        if has_s0:
            s_ref[...] = s0_ref[...]
        else:
            s_ref[...] = jnp.zeros_like(s_ref)

    dm = dm_ref[0]
    qd = qd_ref[0]
    kd = kd_ref[0]
    cd = cd_ref[0]
    gg = gg_ref[...]
    gb = gb_ref[...]
    for ci in range(nchunk):
        sl = pl.ds(ci * c, c)
        q = q_ref[sl, :]
        k = k_ref[sl, :]
        v = v_ref[sl, :]
        s = s_ref[0, 0]
        att = _dot_nt(q, k) * dm
        o = _dot(att.astype(BF16), v) + _dot(q, s.astype(BF16)) * qd
        kdk = (k.astype(F32) * kd).astype(BF16)
        s_ref[0, 0] = s * cd + _dot_tn(kdk, v)
        mu = jnp.mean(o, axis=-1, keepdims=True)
        dev = o - mu
        var = jnp.mean(dev * dev, axis=-1, keepdims=True)
        on = dev * lax.rsqrt(var + EPS)
        rg = rg_ref[sl, :].astype(F32)
        o_ref[sl, :] = (_silu(rg) * (on * gg + gb)).astype(BF16)


def _ret_tables(c_true, c):
    hs = np.arange(RET_HEADS, dtype=np.float64)
    log_g = np.log1p(-np.exp2(-5.0 - hs))
    idx = np.arange(c, dtype=np.float64)
    diff = idx[:, None] - idx[None, :]
    dmat = np.where(diff[None] >= 0, np.exp(np.maximum(diff, 0.0)[None] * log_g[:, None, None]), 0.0)
    q_dec = np.exp((idx[None, :] + 1.0) * log_g[:, None])
    k_dec = np.where(idx[None, :] < c_true,
                     np.exp(np.maximum(c_true - 1.0 - idx[None, :], 0.0) * log_g[:, None]), 0.0)
    c_dec = np.exp(c_true * log_g)
    qd = np.broadcast_to(q_dec[:, :, None], (RET_HEADS, c, RET_DV))
    kd = np.broadcast_to(k_dec[:, :, None], (RET_HEADS, c, RET_DK))
    cd = np.broadcast_to(c_dec[:, None, None], (RET_HEADS, 1, RET_DV))
    f = lambda a: jnp.asarray(np.ascontiguousarray(a), dtype=F32)
    return f(dmat), f(qd), f(kd), f(cd)


def _retention(big, gn_g, gn_b, s0, *, nseq, seqlen, c_true, c, tb):
    n = big.shape[0]
    nt = seqlen // tb
    nchunk = tb // c
    dm, qd, kd, cd = _ret_tables(c_true, c)
    has_s0 = s0 is not None
    kq, kk = COL_Q // RET_DK, COL_K // RET_DK
    kv, kr = COL_V // RET_DV, COL_RG // RET_DV
    row = lambda b, h, t: b * nt + t
    in_specs = [
        pl.BlockSpec((tb, RET_DK), lambda b, h, t: (row(b, h, t), kq + h)),
        pl.BlockSpec((tb, RET_DK), lambda b, h, t: (row(b, h, t), kk + h)),
        pl.BlockSpec((tb, RET_DV), lambda b, h, t: (row(b, h, t), kv + h)),
        pl.BlockSpec((tb, RET_DV), lambda b, h, t: (row(b, h, t), kr + h)),
        pl.BlockSpec((1, c, c), lambda b, h, t: (h, 0, 0)),
        pl.BlockSpec((1, c, RET_DV), lambda b, h, t: (h, 0, 0)),
        pl.BlockSpec((1, c, RET_DK), lambda b, h, t: (h, 0, 0)),
        pl.BlockSpec((1, 1, RET_DV), lambda b, h, t: (h, 0, 0)),
        pl.BlockSpec((1, RET_DV), lambda b, h, t: (0, h)),
        pl.BlockSpec((1, RET_DV), lambda b, h, t: (0, h)),
    ]
    args = [big, big, big, big, dm, qd, kd, cd, gn_g.reshape(1, RET_V), gn_b.reshape(1, RET_V)]
    st_spec = pl.BlockSpec((1, 1, RET_DK, RET_DV), lambda b, h, t: (b, h, 0, 0))
    if has_s0:
        in_specs.append(st_spec)
        args.append(s0)
    return pl.pallas_call(
        functools.partial(_ret_kernel, c=c, nchunk=nchunk, has_s0=has_s0),
        out_shape=(jax.ShapeDtypeStruct((n, RET_V), BF16),
                   jax.ShapeDtypeStruct((nseq, RET_HEADS, RET_DK, RET_DV), F32)),
        grid=(nseq, RET_HEADS, nt),
        in_specs=in_specs,
        out_specs=(pl.BlockSpec((tb, RET_DV), lambda b, h, t: (row(b, h, t), h)), st_spec),
        compiler_params=_cparams(("parallel", "parallel", "arbitrary")),
        name="retention",
    )(*args)


def _conv_silu(raw_scr, cw, bias, tb):
    taps = SSM_CONV
    acc = bias
    for s in range(taps):
        acc = acc + raw_scr[pl.ds(SUBLANES - s, tb), :] * cw[taps - 1 - s:taps - s, :]
    return _silu(acc)


def _lane_expand(v, width):
    rows = v.shape[0]
    lane = lax.broadcasted_iota(jnp.int32, (rows, 2 * width), 1)
    parts = []
    for m in range(v.shape[1] // 2):
        a = jnp.broadcast_to(v[:, 2 * m:2 * m + 1], (rows, 2 * width))
        b = jnp.broadcast_to(v[:, 2 * m + 1:2 * m + 2], (rows, 2 * width))
        parts.append(jnp.where(lane < width, a, b))
    return jnp.concatenate(parts, axis=1)


def _ssd_kernel(*refs, q, nchunk, has_s0):
    if has_s0:
        (xs_ref, b_ref, c_ref, z_ref, dtc_ref, acc_ref, dtr_ref, acr_ref,
         cwx_ref, cwb_ref, cwc_ref, cbx_ref, cbb_ref, cbc_ref, dsk_ref, ng_ref,
         s0_ref, hx0_ref, hb0_ref, hc0_ref,
         y_ref, s_ref, hx, hb, hc, xc_scr, bc_scr, cc_scr) = refs
    else:
        (xs_ref, b_ref, c_ref, z_ref, dtc_ref, acc_ref, dtr_ref, acr_ref,
         cwx_ref, cwb_ref, cwc_ref, cbx_ref, cbb_ref, cbc_ref, dsk_ref, ng_ref,
         y_ref, s_ref, hx, hb, hc, xc_scr, bc_scr, cc_scr) = refs
    t = pl.program_id(2)

    @pl.when(t == 0)
    def _():
        if has_s0:
            s_ref[...] = s0_ref[...]
            hx[0:SUBLANES, :] = hx0_ref[0]
            hb[0:SUBLANES, :] = hb0_ref[0]
            hc[0:SUBLANES, :] = hc0_ref[0]
        else:
            s_ref[...] = jnp.zeros_like(s_ref)
            hx[0:SUBLANES, :] = jnp.zeros((SUBLANES, hx.shape[1]), F32)
            hb[0:SUBLANES, :] = jnp.zeros((SUBLANES, hb.shape[1]), F32)
            hc[0:SUBLANES, :] = jnp.zeros((SUBLANES, hc.shape[1]), F32)

    tb = xs_ref.shape[0]
    for raw_ref, raw_scr, cw_ref, cb_ref, dst in ((xs_ref, hx, cwx_ref, cbx_ref, xc_scr),
                                                  (b_ref, hb, cwb_ref, cbb_ref, bc_scr),
                                                  (c_ref, hc, cwc_ref, cbc_ref, cc_scr)):
        raw_scr[SUBLANES:SUBLANES + tb, :] = raw_ref[...].astype(F32)
        dst[...] = _conv_silu(raw_scr, cw_ref[...], cb_ref[...], tb)
        raw_scr[0:SUBLANES, :] = raw_scr[tb:tb + SUBLANES, :]

    ii = lax.broadcasted_iota(jnp.int32, (q, q), 0)
    jj = lax.broadcasted_iota(jnp.int32, (q, q), 1)
    causal = ii >= jj
    lane = lax.broadcasted_iota(jnp.int32, (q, 2 * HALF_LANES), 1)
    dsk = dsk_ref[...]
    ng = ng_ref[...]
    for ci in range(nchunk):
        sl = pl.ds(ci * q, q)
        x = xc_scr[sl, :]
        bq = bc_scr[sl, :].astype(BF16)
        cq = cc_scr[sl, :].astype(BF16)
        dtc = dtc_ref[0, sl, :]
        acc = acc_ref[0, sl, :]
        dtr = dtr_ref[0, :, sl]
        acr = acr_ref[0, :, sl]
        state = s_ref[0, 0]

        cb = jnp.where(causal, _dot_nt(cq, bq), 0.0)
        pairs = []
        for m in range(SSM_HPG // 2):
            ws = []
            for r in (2 * m, 2 * m + 1):
                seg = acc[:, r:r + 1] - acr[r:r + 1, :]
                ws.append((jnp.exp(jnp.minimum(seg, 0.0)) * (cb * dtr[r:r + 1, :])).astype(BF16))
            xp = x[:, m * 2 * HALF_LANES:(m + 1) * 2 * HALF_LANES]
            x_lo = jnp.where(lane < HALF_LANES, xp, 0.0).astype(BF16)
            x_hi = jnp.where(lane >= HALF_LANES, xp, 0.0).astype(BF16)
            pairs.append(_dot(jnp.concatenate(ws, axis=1), jnp.concatenate([x_lo, x_hi], axis=0)))
        y = jnp.concatenate(pairs, axis=1)

        ea = _lane_expand(jnp.exp(acc), SSM_HEADDIM)
        y = y + _dot(cq, state.astype(BF16)) * ea
        wend = jnp.exp(acc[q - 1:q, :] - acc) * dtc
        xw = (x * _lane_expand(wend, SSM_HEADDIM)).astype(BF16)
        s_ref[0, 0] = state * ea[q - 1:q, :] + _dot_tn(bq, xw)

        y = y + dsk * x
        y = y * _silu(z_ref[sl, :].astype(F32))
        ms = jnp.mean(y * y, axis=-1, keepdims=True)
        y_ref[sl, :] = (y * lax.rsqrt(ms + EPS) * ng).astype(BF16)


def _ssd(big, dts, conv_w, conv_b, d_skip_x, norm_g, s0, hist0, *, nseq, seqlen, q, tb):
    n = big.shape[0]
    nt = seqlen // tb
    nchunk = tb // q
    has_s0 = s0 is not None
    dtc, acc, dtr, acr = dts
    gw, ns = SSM_GW, SSM_STATE
    kz, kx = COL_Z // gw, COL_XBC // gw
    kb = (COL_XBC + SSM_INNER) // ns
    kc = kb + SSM_GROUPS
    cb_x, cb_b = 0, SSM_INNER // ns
    cb_c = cb_b + SSM_GROUPS
    row = lambda b, g, t: b * nt + t
    in_specs = [
        pl.BlockSpec((tb, gw), lambda b, g, t: (row(b, g, t), kx + g)),
        pl.BlockSpec((tb, ns), lambda b, g, t: (row(b, g, t), kb + g)),
        pl.BlockSpec((tb, ns), lambda b, g, t: (row(b, g, t), kc + g)),
        pl.BlockSpec((tb, gw), lambda b, g, t: (row(b, g, t), kz + g)),
        pl.BlockSpec((1, tb, SSM_HPG), lambda b, g, t: (g, row(b, g, t), 0)),
        pl.BlockSpec((1, tb, SSM_HPG), lambda b, g, t: (g, row(b, g, t), 0)),
        pl.BlockSpec((1, SSM_HPG, tb), lambda b, g, t: (g, 0, row(b, g, t))),
        pl.BlockSpec((1, SSM_HPG, tb), lambda b, g, t: (g, 0, row(b, g, t))),
        pl.BlockSpec((SSM_CONV, gw), lambda b, g, t: (0, g)),
        pl.BlockSpec((SSM_CONV, ns), lambda b, g, t: (0, cb_b + g)),
        pl.BlockSpec((SSM_CONV, ns), lambda b, g, t: (0, cb_c + g)),
        pl.BlockSpec((1, gw), lambda b, g, t: (0, g)),
        pl.BlockSpec((1, ns), lambda b, g, t: (0, cb_b + g)),
        pl.BlockSpec((1, ns), lambda b, g, t: (0, cb_c + g)),
        pl.BlockSpec((1, gw), lambda b, g, t: (0, g)),
        pl.BlockSpec((1, gw), lambda b, g, t: (0, g)),
    ]
    cbias = conv_b.reshape(1, SSM_CONV_DIM)
    args = [big, big, big, big, dtc, acc, dtr, acr, conv_w, conv_w, conv_w, cbias, cbias, cbias,
            d_skip_x, norm_g.reshape(1, SSM_INNER)]
    st_spec = pl.BlockSpec((1, 1, ns, gw), lambda b, g, t: (b, g, 0, 0))
    if has_s0:
        in_specs += [
            st_spec,
            pl.BlockSpec((1, SUBLANES, gw), lambda b, g, t: (b, 0, g)),
            pl.BlockSpec((1, SUBLANES, ns), lambda b, g, t: (b, 0, cb_b + g)),
            pl.BlockSpec((1, SUBLANES, ns), lambda b, g, t: (b, 0, cb_c + g)),
        ]
        args += [s0, hist0, hist0, hist0]
    return pl.pallas_call(
        functools.partial(_ssd_kernel, q=q, nchunk=nchunk, has_s0=has_s0),
        out_shape=(jax.ShapeDtypeStruct((n, SSM_INNER), BF16),
                   jax.ShapeDtypeStruct((nseq, SSM_GROUPS, ns, gw), F32)),
        grid=(nseq, SSM_GROUPS, nt),
        in_specs=in_specs,
        out_specs=(pl.BlockSpec((tb, gw), lambda b, g, t: (row(b, g, t), g)), st_spec),
        scratch_shapes=[pltpu.VMEM((SUBLANES + tb, gw), F32), pltpu.VMEM((SUBLANES + tb, ns), F32),
                        pltpu.VMEM((SUBLANES + tb, ns), F32), pltpu.VMEM((tb, gw), F32),
                        pltpu.VMEM((tb, ns), F32), pltpu.VMEM((tb, ns), F32)],
        compiler_params=_cparams(("parallel", "parallel", "arbitrary")),
        name="ssd",
    )(*args)


def _merge_kernel(h_ref, ret_ref, ssm_ref, wgr_ref, wgs_ref, wr_ref, ws_ref, o_ref, *, rc):
    for r in range(h_ref.shape[0] // rc):
        rows = pl.ds(r * rc, rc)
        h = h_ref[rows, :]
        gr = jax.nn.sigmoid(_dot(h, wgr_ref[...]))
        gs = jax.nn.sigmoid(_dot(h, wgs_ref[...]))
        a = _dot(ret_ref[rows, :], wr_ref[...])
        b = _dot(ssm_ref[rows, :], ws_ref[...])
        o_ref[rows, :] = (gr * a + gs * b).astype(BF16)


def _merge(h, ret, ssm, w_gr, w_gs, w_r, w_s, *, tm, tn):
    n, d = h.shape
    kr, ks = ret.shape[1], ssm.shape[1]
    return pl.pallas_call(
        functools.partial(_merge_kernel, rc=min(ROW_CHUNK, tm)),
        out_shape=jax.ShapeDtypeStruct((n, d), BF16),
        grid=(n // tm, d // tn),
        in_specs=[
            pl.BlockSpec((tm, d), lambda i, j: (i, 0)),
            pl.BlockSpec((tm, kr), lambda i, j: (i, 0)),
            pl.BlockSpec((tm, ks), lambda i, j: (i, 0)),
            pl.BlockSpec((d, tn), lambda i, j: (0, j)),
            pl.BlockSpec((d, tn), lambda i, j: (0, j)),
            pl.BlockSpec((kr, tn), lambda i, j: (0, j)),
            pl.BlockSpec((ks, tn), lambda i, j: (0, j)),
        ],
        out_specs=pl.BlockSpec((tm, tn), lambda i, j: (i, j)),
        compiler_params=_cparams(("parallel", "arbitrary")),
        name="merge",
    )(h, ret, ssm, w_gr, w_gs, w_r, w_s)


def _outproj_kernel(m_ref, w_ref, x_ref, o_ref, *, rc):
    for r in range(m_ref.shape[0] // rc):
        rows = pl.ds(r * rc, rc)
        o_ref[rows, :] = x_ref[rows, :] + _dot(m_ref[rows, :], w_ref[...])


def _outproj(m, w, x, *, tm, tn):
    n, d = x.shape
    k = m.shape[1]
    return pl.pallas_call(
        functools.partial(_outproj_kernel, rc=min(ROW_CHUNK, tm)),
        out_shape=jax.ShapeDtypeStruct((n, d), F32),
        grid=(n // tm, d // tn),
        in_specs=[
            pl.BlockSpec((tm, k), lambda i, j: (i, 0)),
            pl.BlockSpec((k, tn), lambda i, j: (0, j)),
            pl.BlockSpec((tm, tn), lambda i, j: (i, j)),
        ],
        out_specs=pl.BlockSpec((tm, tn), lambda i, j: (i, j)),
        compiler_params=_cparams(("parallel", "arbitrary")),
        name="outproj",
    )(m, w, x)


def _ple_kernel(x_ref, h_ref, p_ref, wp_ref, wg_ref, gf_ref, o_ref):
    pe = _dot(p_ref[...].astype(BF16), wp_ref[...])
    gt = jax.nn.sigmoid(_dot(h_ref[...], wg_ref[...]))
    x = x_ref[...] + pe * gt
    ms = jnp.mean(x * x, axis=-1, keepdims=True)
    o_ref[...] = x * lax.rsqrt(ms + EPS) * gf_ref[...]


def _ple(x, h, p, w_ple, w_gate, g_final, *, tm):
    n, d = x.shape
    pd = p.shape[1]
    return pl.pallas_call(
        _ple_kernel,
        out_shape=jax.ShapeDtypeStruct((n, d), F32),
        grid=(n // tm,),
        in_specs=[
            pl.BlockSpec((tm, d), lambda i: (i, 0)),
            pl.BlockSpec((tm, d), lambda i: (i, 0)),
            pl.BlockSpec((tm, pd), lambda i: (i, 0)),
            pl.BlockSpec((pd, d), lambda i: (0, 0)),
            pl.BlockSpec((d, d), lambda i: (0, 0)),
            pl.BlockSpec((1, d), lambda i: (0, 0)),
        ],
        out_specs=pl.BlockSpec((tm, d), lambda i: (i, 0)),
        compiler_params=_cparams(("parallel",)),
        name="ple",
    )(x, h, p, w_ple, w_gate, g_final)


def _rope_tables(pos0, seqlen, rows):
    half = RET_DK // 2
    inv = ROPE_THETA ** (-jnp.arange(half, dtype=F32) / half)
    pos = (pos0 + jnp.arange(seqlen, dtype=jnp.int32)).astype(F32)
    ang = pos[:, None] * inv[None, :]
    reps = max(rows // seqlen, 1)
    return jnp.tile(jnp.cos(ang), (reps, 1)), jnp.tile(jnp.sin(ang), (reps, 1))


def _pick(n, pref):
    t = pref
    while n % t:
        t //= 2
    return t


def _trunk(x, p, pos0, s_ret, s_ssm, s_conv, w, cfg):
    nseq, seqlen, d = x.shape
    n = nseq * seqlen
    x = x.reshape(n, d)
    p = p.reshape(n, p.shape[-1])
    tm = _pick(n, cfg["tm"])

    x1, h = _ffn(x, w["g_ffn1"], w["g_mix"], w["w1_gu"], w["w1_down"], tm=_pick(n, cfg["tm_ffn"]), tf=cfg["tf"])

    cos, sin = _rope_tables(pos0, seqlen, tm)
    big = _inproj(h, w["w_big"], cos, sin, tm=tm, tn=cfg["tn_in"])

    q = cfg["q_ssd"]
    lpad = -(-seqlen // q) * q
    if lpad != seqlen:
        pad = lambda a: jnp.pad(a.reshape(nseq, seqlen, -1), ((0, 0), (0, lpad - seqlen), (0, 0))).reshape(nseq * lpad, -1)
        big_m, h_m = pad(big), pad(h)
    else:
        big_m, h_m = big, h
    npad = nseq * lpad
    valid = min(seqlen, q)

    dts = _dt_proj(h_m, w["w_dt"], w["w_dt_t"], w["dt_bias"], w["a_log"], tm=_pick(npad, 512), q=q, valid=valid)

    c = cfg["c_ret"] if lpad % cfg["c_ret"] == 0 else q
    ret, s_ret_new = _retention(big_m, w["ret_gn_g"], w["ret_gn_b"], s_ret, nseq=nseq, seqlen=lpad,
                                c_true=min(seqlen, c), c=c, tb=_pick(lpad, cfg["tb_ret"]))

    if s_ssm is not None:
        s0 = s_ssm.reshape(nseq, SSM_GROUPS, SSM_HPG, SSM_HEADDIM, SSM_STATE)
        s0 = s0.transpose(0, 1, 4, 2, 3).reshape(nseq, SSM_GROUPS, SSM_STATE, SSM_GW)
        hist0 = jnp.pad(s_conv, ((0, 0), (SUBLANES - (SSM_CONV - 1), 0), (0, 0)))
    else:
        s0, hist0 = None, None
    ssm, s_ssm_new = _ssd(big_m, dts, w["conv_w"], w["conv_b"], w["d_skip_x"], w["ssm_norm_g"], s0, hist0,
                          nseq=nseq, seqlen=lpad, q=q, tb=_pick(lpad, cfg["tb_ssd"]))
    s_ssm_new = s_ssm_new.reshape(nseq, SSM_GROUPS, SSM_STATE, SSM_HPG, SSM_HEADDIM)
    s_ssm_new = s_ssm_new.transpose(0, 1, 3, 4, 2).reshape(nseq, SSM_HEADS, SSM_HEADDIM, SSM_STATE)

    if lpad != seqlen:
        unpad = lambda a: a.reshape(nseq, lpad, -1)[:, :seqlen].reshape(n, -1)
        ret, ssm = unpad(ret), unpad(ssm)

    keep = min(seqlen, SSM_CONV - 1)
    xbc_tail = big.reshape(nseq, seqlen, N_BIG)[:, seqlen - keep:, COL_XBC:].astype(F32)
    if keep < SSM_CONV - 1:
        prev = jnp.zeros((nseq, SSM_CONV - 1, SSM_CONV_DIM), F32) if s_conv is None else s_conv.astype(F32)
        xbc_tail = jnp.concatenate([prev, xbc_tail], axis=1)[:, -(SSM_CONV - 1):]

    merged = _merge(h, ret, ssm, w["w_gr"], w["w_gs"], w["w_br_ret"], w["w_br_ssm"], tm=_pick(n, cfg["tm_mg"]), tn=cfg["tn_mg"])
    x2 = _outproj(merged, w["w_out"], x1, tm=tm, tn=cfg["tn_out"])
    x3, h3 = _ffn(x2, w["g_ffn2"], w["g_ple"], w["w2_gu"], w["w2_down"], tm=_pick(n, cfg["tm_ffn"]), tf=cfg["tf"])
    y = _ple(x3, h3, p, w["w_ple"], w["w_ple_gate"], w["g_final"], tm=_pick(n, cfg["tm_ple"]))
    return y.reshape(nseq, seqlen, d), s_ret_new, s_ssm_new, xbc_tail


CFG = dict(tm=1024, tm_ffn=512, tf=512, tn_in=1024, q_ssd=128, c_ret=256, tb_ret=1024, tb_ssd=256,
           tm_mg=512, tn_mg=256, tn_out=1024, tm_ple=256)


def _prep_weights(g_ffn1, w1_gu, w1_down, g_mix, w_in, ret_gn_g, ret_gn_b, conv_w, conv_b, dt_bias, a_log,
                  d_skip, ssm_norm_g, w_br_ret, w_br_ssm, w_out, g_ffn2, w2_gu, w2_down, g_ple, w_ple,
                  w_ple_gate, g_final):
    b = lambda a: a[0].astype(BF16)
    r = lambda a: a[0].reshape(1, -1).astype(F32)
    win = w_in[0]
    w_dt = win[:, COL_DT:COL_DT + SSM_HEADS].astype(BF16)
    return dict(
        g_ffn1=r(g_ffn1), w1_gu=b(w1_gu), w1_down=b(w1_down), g_mix=r(g_mix),
        w_big=win[:, :N_BIG].astype(BF16), w_dt=w_dt, w_dt_t=w_dt.T,
        w_gr=win[:, COL_GR:COL_GR + D_MODEL].astype(BF16), w_gs=win[:, COL_GS:COL_GS + D_MODEL].astype(BF16),
        ret_gn_g=ret_gn_g[0].astype(F32), ret_gn_b=ret_gn_b[0].astype(F32),
        conv_w=conv_w[0].astype(F32), conv_b=conv_b[0].astype(F32),
        dt_bias=dt_bias[0].astype(F32), a_log=a_log[0].astype(F32),
        d_skip_x=jnp.repeat(d_skip[0].astype(F32), SSM_HEADDIM).reshape(1, SSM_INNER),
        ssm_norm_g=ssm_norm_g[0].astype(F32),
        w_br_ret=b(w_br_ret), w_br_ssm=b(w_br_ssm), w_out=b(w_out),
        g_ffn2=r(g_ffn2), w2_gu=b(w2_gu), w2_down=b(w2_down), g_ple=r(g_ple),
        w_ple=b(w_ple), w_ple_gate=b(w_ple_gate), g_final=g_final.reshape(1, -1).astype(F32),
    )


def kernel(x_prompt, x_sample, state_ret, state_ssm, state_conv, p_prompt, p_sample, g_ffn1, w1_gu, w1_down, g_mix, w_in, ret_gn_g, ret_gn_b, conv_w, conv_b, dt_bias, a_log, d_skip, ssm_norm_g, w_br_ret, w_br_ssm, w_out, g_ffn2, w2_gu, w2_down, g_ple, w_ple, w_ple_gate, g_final):
    assert g_ffn1.shape[0] == 1, "single-layer trunk"
    w = _prep_weights(g_ffn1, w1_gu, w1_down, g_mix, w_in, ret_gn_g, ret_gn_b, conv_w, conv_b, dt_bias, a_log,
                      d_skip, ssm_norm_g, w_br_ret, w_br_ssm, w_out, g_ffn2, w2_gu, w2_down, g_ple, w_ple,
                      w_ple_gate, g_final)
    y_p, ret_p, ssm_p, conv_p = _trunk(x_prompt, p_prompt[0], 0, None, None, None, w, CFG)
    y_s, ret_s, ssm_s, conv_s = _trunk(x_sample, p_sample[0], PAST_LEN, state_ret[0], state_ssm[0],
                                       state_conv[0], w, CFG)
    e = lambda a: a[None]
    return (y_p, y_s, e(ret_p), e(ssm_p), e(conv_p), e(ret_s), e(ssm_s), e(conv_s))
```

```python
import functools

import numpy as np
import jax
import jax.numpy as jnp
from jax import lax
from jax.experimental import pallas as pl
from jax.experimental.pallas import tpu as pltpu

F32 = jnp.float32
BF16 = jnp.bfloat16

D_MODEL = 2048
PAST_LEN = 4096
EPS = 1e-6
RET_HEADS = 8
RET_DK = 256
RET_DV = 512
RET_QK = RET_HEADS * RET_DK
RET_V = RET_HEADS * RET_DV
ROPE_THETA = 10000.0
SSM_INNER = 2 * D_MODEL
SSM_HEADDIM = 64
SSM_HEADS = SSM_INNER // SSM_HEADDIM
SSM_GROUPS = 8
SSM_HPG = SSM_HEADS // SSM_GROUPS
SSM_GW = SSM_HPG * SSM_HEADDIM
SSM_STATE = 128
SSM_CONV = 4
SSM_CONV_DIM = SSM_INNER + 2 * SSM_GROUPS * SSM_STATE
PLE_DIM = 256

COL_Q = 0
COL_K = COL_Q + RET_QK
COL_V = COL_K + RET_QK
COL_RG = COL_V + RET_V
COL_Z = COL_RG + RET_V
COL_XBC = COL_Z + SSM_INNER
N_BIG = COL_XBC + SSM_CONV_DIM
COL_DT = N_BIG
COL_GR = COL_DT + SSM_HEADS
COL_GS = COL_GR + D_MODEL

SUBLANES = 8
HALF_LANES = 64
VMEM_LIMIT = 56 * 1024 * 1024
ROW_CHUNK = 256


def _cparams(sem):
    return pltpu.CompilerParams(dimension_semantics=sem, vmem_limit_bytes=VMEM_LIMIT)


def _rms_bf16(x, g):
    ms = jnp.mean(x * x, axis=-1, keepdims=True)
    return (x * lax.rsqrt(ms + EPS) * g).astype(BF16)


def _silu(x):
    return x * jax.nn.sigmoid(x)


def _dot(a, b):
    return jnp.dot(a, b, preferred_element_type=F32)


def _dot_nt(a, b):
    return lax.dot_general(a, b, (((1,), (1,)), ((), ())), preferred_element_type=F32)


def _dot_tn(a, b):
    return lax.dot_general(a, b, (((0,), (0,)), ((), ())), preferred_element_type=F32)


def _ffn_kernel(x_ref, g_ref, gn_ref, wg_ref, wu_ref, wd_ref, o_ref, hn_ref, h_scr, *, nj):
    j = pl.program_id(1)

    @pl.when(j == 0)
    def _():
        h_scr[...] = _rms_bf16(x_ref[...], g_ref[...])
        o_ref[...] = jnp.zeros_like(o_ref)

    h = h_scr[...]
    gate = _dot(h, wg_ref[...])
    up = _dot(h, wu_ref[...])
    act = (_silu(gate) * up).astype(BF16)
    o_ref[...] += _dot(act, wd_ref[...])

    @pl.when(j == nj - 1)
    def _():
        xn = x_ref[...] + 0.5 * o_ref[...]
        o_ref[...] = xn
        hn_ref[...] = _rms_bf16(xn, gn_ref[...])


def _ffn(x, g, g_next, w_gu, w_down, *, tm, tf):
    n, d = x.shape
    f = w_down.shape[0]
    nj = f // tf
    return pl.pallas_call(
        functools.partial(_ffn_kernel, nj=nj),
        out_shape=(jax.ShapeDtypeStruct((n, d), F32), jax.ShapeDtypeStruct((n, d), BF16)),
        grid=(n // tm, nj),
        in_specs=[
            pl.BlockSpec((tm, d), lambda i, j: (i, 0)),
            pl.BlockSpec((1, d), lambda i, j: (0, 0)),
            pl.BlockSpec((1, d), lambda i, j: (0, 0)),
            pl.BlockSpec((d, tf), lambda i, j: (0, j)),
            pl.BlockSpec((d, tf), lambda i, j: (0, j + nj)),
            pl.BlockSpec((tf, d), lambda i, j: (j, 0)),
        ],
        out_specs=(pl.BlockSpec((tm, d), lambda i, j: (i, 0)),
                   pl.BlockSpec((tm, d), lambda i, j: (i, 0))),
        scratch_shapes=[pltpu.VMEM((tm, d), BF16)],
        compiler_params=_cparams(("parallel", "arbitrary")),
        name="ffn",
    )(x, g, g_next, w_gu, w_gu, w_down)


def _inproj_kernel(h_ref, w_ref, cos_ref, sin_ref, o_ref, *, tn, nq, nrope, rc):
    j = pl.program_id(1)
    nr = h_ref.shape[0] // rc

    @pl.when(j >= nrope)
    def _():
        for r in range(nr):
            rows = pl.ds(r * rc, rc)
            o_ref[rows, :] = _dot(h_ref[rows, :], w_ref[...]).astype(BF16)

    @pl.when(j < nrope)
    def _():
        scale = jnp.where(j >= nq, RET_DK ** -0.5, 1.0).astype(F32)
        half = RET_DK // 2
        for r in range(nr):
            rows = pl.ds(r * rc, rc)
            acc = _dot(h_ref[rows, :], w_ref[...])
            c = cos_ref[rows, :] * scale
            s = sin_ref[rows, :] * scale
            for hh in range(tn // RET_DK):
                lo = hh * RET_DK
                x1 = acc[:, lo:lo + half]
                x2 = acc[:, lo + half:lo + RET_DK]
                o_ref[rows, lo:lo + half] = (x1 * c - x2 * s).astype(BF16)
                o_ref[rows, lo + half:lo + RET_DK] = (x2 * c + x1 * s).astype(BF16)


def _inproj(h, w_big, cos, sin, *, tm, tn):
    n, d = h.shape
    nb = w_big.shape[1]
    nrow = cos.shape[0] // tm
    return pl.pallas_call(
        functools.partial(_inproj_kernel, tn=tn, nq=RET_QK // tn, nrope=2 * RET_QK // tn, rc=min(ROW_CHUNK, tm)),
        out_shape=jax.ShapeDtypeStruct((n, nb), BF16),
        grid=(n // tm, nb // tn),
        in_specs=[
            pl.BlockSpec((tm, d), lambda i, j: (i, 0)),
            pl.BlockSpec((d, tn), lambda i, j: (0, j)),
            pl.BlockSpec((tm, RET_DK // 2), lambda i, j: (i % nrow, 0)),
            pl.BlockSpec((tm, RET_DK // 2), lambda i, j: (i % nrow, 0)),
        ],
        out_specs=pl.BlockSpec((tm, tn), lambda i, j: (i, j)),
        compiler_params=_cparams(("parallel", "arbitrary")),
        name="inproj",
    )(h, w_big, cos, sin)


def _softplus(x):
    return jnp.maximum(x, 0.0) + jnp.log1p(jnp.exp(-jnp.abs(x)))


def _dt_kernel(h_ref, w_ref, wt_ref, br_ref, bc_ref, ar_ref, ac_ref,
               dtc_ref, acc_ref, dtr_ref, acr_ref, *, q, valid):
    h = h_ref[...]
    tm = h.shape[0]
    raw_c = _dot(h, w_ref[...])
    raw_r = _dot_nt(wt_ref[...], h)
    dt_c = _softplus(raw_c + br_ref[...])
    dt_r = _softplus(raw_r + bc_ref[...])
    if valid < q:
        row = lax.broadcasted_iota(jnp.int32, dt_c.shape, 0) % q
        col = lax.broadcasted_iota(jnp.int32, dt_r.shape, 1) % q
        dt_c = jnp.where(row < valid, dt_c, 0.0)
        dt_r = jnp.where(col < valid, dt_r, 0.0)
    dta_c = dt_c * (-jnp.exp(ar_ref[...]))
    dta_r = dt_r * (-jnp.exp(ac_ref[...]))
    i = lax.broadcasted_iota(jnp.int32, (tm, tm), 0)
    j = lax.broadcasted_iota(jnp.int32, (tm, tm), 1)
    same = (i // q) == (j // q)
    tri = jnp.where(same & (j <= i), 1.0, 0.0).astype(F32)
    trit = jnp.where(same & (i <= j), 1.0, 0.0).astype(F32)
    ac_c = jnp.dot(tri, dta_c, preferred_element_type=F32, precision=lax.Precision.HIGHEST)
    ac_r = jnp.dot(dta_r, trit, preferred_element_type=F32, precision=lax.Precision.HIGHEST)
    for g in range(SSM_GROUPS):
        lo = g * SSM_HPG
        dtc_ref[g] = dt_c[:, lo:lo + SSM_HPG]
        acc_ref[g] = ac_c[:, lo:lo + SSM_HPG]
        dtr_ref[g] = dt_r[lo:lo + SSM_HPG, :]
        acr_ref[g] = ac_r[lo:lo + SSM_HPG, :]


def _dt_proj(h, w_dt, w_dt_t, dt_bias, a_log, *, tm, q, valid):
    n, d = h.shape
    hh = SSM_HEADS
    col = jax.ShapeDtypeStruct((SSM_GROUPS, n, SSM_HPG), F32)
    row = jax.ShapeDtypeStruct((SSM_GROUPS, SSM_HPG, n), F32)
    return pl.pallas_call(
        functools.partial(_dt_kernel, q=q, valid=valid),
        out_shape=(col, col, row, row),
        grid=(n // tm,),
        in_specs=[
            pl.BlockSpec((tm, d), lambda i: (i, 0)),
            pl.BlockSpec((d, hh), lambda i: (0, 0)),
            pl.BlockSpec((hh, d), lambda i: (0, 0)),
            pl.BlockSpec((1, hh), lambda i: (0, 0)),
            pl.BlockSpec((hh, 1), lambda i: (0, 0)),
            pl.BlockSpec((1, hh), lambda i: (0, 0)),
            pl.BlockSpec((hh, 1), lambda i: (0, 0)),
        ],
        out_specs=(
            pl.BlockSpec((SSM_GROUPS, tm, SSM_HPG), lambda i: (0, i, 0)),
            pl.BlockSpec((SSM_GROUPS, tm, SSM_HPG), lambda i: (0, i, 0)),
            pl.BlockSpec((SSM_GROUPS, SSM_HPG, tm), lambda i: (0, 0, i)),
            pl.BlockSpec((SSM_GROUPS, SSM_HPG, tm), lambda i: (0, 0, i)),
        ),
        compiler_params=_cparams(("parallel",)),
        name="dt_proj",
    )(h, w_dt, w_dt_t, dt_bias.reshape(1, hh), dt_bias.reshape(hh, 1),
      a_log.reshape(1, hh), a_log.reshape(hh, 1))


def _ret_kernel(*refs, c, nchunk, sb, has_s0):
    if has_s0:
        (q_ref, k_ref, v_ref, rg_ref, dm_ref, qd_ref, kd_ref, cd_ref, gg_ref, gb_ref, s0_ref,
         o_ref, s_ref) = refs
    else:
        (q_ref, k_ref, v_ref, rg_ref, dm_ref, qd_ref, kd_ref, cd_ref, gg_ref, gb_ref,
         o_ref, s_ref) = refs
    t = pl.program_id(2)

    @pl.when(t == 0)
    def _():
        if has_s0:
            s_ref[...] = s0_ref[...]
        else:
            s_ref[...] = jnp.zeros_like(s_ref)

    dm = dm_ref[0]
    qd = qd_ref[0]
    kd = kd_ref[0]
    cd = cd_ref[0]
    gg = gg_ref[...]
    gb = gb_ref[...]
    for si, ci in ((si, ci) for si in range(sb) for ci in range(nchunk)):
        sl = pl.ds((si * nchunk + ci) * c, c)
        q = q_ref[sl, :]
        k = k_ref[sl, :]
        v = v_ref[sl, :]
        s = s_ref[si, 0]
        att = _dot_nt(q, k) * dm
        o = _dot(att.astype(BF16), v) + _dot(q, s.astype(BF16)) * qd
        kdk = (k.astype(F32) * kd).astype(BF16)
        s_ref[si, 0] = s * cd + _dot_tn(kdk, v)
        mu = jnp.mean(o, axis=-1, keepdims=True)
        dev = o - mu
        var = jnp.mean(dev * dev, axis=-1, keepdims=True)
        on = dev * lax.rsqrt(var + EPS)
        rg = rg_ref[sl, :].astype(F32)
        o_ref[sl, :] = (_silu(rg) * (on * gg + gb)).astype(BF16)


def _ret_tables(c_true, c):
    hs = np.arange(RET_HEADS, dtype=np.float64)
    log_g = np.log1p(-np.exp2(-5.0 - hs))
    idx = np.arange(c, dtype=np.float64)
    diff = idx[:, None] - idx[None, :]
    dmat = np.where(diff[None] >= 0, np.exp(np.maximum(diff, 0.0)[None] * log_g[:, None, None]), 0.0)
    q_dec = np.exp((idx[None, :] + 1.0) * log_g[:, None])
    k_dec = np.where(idx[None, :] < c_true,
                     np.exp(np.maximum(c_true - 1.0 - idx[None, :], 0.0) * log_g[:, None]), 0.0)
    c_dec = np.exp(c_true * log_g)
    qd = np.broadcast_to(q_dec[:, :, None], (RET_HEADS, c, RET_DV))
    kd = np.broadcast_to(k_dec[:, :, None], (RET_HEADS, c, RET_DK))
    cd = np.broadcast_to(c_dec[:, None, None], (RET_HEADS, 1, RET_DV))
    f = lambda a: jnp.asarray(np.ascontiguousarray(a), dtype=F32)
    return f(dmat), f(qd), f(kd), f(cd)


def _retention(big, gn_g, gn_b, s0, *, nseq, seqlen, c_true, c, tb, sb):
    n = big.shape[0]
    nt = seqlen // tb
    assert sb == 1 or nt == 1
    nchunk = tb // c
    rb = sb * tb
    dm, qd, kd, cd = _ret_tables(c_true, c)
    has_s0 = s0 is not None
    kq, kk = COL_Q // RET_DK, COL_K // RET_DK
    kv, kr = COL_V // RET_DV, COL_RG // RET_DV
    row = lambda b, h, t: b * nt + t
    in_specs = [
        pl.BlockSpec((rb, RET_DK), lambda b, h, t: (row(b, h, t), kq + h)),
        pl.BlockSpec((rb, RET_DK), lambda b, h, t: (row(b, h, t), kk + h)),
        pl.BlockSpec((rb, RET_DV), lambda b, h, t: (row(b, h, t), kv + h)),
        pl.BlockSpec((rb, RET_DV), lambda b, h, t: (row(b, h, t), kr + h)),
        pl.BlockSpec((1, c, c), lambda b, h, t: (h, 0, 0)),
        pl.BlockSpec((1, c, RET_DV), lambda b, h, t: (h, 0, 0)),
        pl.BlockSpec((1, c, RET_DK), lambda b, h, t: (h, 0, 0)),
        pl.BlockSpec((1, 1, RET_DV), lambda b, h, t: (h, 0, 0)),
        pl.BlockSpec((1, RET_DV), lambda b, h, t: (0, h)),
        pl.BlockSpec((1, RET_DV), lambda b, h, t: (0, h)),
    ]
    args = [big, big, big, big, dm, qd, kd, cd, gn_g.reshape(1, RET_V), gn_b.reshape(1, RET_V)]
    st_spec = pl.BlockSpec((sb, 1, RET_DK, RET_DV), lambda b, h, t: (b, h, 0, 0))
    if has_s0:
        in_specs.append(st_spec)
        args.append(s0)
    return pl.pallas_call(
        functools.partial(_ret_kernel, c=c, nchunk=nchunk, sb=sb, has_s0=has_s0),
        out_shape=(jax.ShapeDtypeStruct((n, RET_V), BF16),
                   jax.ShapeDtypeStruct((nseq, RET_HEADS, RET_DK, RET_DV), F32)),
        grid=(nseq // sb, RET_HEADS, nt),
        in_specs=in_specs,
        out_specs=(pl.BlockSpec((rb, RET_DV), lambda b, h, t: (row(b, h, t), h)), st_spec),
        compiler_params=_cparams(("parallel", "parallel", "arbitrary")),
        name="retention",
    )(*args)


def _conv_silu(raw_scr, cw, bias, tb):
    taps = SSM_CONV
    acc = bias
    for s in range(taps):
        acc = acc + raw_scr[pl.ds(SUBLANES - s, tb), :] * cw[taps - 1 - s:taps - s, :]
    return _silu(acc)


def _lane_expand(v, width):
    rows = v.shape[0]
    lane = lax.broadcasted_iota(jnp.int32, (rows, 2 * width), 1)
    parts = []
    for m in range(v.shape[1] // 2):
        a = jnp.broadcast_to(v[:, 2 * m:2 * m + 1], (rows, 2 * width))
        b = jnp.broadcast_to(v[:, 2 * m + 1:2 * m + 2], (rows, 2 * width))
        parts.append(jnp.where(lane < width, a, b))
    return jnp.concatenate(parts, axis=1)


def _ssd_kernel(*refs, q, nchunk, sb, has_s0):
    if has_s0:
        (xs_ref, b_ref, c_ref, z_ref, dtc_ref, acc_ref, dtr_ref, acr_ref,
         cwx_ref, cwb_ref, cwc_ref, cbx_ref, cbb_ref, cbc_ref, dsk_ref, ng_ref,
         s0_ref, hx0_ref, hb0_ref, hc0_ref,
         y_ref, so_ref, hx, hb, hc, xc_scr, bc_scr, cc_scr, st_scr) = refs
        hist0 = (hx0_ref, hb0_ref, hc0_ref)
    else:
        (xs_ref, b_ref, c_ref, z_ref, dtc_ref, acc_ref, dtr_ref, acr_ref,
         cwx_ref, cwb_ref, cwc_ref, cbx_ref, cbb_ref, cbc_ref, dsk_ref, ng_ref,
         y_ref, so_ref, hx, hb, hc, xc_scr, bc_scr, cc_scr, st_scr) = refs
        hist0 = (None, None, None)
    t = pl.program_id(2)
    tb = xs_ref.shape[0] // sb

    @pl.when(t == 0)
    def _():
        for si in range(sb):
            if has_s0:
                st_scr[si] = s0_ref[si, 0].T
            else:
                st_scr[si] = jnp.zeros(st_scr.shape[1:], F32)

    for si in range(sb):
        rows = pl.ds(si * tb, tb)
        for raw_ref, raw_scr, h0_ref, cw_ref, cb_ref, dst in (
                (xs_ref, hx, hist0[0], cwx_ref, cbx_ref, xc_scr),
                (b_ref, hb, hist0[1], cwb_ref, cbb_ref, bc_scr),
                (c_ref, hc, hist0[2], cwc_ref, cbc_ref, cc_scr)):
            @pl.when(t == 0)
            def _():
                if has_s0:
                    raw_scr[0:SUBLANES, :] = h0_ref[si]
                else:
                    raw_scr[0:SUBLANES, :] = jnp.zeros((SUBLANES, raw_scr.shape[1]), F32)
            raw_scr[SUBLANES:SUBLANES + tb, :] = raw_ref[rows, :].astype(F32)
            dst[rows, :] = _conv_silu(raw_scr, cw_ref[...], cb_ref[...], tb)
            raw_scr[0:SUBLANES, :] = raw_scr[tb:tb + SUBLANES, :]

    ii = lax.broadcasted_iota(jnp.int32, (q, q), 0)
    jj = lax.broadcasted_iota(jnp.int32, (q, q), 1)
    causal = ii >= jj
    lane = lax.broadcasted_iota(jnp.int32, (q, 2 * HALF_LANES), 1)
    dsk = dsk_ref[...]
    ng = ng_ref[...]
    for si, ci in ((si, ci) for si in range(sb) for ci in range(nchunk)):
        sl = pl.ds((si * nchunk + ci) * q, q)
        x = xc_scr[sl, :]
        bq = bc_scr[sl, :].astype(BF16)
        cq = cc_scr[sl, :].astype(BF16)
        dtc = dtc_ref[0, sl, :]
        acc = acc_ref[0, sl, :]
        dtr = dtr_ref[0, :, sl]
        acr = acr_ref[0, :, sl]
        state = st_scr[si]

        cb = jnp.where(causal, _dot_nt(cq, bq), 0.0)
        pairs = []
        for m in range(SSM_HPG // 2):
            ws = []
            for r in (2 * m, 2 * m + 1):
                seg = acc[:, r:r + 1] - acr[r:r + 1, :]
                ws.append((jnp.exp(jnp.minimum(seg, 0.0)) * (cb * dtr[r:r + 1, :])).astype(BF16))
            xp = x[:, m * 2 * HALF_LANES:(m + 1) * 2 * HALF_LANES]
            x_lo = jnp.where(lane < HALF_LANES, xp, 0.0).astype(BF16)
            x_hi = jnp.where(lane >= HALF_LANES, xp, 0.0).astype(BF16)
            pairs.append(_dot(jnp.concatenate(ws, axis=1), jnp.concatenate([x_lo, x_hi], axis=0)))
        y = jnp.concatenate(pairs, axis=1)

        ea = _lane_expand(jnp.exp(acc), SSM_HEADDIM)
        y = y + _dot(cq, state.astype(BF16)) * ea
        wend = jnp.exp(acc[q - 1:q, :] - acc) * dtc
        xw = (x * _lane_expand(wend, SSM_HEADDIM)).astype(BF16)
        st_scr[si] = state * ea[q - 1:q, :] + _dot_tn(bq, xw)

        y = y + dsk * x
        y = y * _silu(z_ref[sl, :].astype(F32))
        ms = jnp.mean(y * y, axis=-1, keepdims=True)
        y_ref[sl, :] = (y * lax.rsqrt(ms + EPS) * ng).astype(BF16)

    @pl.when(t == pl.num_programs(2) - 1)
    def _():
        for si in range(sb):
            so_ref[si, 0] = st_scr[si].T


def _ssd(big, dts, conv_w, conv_b, d_skip_x, norm_g, s0, hist0, *, nseq, seqlen, q, tb, sb):
    n = big.shape[0]
    nt = seqlen // tb
    assert sb == 1 or nt == 1
    nchunk = tb // q
    rb = sb * tb
    has_s0 = s0 is not None
    dtc, acc, dtr, acr = dts
    gw, ns = SSM_GW, SSM_STATE
    kz, kx = COL_Z // gw, COL_XBC // gw
    kb = (COL_XBC + SSM_INNER) // ns
    kc = kb + SSM_GROUPS
    cb_b = SSM_INNER // ns
    cb_c = cb_b + SSM_GROUPS
    row = lambda b, g, t: b * nt + t
    in_specs = [
        pl.BlockSpec((rb, gw), lambda b, g, t: (row(b, g, t), kx + g)),
        pl.BlockSpec((rb, ns), lambda b, g, t: (row(b, g, t), kb + g)),
        pl.BlockSpec((rb, ns), lambda b, g, t: (row(b, g, t), kc + g)),
        pl.BlockSpec((rb, gw), lambda b, g, t: (row(b, g, t), kz + g)),
        pl.BlockSpec((1, rb, SSM_HPG), lambda b, g, t: (g, row(b, g, t), 0)),
        pl.BlockSpec((1, rb, SSM_HPG), lambda b, g, t: (g, row(b, g, t), 0)),
        pl.BlockSpec((1, SSM_HPG, rb), lambda b, g, t: (g, 0, row(b, g, t))),
        pl.BlockSpec((1, SSM_HPG, rb), lambda b, g, t: (g, 0, row(b, g, t))),
        pl.BlockSpec((SSM_CONV, gw), lambda b, g, t: (0, g)),
        pl.BlockSpec((SSM_CONV, ns), lambda b, g, t: (0, cb_b + g)),
        pl.BlockSpec((SSM_CONV, ns), lambda b, g, t: (0, cb_c + g)),
        pl.BlockSpec((1, gw), lambda b, g, t: (0, g)),
        pl.BlockSpec((1, ns), lambda b, g, t: (0, cb_b + g)),
        pl.BlockSpec((1, ns), lambda b, g, t: (0, cb_c + g)),
        pl.BlockSpec((1, gw), lambda b, g, t: (0, g)),
        pl.BlockSpec((1, gw), lambda b, g, t: (0, g)),
    ]
    cbias = conv_b.reshape(1, SSM_CONV_DIM)
    args = [big, big, big, big, dtc, acc, dtr, acr, conv_w, conv_w, conv_w, cbias, cbias, cbias,
            d_skip_x, norm_g.reshape(1, SSM_INNER)]
    st_spec = pl.BlockSpec((sb, 1, gw, ns), lambda b, g, t: (b, g, 0, 0))
    if has_s0:
        in_specs += [
            st_spec,
            pl.BlockSpec((sb, SUBLANES, gw), lambda b, g, t: (b, 0, g)),
            pl.BlockSpec((sb, SUBLANES, ns), lambda b, g, t: (b, 0, cb_b + g)),
            pl.BlockSpec((sb, SUBLANES, ns), lambda b, g, t: (b, 0, cb_c + g)),
        ]
        args += [s0, hist0, hist0, hist0]
    return pl.pallas_call(
        functools.partial(_ssd_kernel, q=q, nchunk=nchunk, sb=sb, has_s0=has_s0),
        out_shape=(jax.ShapeDtypeStruct((n, SSM_INNER), BF16),
                   jax.ShapeDtypeStruct((nseq, SSM_GROUPS, gw, ns), F32)),
        grid=(nseq // sb, SSM_GROUPS, nt),
        in_specs=in_specs,
        out_specs=(pl.BlockSpec((rb, gw), lambda b, g, t: (row(b, g, t), g)), st_spec),
        scratch_shapes=[pltpu.VMEM((SUBLANES + tb, gw), F32), pltpu.VMEM((SUBLANES + tb, ns), F32),
                        pltpu.VMEM((SUBLANES + tb, ns), F32), pltpu.VMEM((rb, gw), F32),
                        pltpu.VMEM((rb, ns), F32), pltpu.VMEM((rb, ns), F32),
                        pltpu.VMEM((sb, ns, gw), F32)],
        compiler_params=_cparams(("parallel", "parallel", "arbitrary")),
        name="ssd",
    )(*args)


def _merge_kernel(h_ref, ret_ref, ssm_ref, wgr_ref, wgs_ref, wr_ref, ws_ref, o_ref):
    h = h_ref[...]
    gr = jax.nn.sigmoid(_dot(h, wgr_ref[...]))
    gs = jax.nn.sigmoid(_dot(h, wgs_ref[...]))
    a = _dot(ret_ref[...], wr_ref[...])
    b = _dot(ssm_ref[...], ws_ref[...])
    o_ref[...] = (gr * a + gs * b).astype(BF16)


def _merge(h, ret, ssm, w_gr, w_gs, w_r, w_s, *, tm, tn):
    n, d = h.shape
    kr, ks = ret.shape[1], ssm.shape[1]
    return pl.pallas_call(
        _merge_kernel,
        out_shape=jax.ShapeDtypeStruct((n, d), BF16),
        grid=(n // tm, d // tn),
        in_specs=[
            pl.BlockSpec((tm, d), lambda i, j: (i, 0)),
            pl.BlockSpec((tm, kr), lambda i, j: (i, 0)),
            pl.BlockSpec((tm, ks), lambda i, j: (i, 0)),
            pl.BlockSpec((d, tn), lambda i, j: (0, j)),
            pl.BlockSpec((d, tn), lambda i, j: (0, j)),
            pl.BlockSpec((kr, tn), lambda i, j: (0, j)),
            pl.BlockSpec((ks, tn), lambda i, j: (0, j)),
        ],
        out_specs=pl.BlockSpec((tm, tn), lambda i, j: (i, j)),
        compiler_params=_cparams(("parallel", "arbitrary")),
        name="merge",
    )(h, ret, ssm, w_gr, w_gs, w_r, w_s)


def _outproj_kernel(m_ref, w_ref, x_ref, o_ref, *, rc):
    for r in range(m_ref.shape[0] // rc):
        rows = pl.ds(r * rc, rc)
        o_ref[rows, :] = x_ref[rows, :] + _dot(m_ref[rows, :], w_ref[...])


def _outproj(m, w, x, *, tm, tn):
    n, d = x.shape
    k = m.shape[1]
    return pl.pallas_call(
        functools.partial(_outproj_kernel, rc=min(ROW_CHUNK, tm)),
        out_shape=jax.ShapeDtypeStruct((n, d), F32),
        grid=(n // tm, d // tn),
        in_specs=[
            pl.BlockSpec((tm, k), lambda i, j: (i, 0)),
            pl.BlockSpec((k, tn), lambda i, j: (0, j)),
            pl.BlockSpec((tm, tn), lambda i, j: (i, j)),
        ],
        out_specs=pl.BlockSpec((tm, tn), lambda i, j: (i, j)),
        compiler_params=_cparams(("parallel", "arbitrary")),
        name="outproj",
    )(m, w, x)


def _ple_kernel(x_ref, h_ref, p_ref, wp_ref, wg_ref, gf_ref, o_ref):
    pe = _dot(p_ref[...].astype(BF16), wp_ref[...])
    gt = jax.nn.sigmoid(_dot(h_ref[...], wg_ref[...]))
    x = x_ref[...] + pe * gt
    ms = jnp.mean(x * x, axis=-1, keepdims=True)
    o_ref[...] = x * lax.rsqrt(ms + EPS) * gf_ref[...]


def _ple(x, h, p, w_ple, w_gate, g_final, *, tm):
    n, d = x.shape
    pd = p.shape[1]
    return pl.pallas_call(
        _ple_kernel,
        out_shape=jax.ShapeDtypeStruct((n, d), F32),
        grid=(n // tm,),
        in_specs=[
            pl.BlockSpec((tm, d), lambda i: (i, 0)),
            pl.BlockSpec((tm, d), lambda i: (i, 0)),
            pl.BlockSpec((tm, pd), lambda i: (i, 0)),
            pl.BlockSpec((pd, d), lambda i: (0, 0)),
            pl.BlockSpec((d, d), lambda i: (0, 0)),
            pl.BlockSpec((1, d), lambda i: (0, 0)),
        ],
        out_specs=pl.BlockSpec((tm, d), lambda i: (i, 0)),
        compiler_params=_cparams(("parallel",)),
        name="ple",
    )(x, h, p, w_ple, w_gate, g_final)


def _rope_tables(pos0, seqlen, rows):
    half = RET_DK // 2
    inv = ROPE_THETA ** (-jnp.arange(half, dtype=F32) / half)
    pos = (pos0 + jnp.arange(seqlen, dtype=jnp.int32)).astype(F32)
    ang = pos[:, None] * inv[None, :]
    reps = max(rows // seqlen, 1)
    return jnp.tile(jnp.cos(ang), (reps, 1)), jnp.tile(jnp.sin(ang), (reps, 1))


def _pick(n, pref):
    t = pref
    while n % t:
        t //= 2
    return t


def _trunk(x, p, pos0, s_ret, s_ssm, s_conv, w, cfg):
    nseq, seqlen, d = x.shape
    n = nseq * seqlen
    x = x.reshape(n, d)
    p = p.reshape(n, p.shape[-1])
    tm = _pick(n, cfg["tm"])

    x1, h = _ffn(x, w["g_ffn1"], w["g_mix"], w["w1_gu"], w["w1_down"], tm=_pick(n, cfg["tm_ffn"]), tf=cfg["tf"])

    cos, sin = _rope_tables(pos0, seqlen, tm)
    big = _inproj(h, w["w_big"], cos, sin, tm=tm, tn=cfg["tn_in"])

    q = cfg["q_ssd"]
    lpad = -(-seqlen // q) * q
    if lpad != seqlen:
        pad = lambda a: jnp.pad(a.reshape(nseq, seqlen, -1), ((0, 0), (0, lpad - seqlen), (0, 0))).reshape(nseq * lpad, -1)
        big_m, h_m = pad(big), pad(h)
    else:
        big_m, h_m = big, h
    npad = nseq * lpad
    valid = min(seqlen, q)

    dts = _dt_proj(h_m, w["w_dt"], w["w_dt_t"], w["dt_bias"], w["a_log"], tm=_pick(npad, 512), q=q, valid=valid)

    c = cfg["c_ret"] if lpad % cfg["c_ret"] == 0 else q
    tb_ret, tb_ssd = _pick(lpad, cfg["tb_ret"]), _pick(lpad, cfg["tb_ssd"])
    sb = _pick(nseq, cfg["seq_batch"]) if max(tb_ret, tb_ssd) == lpad else 1
    ret, s_ret_new = _retention(big_m, w["ret_gn_g"], w["ret_gn_b"], s_ret, nseq=nseq, seqlen=lpad,
                                c_true=min(seqlen, c), c=c, tb=tb_ret, sb=sb)

    if s_ssm is not None:
        s0 = s_ssm.reshape(nseq, SSM_GROUPS, SSM_GW, SSM_STATE)
        hist0 = jnp.pad(s_conv, ((0, 0), (SUBLANES - (SSM_CONV - 1), 0), (0, 0)))
    else:
        s0, hist0 = None, None
    ssm, s_ssm_new = _ssd(big_m, dts, w["conv_w"], w["conv_b"], w["d_skip_x"], w["ssm_norm_g"], s0, hist0,
                          nseq=nseq, seqlen=lpad, q=q, tb=tb_ssd, sb=sb)
    s_ssm_new = s_ssm_new.reshape(nseq, SSM_HEADS, SSM_HEADDIM, SSM_STATE)

    if lpad != seqlen:
        unpad = lambda a: a.reshape(nseq, lpad, -1)[:, :seqlen].reshape(n, -1)
        ret, ssm = unpad(ret), unpad(ssm)

    keep = min(seqlen, SSM_CONV - 1)
    xbc_tail = big.reshape(nseq, seqlen, N_BIG)[:, seqlen - keep:, COL_XBC:].astype(F32)
    if keep < SSM_CONV - 1:
        prev = jnp.zeros((nseq, SSM_CONV - 1, SSM_CONV_DIM), F32) if s_conv is None else s_conv.astype(F32)
        xbc_tail = jnp.concatenate([prev, xbc_tail], axis=1)[:, -(SSM_CONV - 1):]

    merged = _merge(h, ret, ssm, w["w_gr"], w["w_gs"], w["w_br_ret"], w["w_br_ssm"], tm=_pick(n, cfg["tm_mg"]), tn=cfg["tn_mg"])
    x2 = _outproj(merged, w["w_out"], x1, tm=tm, tn=cfg["tn_out"])
    x3, h3 = _ffn(x2, w["g_ffn2"], w["g_ple"], w["w2_gu"], w["w2_down"], tm=_pick(n, cfg["tm_ffn"]), tf=cfg["tf"])
    y = _ple(x3, h3, p, w["w_ple"], w["w_ple_gate"], w["g_final"], tm=_pick(n, cfg["tm_ple"]))
    return y.reshape(nseq, seqlen, d), s_ret_new, s_ssm_new, xbc_tail


CFG = dict(tm=1024, tm_ffn=512, tf=512, tn_in=1024, q_ssd=128, c_ret=256, tb_ret=1024, tb_ssd=512, seq_batch=4,
           tm_mg=512, tn_mg=256, tn_out=1024, tm_ple=256)


def _prep_weights(g_ffn1, w1_gu, w1_down, g_mix, w_in, ret_gn_g, ret_gn_b, conv_w, conv_b, dt_bias, a_log,
                  d_skip, ssm_norm_g, w_br_ret, w_br_ssm, w_out, g_ffn2, w2_gu, w2_down, g_ple, w_ple,
                  w_ple_gate, g_final):
    b = lambda a: a[0].astype(BF16)
    r = lambda a: a[0].reshape(1, -1).astype(F32)
    win = w_in[0]
    w_dt = win[:, COL_DT:COL_DT + SSM_HEADS].astype(BF16)
    return dict(
        g_ffn1=r(g_ffn1), w1_gu=b(w1_gu), w1_down=b(w1_down), g_mix=r(g_mix),
        w_big=win[:, :N_BIG].astype(BF16), w_dt=w_dt, w_dt_t=w_dt.T,
        w_gr=win[:, COL_GR:COL_GR + D_MODEL].astype(BF16), w_gs=win[:, COL_GS:COL_GS + D_MODEL].astype(BF16),
        ret_gn_g=ret_gn_g[0].astype(F32), ret_gn_b=ret_gn_b[0].astype(F32),
        conv_w=conv_w[0].astype(F32), conv_b=conv_b[0].astype(F32),
        dt_bias=dt_bias[0].astype(F32), a_log=a_log[0].astype(F32),
        d_skip_x=jnp.repeat(d_skip[0].astype(F32), SSM_HEADDIM).reshape(1, SSM_INNER),
        ssm_norm_g=ssm_norm_g[0].astype(F32),
        w_br_ret=b(w_br_ret), w_br_ssm=b(w_br_ssm), w_out=b(w_out),
        g_ffn2=r(g_ffn2), w2_gu=b(w2_gu), w2_down=b(w2_down), g_ple=r(g_ple),
        w_ple=b(w_ple), w_ple_gate=b(w_ple_gate), g_final=g_final.reshape(1, -1).astype(F32),
    )


def kernel(x_prompt, x_sample, state_ret, state_ssm, state_conv, p_prompt, p_sample, g_ffn1, w1_gu, w1_down, g_mix, w_in, ret_gn_g, ret_gn_b, conv_w, conv_b, dt_bias, a_log, d_skip, ssm_norm_g, w_br_ret, w_br_ssm, w_out, g_ffn2, w2_gu, w2_down, g_ple, w_ple, w_ple_gate, g_final):
    assert g_ffn1.shape[0] == 1, "single-layer trunk"
    w = _prep_weights(g_ffn1, w1_gu, w1_down, g_mix, w_in, ret_gn_g, ret_gn_b, conv_w, conv_b, dt_bias, a_log,
                      d_skip, ssm_norm_g, w_br_ret, w_br_ssm, w_out, g_ffn2, w2_gu, w2_down, g_ple, w_ple,
                      w_ple_gate, g_final)
    y_p, ret_p, ssm_p, conv_p = _trunk(x_prompt, p_prompt[0], 0, None, None, None, w, CFG)
    y_s, ret_s, ssm_s, conv_s = _trunk(x_sample, p_sample[0], PAST_LEN, state_ret[0], state_ssm[0],
                                       state_conv[0], w, CFG)
    e = lambda a: a[None]
    return (y_p, y_s, e(ret_p), e(ssm_p), e(conv_p), e(ret_s), e(ssm_s), e(conv_s))
```

```python
import functools

import numpy as np
import jax
import jax.numpy as jnp
from jax import lax
from jax.experimental import pallas as pl
from jax.experimental.pallas import tpu as pltpu

F32 = jnp.float32
BF16 = jnp.bfloat16

D_MODEL = 2048
PAST_LEN = 4096
EPS = 1e-6
RET_HEADS = 8
RET_DK = 256
RET_DV = 512
RET_QK = RET_HEADS * RET_DK
RET_V = RET_HEADS * RET_DV
ROPE_THETA = 10000.0
SSM_INNER = 2 * D_MODEL
SSM_HEADDIM = 64
SSM_HEADS = SSM_INNER // SSM_HEADDIM
SSM_GROUPS = 8
SSM_HPG = SSM_HEADS // SSM_GROUPS
SSM_GW = SSM_HPG * SSM_HEADDIM
SSM_STATE = 128
SSM_CONV = 4
SSM_CONV_DIM = SSM_INNER + 2 * SSM_GROUPS * SSM_STATE
PLE_DIM = 256

COL_Q = 0
COL_K = COL_Q + RET_QK
COL_V = COL_K + RET_QK
COL_RG = COL_V + RET_V
COL_Z = COL_RG + RET_V
COL_XBC = COL_Z + SSM_INNER
N_BIG = COL_XBC + SSM_CONV_DIM
COL_DT = N_BIG
COL_GR = COL_DT + SSM_HEADS
COL_GS = COL_GR + D_MODEL

SUBLANES = 8
HALF_LANES = 64
VMEM_LIMIT = 56 * 1024 * 1024
ROW_CHUNK = 256


def _cparams(sem):
    return pltpu.CompilerParams(dimension_semantics=sem, vmem_limit_bytes=VMEM_LIMIT)


def _rms_bf16(x, g):
    ms = jnp.mean(x * x, axis=-1, keepdims=True)
    return (x * lax.rsqrt(ms + EPS) * g).astype(BF16)


def _silu(x):
    return x * jax.nn.sigmoid(x)


def _dot(a, b):
    return jnp.dot(a, b, preferred_element_type=F32)


def _dot_nt(a, b):
    return lax.dot_general(a, b, (((1,), (1,)), ((), ())), preferred_element_type=F32)


def _dot_tn(a, b):
    return lax.dot_general(a, b, (((0,), (0,)), ((), ())), preferred_element_type=F32)


def _ffn_kernel(x_ref, g_ref, gn_ref, wg_ref, wu_ref, wd_ref, o_ref, hn_ref, h_scr, *, nj):
    j = pl.program_id(1)

    @pl.when(j == 0)
    def _():
        h_scr[...] = _rms_bf16(x_ref[...], g_ref[...])
        o_ref[...] = jnp.zeros_like(o_ref)

    h = h_scr[...]
    gate = _dot(h, wg_ref[...])
    up = _dot(h, wu_ref[...])
    act = (_silu(gate) * up).astype(BF16)
    o_ref[...] += _dot(act, wd_ref[...])

    @pl.when(j == nj - 1)
    def _():
        xn = x_ref[...] + 0.5 * o_ref[...]
        o_ref[...] = xn
        hn_ref[...] = _rms_bf16(xn, gn_ref[...])


def _ffn(x, g, g_next, w_gu, w_down, *, tm, tf):
    n, d = x.shape
    f = w_down.shape[0]
    nj = f // tf
    return pl.pallas_call(
        functools.partial(_ffn_kernel, nj=nj),
        out_shape=(jax.ShapeDtypeStruct((n, d), F32), jax.ShapeDtypeStruct((n, d), BF16)),
        grid=(n // tm, nj),
        in_specs=[
            pl.BlockSpec((tm, d), lambda i, j: (i, 0)),
            pl.BlockSpec((1, d), lambda i, j: (0, 0)),
            pl.BlockSpec((1, d), lambda i, j: (0, 0)),
            pl.BlockSpec((d, tf), lambda i, j: (0, j)),
            pl.BlockSpec((d, tf), lambda i, j: (0, j + nj)),
            pl.BlockSpec((tf, d), lambda i, j: (j, 0)),
        ],
        out_specs=(pl.BlockSpec((tm, d), lambda i, j: (i, 0)),
                   pl.BlockSpec((tm, d), lambda i, j: (i, 0))),
        scratch_shapes=[pltpu.VMEM((tm, d), BF16)],
        compiler_params=_cparams(("parallel", "arbitrary")),
        name="ffn",
    )(x, g, g_next, w_gu, w_gu, w_down)


def _inproj_kernel(*refs, tn, nq, nrope, jx, rc, tiles_per_seq, fuse_conv):
    if fuse_conv:
        h_ref, w_ref, cos_ref, sin_ref, cw_ref, cb_ref, o_ref, tail_ref, cv_scr, hist_scr = refs
    else:
        h_ref, w_ref, cos_ref, sin_ref, o_ref = refs
    i = pl.program_id(0)
    j = pl.program_id(1)
    tm = h_ref.shape[0]
    nr = tm // rc

    @pl.when((j >= nrope) & (j < jx))
    def _():
        for r in range(nr):
            rows = pl.ds(r * rc, rc)
            o_ref[rows, :] = _dot(h_ref[rows, :], w_ref[...]).astype(BF16)

    @pl.when(j < nrope)
    def _():
        scale = jnp.where(j >= nq, RET_DK ** -0.5, 1.0).astype(F32)
        half = RET_DK // 2
        for r in range(nr):
            rows = pl.ds(r * rc, rc)
            acc = _dot(h_ref[rows, :], w_ref[...])
            c = cos_ref[rows, :] * scale
            s = sin_ref[rows, :] * scale
            for hh in range(tn // RET_DK):
                lo = hh * RET_DK
                x1 = acc[:, lo:lo + half]
                x2 = acc[:, lo + half:lo + RET_DK]
                o_ref[rows, lo:lo + half] = (x1 * c - x2 * s).astype(BF16)
                o_ref[rows, lo + half:lo + RET_DK] = (x2 * c + x1 * s).astype(BF16)

    if fuse_conv:
        @pl.when(j >= jx)
        def _():
            jj = j - jx
            seq_start = (i % tiles_per_seq) == 0
            cv_scr[0:SUBLANES, :] = jnp.where(seq_start, 0.0, hist_scr[jj])
            cw = cw_ref[...]
            bias = cb_ref[...]
            for r in range(nr):
                rows = pl.ds(r * rc, rc)
                cv_scr[pl.ds(SUBLANES + r * rc, rc), :] = _dot(h_ref[rows, :], w_ref[...])
                y = bias
                for s in range(SSM_CONV):
                    y = y + cv_scr[pl.ds(SUBLANES + r * rc - s, rc), :] * cw[SSM_CONV - 1 - s:SSM_CONV - s, :]
                o_ref[rows, :] = _silu(y).astype(BF16)
            tail = cv_scr[tm:tm + SUBLANES, :]
            hist_scr[jj] = tail
            tail_ref[0] = tail


def _inproj(h, w_in, cos, sin, conv_w, conv_b, *, tm, tn, seqlen):
    n, d = h.shape
    nb = N_BIG
    nrow = cos.shape[0] // tm
    fuse_conv = seqlen % tm == 0
    nj = nb // tn
    jx = COL_XBC // tn if fuse_conv else nj
    nx = nj - jx
    kern = functools.partial(_inproj_kernel, tn=tn, nq=RET_QK // tn, nrope=2 * RET_QK // tn, jx=jx,
                             rc=min(ROW_CHUNK, tm), tiles_per_seq=max(seqlen // tm, 1), fuse_conv=fuse_conv)
    in_specs = [
        pl.BlockSpec((tm, d), lambda i, j: (i, 0)),
        pl.BlockSpec((d, tn), lambda i, j: (0, j)),
        pl.BlockSpec((tm, RET_DK // 2), lambda i, j: (i % nrow, 0)),
        pl.BlockSpec((tm, RET_DK // 2), lambda i, j: (i % nrow, 0)),
    ]
    big_spec = pl.BlockSpec((tm, tn), lambda i, j: (i, j))
    big_shape = jax.ShapeDtypeStruct((n, nb), BF16)
    if not fuse_conv:
        big = pl.pallas_call(
            kern, out_shape=big_shape, grid=(n // tm, nj), in_specs=in_specs, out_specs=big_spec,
            compiler_params=_cparams(("parallel", "arbitrary")), name="inproj",
        )(h, w_in, cos, sin)
        return big, None
    xcol = lambda i, j: (0, jnp.maximum(j - jx, 0))
    in_specs += [pl.BlockSpec((SSM_CONV, tn), xcol), pl.BlockSpec((1, tn), xcol)]
    return pl.pallas_call(
        kern,
        out_shape=(big_shape, jax.ShapeDtypeStruct((n // tm, SUBLANES, SSM_CONV_DIM), F32)),
        grid=(n // tm, nj),
        in_specs=in_specs,
        out_specs=(big_spec, pl.BlockSpec((1, SUBLANES, tn), lambda i, j: (i, 0, jnp.maximum(j - jx, 0)))),
        scratch_shapes=[pltpu.VMEM((SUBLANES + tm, tn), F32), pltpu.VMEM((nx, SUBLANES, tn), F32)],
        compiler_params=_cparams(("arbitrary", "arbitrary")),
        name="inproj",
    )(h, w_in, cos, sin, conv_w, conv_b.reshape(1, SSM_CONV_DIM))


def _softplus(x):
    return jnp.maximum(x, 0.0) + jnp.log1p(jnp.exp(-jnp.abs(x)))


def _dt_kernel(h_ref, w_ref, wt_ref, br_ref, bc_ref, ar_ref, ac_ref,
               dtc_ref, acc_ref, dtr_ref, acr_ref, *, q, valid):
    h = h_ref[...]
    tm = h.shape[0]
    raw_c = _dot(h, w_ref[...])
    raw_r = _dot_nt(wt_ref[...], h)
    dt_c = _softplus(raw_c + br_ref[...])
    dt_r = _softplus(raw_r + bc_ref[...])
    if valid < q:
        row = lax.broadcasted_iota(jnp.int32, dt_c.shape, 0) % q
        col = lax.broadcasted_iota(jnp.int32, dt_r.shape, 1) % q
        dt_c = jnp.where(row < valid, dt_c, 0.0)
        dt_r = jnp.where(col < valid, dt_r, 0.0)
    dta_c = dt_c * (-jnp.exp(ar_ref[...]))
    dta_r = dt_r * (-jnp.exp(ac_ref[...]))
    i = lax.broadcasted_iota(jnp.int32, (tm, tm), 0)
    j = lax.broadcasted_iota(jnp.int32, (tm, tm), 1)
    same = (i // q) == (j // q)
    tri = jnp.where(same & (j <= i), 1.0, 0.0).astype(F32)
    trit = jnp.where(same & (i <= j), 1.0, 0.0).astype(F32)
    ac_c = jnp.dot(tri, dta_c, preferred_element_type=F32, precision=lax.Precision.HIGHEST)
    ac_r = jnp.dot(dta_r, trit, preferred_element_type=F32, precision=lax.Precision.HIGHEST)
    for g in range(SSM_GROUPS):
        lo = g * SSM_HPG
        dtc_ref[g] = dt_c[:, lo:lo + SSM_HPG]
        acc_ref[g] = ac_c[:, lo:lo + SSM_HPG]
        dtr_ref[g] = dt_r[lo:lo + SSM_HPG, :]
        acr_ref[g] = ac_r[lo:lo + SSM_HPG, :]


def _dt_proj(h, w_dt, w_dt_t, dt_bias, a_log, *, tm, q, valid):
    n, d = h.shape
    hh = SSM_HEADS
    col = jax.ShapeDtypeStruct((SSM_GROUPS, n, SSM_HPG), F32)
    row = jax.ShapeDtypeStruct((SSM_GROUPS, SSM_HPG, n), F32)
    return pl.pallas_call(
        functools.partial(_dt_kernel, q=q, valid=valid),
        out_shape=(col, col, row, row),
        grid=(n // tm,),
        in_specs=[
            pl.BlockSpec((tm, d), lambda i: (i, 0)),
            pl.BlockSpec((d, hh), lambda i: (0, 0)),
            pl.BlockSpec((hh, d), lambda i: (0, 0)),
            pl.BlockSpec((1, hh), lambda i: (0, 0)),
            pl.BlockSpec((hh, 1), lambda i: (0, 0)),
            pl.BlockSpec((1, hh), lambda i: (0, 0)),
            pl.BlockSpec((hh, 1), lambda i: (0, 0)),
        ],
        out_specs=(
            pl.BlockSpec((SSM_GROUPS, tm, SSM_HPG), lambda i: (0, i, 0)),
            pl.BlockSpec((SSM_GROUPS, tm, SSM_HPG), lambda i: (0, i, 0)),
            pl.BlockSpec((SSM_GROUPS, SSM_HPG, tm), lambda i: (0, 0, i)),
            pl.BlockSpec((SSM_GROUPS, SSM_HPG, tm), lambda i: (0, 0, i)),
        ),
        compiler_params=_cparams(("parallel",)),
        name="dt_proj",
    )(h, w_dt, w_dt_t, dt_bias.reshape(1, hh), dt_bias.reshape(hh, 1),
      a_log.reshape(1, hh), a_log.reshape(hh, 1))


def _ret_kernel(*refs, c, nchunk, sb, has_s0):
    if has_s0:
        (q_ref, k_ref, v_ref, rg_ref, dm_ref, qd_ref, kd_ref, cd_ref, gg_ref, gb_ref, s0_ref,
         o_ref, s_ref) = refs
    else:
        (q_ref, k_ref, v_ref, rg_ref, dm_ref, qd_ref, kd_ref, cd_ref, gg_ref, gb_ref,
         o_ref, s_ref) = refs
    t = pl.program_id(2)

    @pl.when(t == 0)
    def _():
        if has_s0:
            s_ref[...] = s0_ref[...]
        else:
            s_ref[...] = jnp.zeros_like(s_ref)

    dm = dm_ref[0]
    qd = qd_ref[0]
    kd = kd_ref[0]
    cd = cd_ref[0]
    gg = gg_ref[...]
    gb = gb_ref[...]
    for si, ci in ((si, ci) for si in range(sb) for ci in range(nchunk)):
        sl = pl.ds((si * nchunk + ci) * c, c)
        q = q_ref[sl, :]
        k = k_ref[sl, :]
        v = v_ref[sl, :]
        s = s_ref[si, 0]
        att = _dot_nt(q, k) * dm
        o = _dot(att.astype(BF16), v) + _dot(q, s.astype(BF16)) * qd
        kdk = (k.astype(F32) * kd).astype(BF16)
        s_ref[si, 0] = s * cd + _dot_tn(kdk, v)
        mu = jnp.mean(o, axis=-1, keepdims=True)
        dev = o - mu
        var = jnp.mean(dev * dev, axis=-1, keepdims=True)
        on = dev * lax.rsqrt(var + EPS)
        rg = rg_ref[sl, :].astype(F32)
        o_ref[sl, :] = (_silu(rg) * (on * gg + gb)).astype(BF16)


def _ret_tables(c_true, c):
    hs = np.arange(RET_HEADS, dtype=np.float64)
    log_g = np.log1p(-np.exp2(-5.0 - hs))
    idx = np.arange(c, dtype=np.float64)
    diff = idx[:, None] - idx[None, :]
    dmat = np.where(diff[None] >= 0, np.exp(np.maximum(diff, 0.0)[None] * log_g[:, None, None]), 0.0)
    q_dec = np.exp((idx[None, :] + 1.0) * log_g[:, None])
    k_dec = np.where(idx[None, :] < c_true,
                     np.exp(np.maximum(c_true - 1.0 - idx[None, :], 0.0) * log_g[:, None]), 0.0)
    c_dec = np.exp(c_true * log_g)
    qd = np.broadcast_to(q_dec[:, :, None], (RET_HEADS, c, RET_DV))
    kd = np.broadcast_to(k_dec[:, :, None], (RET_HEADS, c, RET_DK))
    cd = np.broadcast_to(c_dec[:, None, None], (RET_HEADS, 1, RET_DV))
    f = lambda a: jnp.asarray(np.ascontiguousarray(a), dtype=F32)
    return f(dmat), f(qd), f(kd), f(cd)


def _retention(big, gn_g, gn_b, s0, *, nseq, seqlen, c_true, c, tb, sb):
    n = big.shape[0]
    nt = seqlen // tb
    assert sb == 1 or nt == 1
    nchunk = tb // c
    rb = sb * tb
    dm, qd, kd, cd = _ret_tables(c_true, c)
    has_s0 = s0 is not None
    kq, kk = COL_Q // RET_DK, COL_K // RET_DK
    kv, kr = COL_V // RET_DV, COL_RG // RET_DV
    row = lambda b, h, t: b * nt + t
    in_specs = [
        pl.BlockSpec((rb, RET_DK), lambda b, h, t: (row(b, h, t), kq + h)),
        pl.BlockSpec((rb, RET_DK), lambda b, h, t: (row(b, h, t), kk + h)),
        pl.BlockSpec((rb, RET_DV), lambda b, h, t: (row(b, h, t), kv + h)),
        pl.BlockSpec((rb, RET_DV), lambda b, h, t: (row(b, h, t), kr + h)),
        pl.BlockSpec((1, c, c), lambda b, h, t: (h, 0, 0)),
        pl.BlockSpec((1, c, RET_DV), lambda b, h, t: (h, 0, 0)),
        pl.BlockSpec((1, c, RET_DK), lambda b, h, t: (h, 0, 0)),
        pl.BlockSpec((1, 1, RET_DV), lambda b, h, t: (h, 0, 0)),
        pl.BlockSpec((1, RET_DV), lambda b, h, t: (0, h)),
        pl.BlockSpec((1, RET_DV), lambda b, h, t: (0, h)),
    ]
    args = [big, big, big, big, dm, qd, kd, cd, gn_g.reshape(1, RET_V), gn_b.reshape(1, RET_V)]
    st_spec = pl.BlockSpec((sb, 1, RET_DK, RET_DV), lambda b, h, t: (b, h, 0, 0))
    if has_s0:
        in_specs.append(st_spec)
        args.append(s0)
    return pl.pallas_call(
        functools.partial(_ret_kernel, c=c, nchunk=nchunk, sb=sb, has_s0=has_s0),
        out_shape=(jax.ShapeDtypeStruct((n, RET_V), BF16),
                   jax.ShapeDtypeStruct((nseq, RET_HEADS, RET_DK, RET_DV), F32)),
        grid=(nseq // sb, RET_HEADS, nt),
        in_specs=in_specs,
        out_specs=(pl.BlockSpec((rb, RET_DV), lambda b, h, t: (row(b, h, t), h)), st_spec),
        compiler_params=_cparams(("parallel", "parallel", "arbitrary")),
        name="retention",
    )(*args)


def _conv_silu(raw_scr, cw, bias, tb):
    taps = SSM_CONV
    acc = bias
    for s in range(taps):
        acc = acc + raw_scr[pl.ds(SUBLANES - s, tb), :] * cw[taps - 1 - s:taps - s, :]
    return _silu(acc)


def _lane_expand(v, width):
    rows = v.shape[0]
    lane = lax.broadcasted_iota(jnp.int32, (rows, 2 * width), 1)
    parts = []
    for m in range(v.shape[1] // 2):
        a = jnp.broadcast_to(v[:, 2 * m:2 * m + 1], (rows, 2 * width))
        b = jnp.broadcast_to(v[:, 2 * m + 1:2 * m + 2], (rows, 2 * width))
        parts.append(jnp.where(lane < width, a, b))
    return jnp.concatenate(parts, axis=1)


def _ssd_kernel(*refs, q, nchunk, sb, has_s0, conv_done):
    if conv_done:
        (xs_ref, b_ref, c_ref, z_ref, dtc_ref, acc_ref, dtr_ref, acr_ref, dsk_ref, ng_ref,
         y_ref, so_ref, st_scr) = refs
        assert not has_s0
    elif has_s0:
        (xs_ref, b_ref, c_ref, z_ref, dtc_ref, acc_ref, dtr_ref, acr_ref,
         cwx_ref, cwb_ref, cwc_ref, cbx_ref, cbb_ref, cbc_ref, dsk_ref, ng_ref,
         s0_ref, hx0_ref, hb0_ref, hc0_ref,
         y_ref, so_ref, hx, hb, hc, xc_scr, bc_scr, cc_scr, st_scr) = refs
        hist0 = (hx0_ref, hb0_ref, hc0_ref)
    else:
        (xs_ref, b_ref, c_ref, z_ref, dtc_ref, acc_ref, dtr_ref, acr_ref,
         cwx_ref, cwb_ref, cwc_ref, cbx_ref, cbb_ref, cbc_ref, dsk_ref, ng_ref,
         y_ref, so_ref, hx, hb, hc, xc_scr, bc_scr, cc_scr, st_scr) = refs
        hist0 = (None, None, None)
    t = pl.program_id(2)
    tb = xs_ref.shape[0] // sb

    @pl.when(t == 0)
    def _():
        for si in range(sb):
            if has_s0:
                st_scr[si] = s0_ref[si, 0].T
            else:
                st_scr[si] = jnp.zeros(st_scr.shape[1:], F32)

    for si in range(0 if conv_done else sb):
        rows = pl.ds(si * tb, tb)
        for raw_ref, raw_scr, h0_ref, cw_ref, cb_ref, dst in (
                (xs_ref, hx, hist0[0], cwx_ref, cbx_ref, xc_scr),
                (b_ref, hb, hist0[1], cwb_ref, cbb_ref, bc_scr),
                (c_ref, hc, hist0[2], cwc_ref, cbc_ref, cc_scr)):
            @pl.when(t == 0)
            def _():
                if has_s0:
                    raw_scr[0:SUBLANES, :] = h0_ref[si]
                else:
                    raw_scr[0:SUBLANES, :] = jnp.zeros((SUBLANES, raw_scr.shape[1]), F32)
            raw_scr[SUBLANES:SUBLANES + tb, :] = raw_ref[rows, :].astype(F32)
            dst[rows, :] = _conv_silu(raw_scr, cw_ref[...], cb_ref[...], tb)
            raw_scr[0:SUBLANES, :] = raw_scr[tb:tb + SUBLANES, :]

    ii = lax.broadcasted_iota(jnp.int32, (q, q), 0)
    jj = lax.broadcasted_iota(jnp.int32, (q, q), 1)
    causal = ii >= jj
    lane = lax.broadcasted_iota(jnp.int32, (q, 2 * HALF_LANES), 1)
    dsk = dsk_ref[...]
    ng = ng_ref[...]
    for si, ci in ((si, ci) for si in range(sb) for ci in range(nchunk)):
        sl = pl.ds((si * nchunk + ci) * q, q)
        if conv_done:
            x = xs_ref[sl, :].astype(F32)
            bq = b_ref[sl, :]
            cq = c_ref[sl, :]
        else:
            x = xc_scr[sl, :]
            bq = bc_scr[sl, :].astype(BF16)
            cq = cc_scr[sl, :].astype(BF16)
        dtc = dtc_ref[0, sl, :]
        acc = acc_ref[0, sl, :]
        dtr = dtr_ref[0, :, sl]
        acr = acr_ref[0, :, sl]
        state = st_scr[si]

        cb = jnp.where(causal, _dot_nt(cq, bq), 0.0)
        pairs = []
        for m in range(SSM_HPG // 2):
            ws = []
            for r in (2 * m, 2 * m + 1):
                seg = acc[:, r:r + 1] - acr[r:r + 1, :]
                ws.append((jnp.exp(jnp.minimum(seg, 0.0)) * (cb * dtr[r:r + 1, :])).astype(BF16))
            xp = x[:, m * 2 * HALF_LANES:(m + 1) * 2 * HALF_LANES]
            x_lo = jnp.where(lane < HALF_LANES, xp, 0.0).astype(BF16)
            x_hi = jnp.where(lane >= HALF_LANES, xp, 0.0).astype(BF16)
            pairs.append(_dot(jnp.concatenate(ws, axis=1), jnp.concatenate([x_lo, x_hi], axis=0)))
        y = jnp.concatenate(pairs, axis=1)

        ea = _lane_expand(jnp.exp(acc), SSM_HEADDIM)
        y = y + _dot(cq, state.astype(BF16)) * ea
        wend = jnp.exp(acc[q - 1:q, :] - acc) * dtc
        xw = (x * _lane_expand(wend, SSM_HEADDIM)).astype(BF16)
        st_scr[si] = state * ea[q - 1:q, :] + _dot_tn(bq, xw)

        y = y + dsk * x
        y = y * _silu(z_ref[sl, :].astype(F32))
        ms = jnp.mean(y * y, axis=-1, keepdims=True)
        y_ref[sl, :] = (y * lax.rsqrt(ms + EPS) * ng).astype(BF16)

    @pl.when(t == pl.num_programs(2) - 1)
    def _():
        for si in range(sb):
            so_ref[si, 0] = st_scr[si].T


def _ssd(big, dts, conv_w, conv_b, d_skip_x, norm_g, s0, hist0, *, nseq, seqlen, q, tb, sb, conv_done):
    n = big.shape[0]
    nt = seqlen // tb
    assert sb == 1 or nt == 1
    nchunk = tb // q
    rb = sb * tb
    has_s0 = s0 is not None
    dtc, acc, dtr, acr = dts
    gw, ns = SSM_GW, SSM_STATE
    kz, kx = COL_Z // gw, COL_XBC // gw
    kb = (COL_XBC + SSM_INNER) // ns
    kc = kb + SSM_GROUPS
    cb_b = SSM_INNER // ns
    cb_c = cb_b + SSM_GROUPS
    row = lambda b, g, t: b * nt + t
    in_specs = [
        pl.BlockSpec((rb, gw), lambda b, g, t: (row(b, g, t), kx + g)),
        pl.BlockSpec((rb, ns), lambda b, g, t: (row(b, g, t), kb + g)),
        pl.BlockSpec((rb, ns), lambda b, g, t: (row(b, g, t), kc + g)),
        pl.BlockSpec((rb, gw), lambda b, g, t: (row(b, g, t), kz + g)),
        pl.BlockSpec((1, rb, SSM_HPG), lambda b, g, t: (g, row(b, g, t), 0)),
        pl.BlockSpec((1, rb, SSM_HPG), lambda b, g, t: (g, row(b, g, t), 0)),
        pl.BlockSpec((1, SSM_HPG, rb), lambda b, g, t: (g, 0, row(b, g, t))),
        pl.BlockSpec((1, SSM_HPG, rb), lambda b, g, t: (g, 0, row(b, g, t))),
    ]
    args = [big, big, big, big, dtc, acc, dtr, acr]
    if not conv_done:
        in_specs += [
            pl.BlockSpec((SSM_CONV, gw), lambda b, g, t: (0, g)),
            pl.BlockSpec((SSM_CONV, ns), lambda b, g, t: (0, cb_b + g)),
            pl.BlockSpec((SSM_CONV, ns), lambda b, g, t: (0, cb_c + g)),
            pl.BlockSpec((1, gw), lambda b, g, t: (0, g)),
            pl.BlockSpec((1, ns), lambda b, g, t: (0, cb_b + g)),
            pl.BlockSpec((1, ns), lambda b, g, t: (0, cb_c + g)),
        ]
        cbias = conv_b.reshape(1, SSM_CONV_DIM)
        args += [conv_w, conv_w, conv_w, cbias, cbias, cbias]
    in_specs += [pl.BlockSpec((1, gw), lambda b, g, t: (0, g)), pl.BlockSpec((1, gw), lambda b, g, t: (0, g))]
    args += [d_skip_x, norm_g.reshape(1, SSM_INNER)]
    st_spec = pl.BlockSpec((sb, 1, gw, ns), lambda b, g, t: (b, g, 0, 0))
    if has_s0:
        in_specs += [
            st_spec,
            pl.BlockSpec((sb, SUBLANES, gw), lambda b, g, t: (b, 0, g)),
            pl.BlockSpec((sb, SUBLANES, ns), lambda b, g, t: (b, 0, cb_b + g)),
            pl.BlockSpec((sb, SUBLANES, ns), lambda b, g, t: (b, 0, cb_c + g)),
        ]
        args += [s0, hist0, hist0, hist0]
    return pl.pallas_call(
        functools.partial(_ssd_kernel, q=q, nchunk=nchunk, sb=sb, has_s0=has_s0, conv_done=conv_done),
        out_shape=(jax.ShapeDtypeStruct((n, SSM_INNER), BF16),
                   jax.ShapeDtypeStruct((nseq, SSM_GROUPS, gw, ns), F32)),
        grid=(nseq // sb, SSM_GROUPS, nt),
        in_specs=in_specs,
        out_specs=(pl.BlockSpec((rb, gw), lambda b, g, t: (row(b, g, t), g)), st_spec),
        scratch_shapes=([] if conv_done else
                        [pltpu.VMEM((SUBLANES + tb, gw), F32), pltpu.VMEM((SUBLANES + tb, ns), F32),
                         pltpu.VMEM((SUBLANES + tb, ns), F32), pltpu.VMEM((rb, gw), F32),
                         pltpu.VMEM((rb, ns), F32), pltpu.VMEM((rb, ns), F32)])
        + [pltpu.VMEM((sb, ns, gw), F32)],
        compiler_params=_cparams(("parallel", "parallel", "arbitrary")),
        name="ssd",
    )(*args)


def _merge_kernel(h_ref, ret_ref, ssm_ref, wgr_ref, wgs_ref, wr_ref, ws_ref, o_ref):
    h = h_ref[...]
    gr = jax.nn.sigmoid(_dot(h, wgr_ref[...]))
    gs = jax.nn.sigmoid(_dot(h, wgs_ref[...]))
    a = _dot(ret_ref[...], wr_ref[...])
    b = _dot(ssm_ref[...], ws_ref[...])
    o_ref[...] = (gr * a + gs * b).astype(BF16)


def _merge(h, ret, ssm, w_gr, w_gs, w_r, w_s, *, tm, tn):
    n, d = h.shape
    kr, ks = ret.shape[1], ssm.shape[1]
    return pl.pallas_call(
        _merge_kernel,
        out_shape=jax.ShapeDtypeStruct((n, d), BF16),
        grid=(n // tm, d // tn),
        in_specs=[
            pl.BlockSpec((tm, d), lambda i, j: (i, 0)),
            pl.BlockSpec((tm, kr), lambda i, j: (i, 0)),
            pl.BlockSpec((tm, ks), lambda i, j: (i, 0)),
            pl.BlockSpec((d, tn), lambda i, j: (0, j)),
            pl.BlockSpec((d, tn), lambda i, j: (0, j)),
            pl.BlockSpec((kr, tn), lambda i, j: (0, j)),
            pl.BlockSpec((ks, tn), lambda i, j: (0, j)),
        ],
        out_specs=pl.BlockSpec((tm, tn), lambda i, j: (i, j)),
        compiler_params=_cparams(("parallel", "arbitrary")),
        name="merge",
    )(h, ret, ssm, w_gr, w_gs, w_r, w_s)


def _outproj_kernel(m_ref, w_ref, x_ref, o_ref, *, rc):
    for r in range(m_ref.shape[0] // rc):
        rows = pl.ds(r * rc, rc)
        o_ref[rows, :] = x_ref[rows, :] + _dot(m_ref[rows, :], w_ref[...])


def _outproj(m, w, x, *, tm, tn):
    n, d = x.shape
    k = m.shape[1]
    return pl.pallas_call(
        functools.partial(_outproj_kernel, rc=min(ROW_CHUNK, tm)),
        out_shape=jax.ShapeDtypeStruct((n, d), F32),
        grid=(n // tm, d // tn),
        in_specs=[
            pl.BlockSpec((tm, k), lambda i, j: (i, 0)),
            pl.BlockSpec((k, tn), lambda i, j: (0, j)),
            pl.BlockSpec((tm, tn), lambda i, j: (i, j)),
        ],
        out_specs=pl.BlockSpec((tm, tn), lambda i, j: (i, j)),
        compiler_params=_cparams(("parallel", "arbitrary")),
        name="outproj",
    )(m, w, x)


def _ple_kernel(x_ref, h_ref, p_ref, wp_ref, wg_ref, gf_ref, o_ref):
    pe = _dot(p_ref[...].astype(BF16), wp_ref[...])
    gt = jax.nn.sigmoid(_dot(h_ref[...], wg_ref[...]))
    x = x_ref[...] + pe * gt
    ms = jnp.mean(x * x, axis=-1, keepdims=True)
    o_ref[...] = x * lax.rsqrt(ms + EPS) * gf_ref[...]


def _ple(x, h, p, w_ple, w_gate, g_final, *, tm):
    n, d = x.shape
    pd = p.shape[1]
    return pl.pallas_call(
        _ple_kernel,
        out_shape=jax.ShapeDtypeStruct((n, d), F32),
        grid=(n // tm,),
        in_specs=[
            pl.BlockSpec((tm, d), lambda i: (i, 0)),
            pl.BlockSpec((tm, d), lambda i: (i, 0)),
            pl.BlockSpec((tm, pd), lambda i: (i, 0)),
            pl.BlockSpec((pd, d), lambda i: (0, 0)),
            pl.BlockSpec((d, d), lambda i: (0, 0)),
            pl.BlockSpec((1, d), lambda i: (0, 0)),
        ],
        out_specs=pl.BlockSpec((tm, d), lambda i: (i, 0)),
        compiler_params=_cparams(("parallel",)),
        name="ple",
    )(x, h, p, w_ple, w_gate, g_final)


def _rope_tables(pos0, seqlen, rows):
    half = RET_DK // 2
    inv = ROPE_THETA ** (-jnp.arange(half, dtype=F32) / half)
    pos = (pos0 + jnp.arange(seqlen, dtype=jnp.int32)).astype(F32)
    ang = pos[:, None] * inv[None, :]
    reps = max(rows // seqlen, 1)
    return jnp.tile(jnp.cos(ang), (reps, 1)), jnp.tile(jnp.sin(ang), (reps, 1))


def _pick(n, pref):
    t = pref
    while n % t:
        t //= 2
    return t


def _trunk(x, p, pos0, s_ret, s_ssm, s_conv, w, cfg):
    nseq, seqlen, d = x.shape
    n = nseq * seqlen
    x = x.reshape(n, d)
    p = p.reshape(n, p.shape[-1])
    tm = _pick(n, cfg["tm"])

    x1, h = _ffn(x, w["g_ffn1"], w["g_mix"], w["w1_gu"], w["w1_down"], tm=_pick(n, cfg["tm_ffn"]), tf=cfg["tf"])

    cos, sin = _rope_tables(pos0, seqlen, tm)
    big, tail = _inproj(h, w["w_in"], cos, sin, w["conv_w"], w["conv_b"], tm=tm, tn=cfg["tn_in"], seqlen=seqlen)

    q = cfg["q_ssd"]
    lpad = -(-seqlen // q) * q
    if lpad != seqlen:
        pad = lambda a: jnp.pad(a.reshape(nseq, seqlen, -1), ((0, 0), (0, lpad - seqlen), (0, 0))).reshape(nseq * lpad, -1)
        big_m, h_m = pad(big), pad(h)
    else:
        big_m, h_m = big, h
    npad = nseq * lpad
    valid = min(seqlen, q)

    dts = _dt_proj(h_m, w["w_dt"], w["w_dt_t"], w["dt_bias"], w["a_log"], tm=_pick(npad, 512), q=q, valid=valid)

    c = cfg["c_ret"] if lpad % cfg["c_ret"] == 0 else q
    tb_ret, tb_ssd = _pick(lpad, cfg["tb_ret"]), _pick(lpad, cfg["tb_ssd"])
    sb = _pick(nseq, cfg["seq_batch"]) if max(tb_ret, tb_ssd) == lpad else 1
    ret, s_ret_new = _retention(big_m, w["ret_gn_g"], w["ret_gn_b"], s_ret, nseq=nseq, seqlen=lpad,
                                c_true=min(seqlen, c), c=c, tb=tb_ret, sb=sb)

    if s_ssm is not None:
        s0 = s_ssm.reshape(nseq, SSM_GROUPS, SSM_GW, SSM_STATE)
        hist0 = jnp.pad(s_conv, ((0, 0), (SUBLANES - (SSM_CONV - 1), 0), (0, 0)))
    else:
        s0, hist0 = None, None
    ssm, s_ssm_new = _ssd(big_m, dts, w["conv_w"], w["conv_b"], w["d_skip_x"], w["ssm_norm_g"], s0, hist0,
                          nseq=nseq, seqlen=lpad, q=q, tb=tb_ssd, sb=sb, conv_done=tail is not None)
    s_ssm_new = s_ssm_new.reshape(nseq, SSM_HEADS, SSM_HEADDIM, SSM_STATE)

    if lpad != seqlen:
        unpad = lambda a: a.reshape(nseq, lpad, -1)[:, :seqlen].reshape(n, -1)
        ret, ssm = unpad(ret), unpad(ssm)

    if tail is not None:
        xbc_tail = tail.reshape(nseq, seqlen // tm, SUBLANES, SSM_CONV_DIM)[:, -1, SUBLANES - (SSM_CONV - 1):]
    else:
        keep = min(seqlen, SSM_CONV - 1)
        xbc_tail = big.reshape(nseq, seqlen, N_BIG)[:, seqlen - keep:, COL_XBC:].astype(F32)
        if keep < SSM_CONV - 1:
            prev = jnp.zeros((nseq, SSM_CONV - 1, SSM_CONV_DIM), F32) if s_conv is None else s_conv.astype(F32)
            xbc_tail = jnp.concatenate([prev, xbc_tail], axis=1)[:, -(SSM_CONV - 1):]

    merged = _merge(h, ret, ssm, w["w_gr"], w["w_gs"], w["w_br_ret"], w["w_br_ssm"], tm=_pick(n, cfg["tm_mg"]), tn=cfg["tn_mg"])
    x2 = _outproj(merged, w["w_out"], x1, tm=tm, tn=cfg["tn_out"])
    x3, h3 = _ffn(x2, w["g_ffn2"], w["g_ple"], w["w2_gu"], w["w2_down"], tm=_pick(n, cfg["tm_ffn"]), tf=cfg["tf"])
    y = _ple(x3, h3, p, w["w_ple"], w["w_ple_gate"], w["g_final"], tm=_pick(n, cfg["tm_ple"]))
    return y.reshape(nseq, seqlen, d), s_ret_new, s_ssm_new, xbc_tail


CFG = dict(tm=1024, tm_ffn=512, tf=512, tn_in=1024, q_ssd=128, c_ret=256, tb_ret=1024, tb_ssd=256, seq_batch=4,
           tm_mg=512, tn_mg=256, tn_out=1024, tm_ple=256)


def _prep_weights(g_ffn1, w1_gu, w1_down, g_mix, w_in, ret_gn_g, ret_gn_b, conv_w, conv_b, dt_bias, a_log,
                  d_skip, ssm_norm_g, w_br_ret, w_br_ssm, w_out, g_ffn2, w2_gu, w2_down, g_ple, w_ple,
                  w_ple_gate, g_final):
    b = lambda a: a[0].astype(BF16)
    r = lambda a: a[0].reshape(1, -1).astype(F32)
    win = w_in[0]
    w_dt = win[:, COL_DT:COL_DT + SSM_HEADS].astype(BF16)
    return dict(
        g_ffn1=r(g_ffn1), w1_gu=b(w1_gu), w1_down=b(w1_down), g_mix=r(g_mix),
        w_in=win.astype(BF16), w_dt=w_dt, w_dt_t=w_dt.T,
        w_gr=win[:, COL_GR:COL_GR + D_MODEL].astype(BF16), w_gs=win[:, COL_GS:COL_GS + D_MODEL].astype(BF16),
        ret_gn_g=ret_gn_g[0].astype(F32), ret_gn_b=ret_gn_b[0].astype(F32),
        conv_w=conv_w[0].astype(F32), conv_b=conv_b[0].astype(F32),
        dt_bias=dt_bias[0].astype(F32), a_log=a_log[0].astype(F32),
        d_skip_x=jnp.repeat(d_skip[0].astype(F32), SSM_HEADDIM).reshape(1, SSM_INNER),
        ssm_norm_g=ssm_norm_g[0].astype(F32),
        w_br_ret=b(w_br_ret), w_br_ssm=b(w_br_ssm), w_out=b(w_out),
        g_ffn2=r(g_ffn2), w2_gu=b(w2_gu), w2_down=b(w2_down), g_ple=r(g_ple),
        w_ple=b(w_ple), w_ple_gate=b(w_ple_gate), g_final=g_final.reshape(1, -1).astype(F32),
    )


def kernel(x_prompt, x_sample, state_ret, state_ssm, state_conv, p_prompt, p_sample, g_ffn1, w1_gu, w1_down, g_mix, w_in, ret_gn_g, ret_gn_b, conv_w, conv_b, dt_bias, a_log, d_skip, ssm_norm_g, w_br_ret, w_br_ssm, w_out, g_ffn2, w2_gu, w2_down, g_ple, w_ple, w_ple_gate, g_final):
    assert g_ffn1.shape[0] == 1, "single-layer trunk"
    w = _prep_weights(g_ffn1, w1_gu, w1_down, g_mix, w_in, ret_gn_g, ret_gn_b, conv_w, conv_b, dt_bias, a_log,
                      d_skip, ssm_norm_g, w_br_ret, w_br_ssm, w_out, g_ffn2, w2_gu, w2_down, g_ple, w_ple,
                      w_ple_gate, g_final)
    y_p, ret_p, ssm_p, conv_p = _trunk(x_prompt, p_prompt[0], 0, None, None, None, w, CFG)
    y_s, ret_s, ssm_s, conv_s = _trunk(x_sample, p_sample[0], PAST_LEN, state_ret[0], state_ssm[0],
                                       state_conv[0], w, CFG)
    e = lambda a: a[None]
    return (y_p, y_s, e(ret_p), e(ssm_p), e(conv_p), e(ret_s), e(ssm_s), e(conv_s))
```

```python
import functools

import numpy as np
import jax
import jax.numpy as jnp
from jax import lax
from jax.experimental import pallas as pl
from jax.experimental.pallas import tpu as pltpu

F32 = jnp.float32
BF16 = jnp.bfloat16

D_MODEL = 2048
PAST_LEN = 4096
EPS = 1e-6
RET_HEADS = 8
RET_DK = 256
RET_DV = 512
RET_QK = RET_HEADS * RET_DK
RET_V = RET_HEADS * RET_DV
ROPE_THETA = 10000.0
SSM_INNER = 2 * D_MODEL
SSM_HEADDIM = 64
SSM_HEADS = SSM_INNER // SSM_HEADDIM
SSM_GROUPS = 8
SSM_HPG = SSM_HEADS // SSM_GROUPS
SSM_GW = SSM_HPG * SSM_HEADDIM
SSM_STATE = 128
SSM_CONV = 4
SSM_CONV_DIM = SSM_INNER + 2 * SSM_GROUPS * SSM_STATE
PLE_DIM = 256

COL_Q = 0
COL_K = COL_Q + RET_QK
COL_V = COL_K + RET_QK
COL_RG = COL_V + RET_V
COL_Z = COL_RG + RET_V
COL_XBC = COL_Z + SSM_INNER
N_BIG = COL_XBC + SSM_CONV_DIM
COL_DT = N_BIG
COL_GR = COL_DT + SSM_HEADS
COL_GS = COL_GR + D_MODEL

SUBLANES = 8
HALF_LANES = 64
VMEM_LIMIT = 56 * 1024 * 1024
ROW_CHUNK = 256


def _cparams(sem):
    return pltpu.CompilerParams(dimension_semantics=sem, vmem_limit_bytes=VMEM_LIMIT)


def _rms_bf16(x, g):
    ms = jnp.mean(x * x, axis=-1, keepdims=True)
    return (x * lax.rsqrt(ms + EPS) * g).astype(BF16)


def _silu(x):
    return x * jax.nn.sigmoid(x)


def _dot(a, b):
    return jnp.dot(a, b, preferred_element_type=F32)


def _dot_nt(a, b):
    return lax.dot_general(a, b, (((1,), (1,)), ((), ())), preferred_element_type=F32)


def _dot_tn(a, b):
    return lax.dot_general(a, b, (((0,), (0,)), ((), ())), preferred_element_type=F32)


def _ffn_kernel(x_ref, g_ref, gn_ref, wg_ref, wu_ref, wd_ref, o_ref, hn_ref, h_scr, *, nj):
    j = pl.program_id(1)

    @pl.when(j == 0)
    def _():
        h_scr[...] = _rms_bf16(x_ref[...], g_ref[...])
        o_ref[...] = jnp.zeros_like(o_ref)

    h = h_scr[...]
    gate = _dot(h, wg_ref[...])
    up = _dot(h, wu_ref[...])
    act = (_silu(gate) * up).astype(BF16)
    o_ref[...] += _dot(act, wd_ref[...])

    @pl.when(j == nj - 1)
    def _():
        xn = x_ref[...] + 0.5 * o_ref[...]
        o_ref[...] = xn
        hn_ref[...] = _rms_bf16(xn, gn_ref[...])


def _ffn(x, g, g_next, w_gu, w_down, *, tm, tf):
    n, d = x.shape
    f = w_down.shape[0]
    nj = f // tf
    return pl.pallas_call(
        functools.partial(_ffn_kernel, nj=nj),
        out_shape=(jax.ShapeDtypeStruct((n, d), F32), jax.ShapeDtypeStruct((n, d), BF16)),
        grid=(n // tm, nj),
        in_specs=[
            pl.BlockSpec((tm, d), lambda i, j: (i, 0)),
            pl.BlockSpec((1, d), lambda i, j: (0, 0)),
            pl.BlockSpec((1, d), lambda i, j: (0, 0)),
            pl.BlockSpec((d, tf), lambda i, j: (0, j)),
            pl.BlockSpec((d, tf), lambda i, j: (0, j + nj)),
            pl.BlockSpec((tf, d), lambda i, j: (j, 0)),
        ],
        out_specs=(pl.BlockSpec((tm, d), lambda i, j: (i, 0)),
                   pl.BlockSpec((tm, d), lambda i, j: (i, 0))),
        scratch_shapes=[pltpu.VMEM((tm, d), BF16)],
        compiler_params=_cparams(("parallel", "arbitrary")),
        name="ffn",
    )(x, g, g_next, w_gu, w_gu, w_down)


def _inproj_kernel(*refs, tn, nq, nrope, jx, rc, tiles_per_seq, fuse_conv):
    if fuse_conv:
        h_ref, w_ref, cos_ref, sin_ref, cw_ref, cb_ref, o_ref, tail_ref, cv_scr, hist_scr = refs
    else:
        h_ref, w_ref, cos_ref, sin_ref, o_ref = refs
    i = pl.program_id(0)
    j = pl.program_id(1)
    tm = h_ref.shape[0]
    nr = tm // rc

    @pl.when((j >= nrope) & (j < jx))
    def _():
        for r in range(nr):
            rows = pl.ds(r * rc, rc)
            o_ref[rows, :] = _dot(h_ref[rows, :], w_ref[...]).astype(BF16)

    @pl.when(j < nrope)
    def _():
        scale = jnp.where(j >= nq, RET_DK ** -0.5, 1.0).astype(F32)
        half = RET_DK // 2
        for r in range(nr):
            rows = pl.ds(r * rc, rc)
            acc = _dot(h_ref[rows, :], w_ref[...])
            c = cos_ref[rows, :] * scale
            s = sin_ref[rows, :] * scale
            for hh in range(tn // RET_DK):
                lo = hh * RET_DK
                x1 = acc[:, lo:lo + half]
                x2 = acc[:, lo + half:lo + RET_DK]
                o_ref[rows, lo:lo + half] = (x1 * c - x2 * s).astype(BF16)
                o_ref[rows, lo + half:lo + RET_DK] = (x2 * c + x1 * s).astype(BF16)

    if fuse_conv:
        @pl.when(j >= jx)
        def _():
            jj = j - jx
            seq_start = (i % tiles_per_seq) == 0
            cv_scr[0:SUBLANES, :] = jnp.where(seq_start, 0.0, hist_scr[jj])
            cw = cw_ref[...]
            bias = cb_ref[...]
            for r in range(nr):
                rows = pl.ds(r * rc, rc)
                cv_scr[pl.ds(SUBLANES + r * rc, rc), :] = _dot(h_ref[rows, :], w_ref[...])
                y = bias
                for s in range(SSM_CONV):
                    y = y + cv_scr[pl.ds(SUBLANES + r * rc - s, rc), :] * cw[SSM_CONV - 1 - s:SSM_CONV - s, :]
                o_ref[rows, :] = _silu(y).astype(BF16)
            tail = cv_scr[tm:tm + SUBLANES, :]
            hist_scr[jj] = tail
            tail_ref[0] = tail


def _inproj(h, w_in, cos, sin, conv_w, conv_b, *, tm, tn, seqlen):
    n, d = h.shape
    nb = N_BIG
    nrow = cos.shape[0] // tm
    fuse_conv = seqlen % tm == 0
    nj = nb // tn
    jx = COL_XBC // tn if fuse_conv else nj
    nx = nj - jx
    kern = functools.partial(_inproj_kernel, tn=tn, nq=RET_QK // tn, nrope=2 * RET_QK // tn, jx=jx,
                             rc=min(ROW_CHUNK, tm), tiles_per_seq=max(seqlen // tm, 1), fuse_conv=fuse_conv)
    in_specs = [
        pl.BlockSpec((tm, d), lambda i, j: (i, 0)),
        pl.BlockSpec((d, tn), lambda i, j: (0, j)),
        pl.BlockSpec((tm, RET_DK // 2), lambda i, j: (i % nrow, 0)),
        pl.BlockSpec((tm, RET_DK // 2), lambda i, j: (i % nrow, 0)),
    ]
    big_spec = pl.BlockSpec((tm, tn), lambda i, j: (i, j))
    big_shape = jax.ShapeDtypeStruct((n, nb), BF16)
    if not fuse_conv:
        big = pl.pallas_call(
            kern, out_shape=big_shape, grid=(n // tm, nj), in_specs=in_specs, out_specs=big_spec,
            compiler_params=_cparams(("parallel", "arbitrary")), name="inproj",
        )(h, w_in, cos, sin)
        return big, None
    xcol = lambda i, j: (0, jnp.maximum(j - jx, 0))
    in_specs += [pl.BlockSpec((SSM_CONV, tn), xcol), pl.BlockSpec((1, tn), xcol)]
    return pl.pallas_call(
        kern,
        out_shape=(big_shape, jax.ShapeDtypeStruct((n // tm, SUBLANES, SSM_CONV_DIM), F32)),
        grid=(n // tm, nj),
        in_specs=in_specs,
        out_specs=(big_spec, pl.BlockSpec((1, SUBLANES, tn), lambda i, j: (i, 0, jnp.maximum(j - jx, 0)))),
        scratch_shapes=[pltpu.VMEM((SUBLANES + tm, tn), F32), pltpu.VMEM((nx, SUBLANES, tn), F32)],
        compiler_params=_cparams(("arbitrary", "arbitrary")),
        name="inproj",
    )(h, w_in, cos, sin, conv_w, conv_b.reshape(1, SSM_CONV_DIM))


def _softplus(x):
    return jnp.maximum(x, 0.0) + jnp.log1p(jnp.exp(-jnp.abs(x)))


def _dt_kernel(h_ref, w_ref, wt_ref, br_ref, bc_ref, ar_ref, ac_ref,
               dtc_ref, acc_ref, dtr_ref, acr_ref, *, q, valid):
    h = h_ref[...]
    tm = h.shape[0]
    raw_c = _dot(h, w_ref[...])
    raw_r = _dot_nt(wt_ref[...], h)
    dt_c = _softplus(raw_c + br_ref[...])
    dt_r = _softplus(raw_r + bc_ref[...])
    if valid < q:
        row = lax.broadcasted_iota(jnp.int32, dt_c.shape, 0) % q
        col = lax.broadcasted_iota(jnp.int32, dt_r.shape, 1) % q
        dt_c = jnp.where(row < valid, dt_c, 0.0)
        dt_r = jnp.where(col < valid, dt_r, 0.0)
    dta_c = dt_c * (-jnp.exp(ar_ref[...]))
    dta_r = dt_r * (-jnp.exp(ac_ref[...]))
    i = lax.broadcasted_iota(jnp.int32, (tm, tm), 0)
    j = lax.broadcasted_iota(jnp.int32, (tm, tm), 1)
    same = (i // q) == (j // q)
    tri = jnp.where(same & (j <= i), 1.0, 0.0).astype(F32)
    trit = jnp.where(same & (i <= j), 1.0, 0.0).astype(F32)
    ac_c = jnp.dot(tri, dta_c, preferred_element_type=F32, precision=lax.Precision.HIGHEST)
    ac_r = jnp.dot(dta_r, trit, preferred_element_type=F32, precision=lax.Precision.HIGHEST)
    for g in range(SSM_GROUPS):
        lo = g * SSM_HPG
        dtc_ref[g] = dt_c[:, lo:lo + SSM_HPG]
        acc_ref[g] = ac_c[:, lo:lo + SSM_HPG]
        for ci in range(tm // q):
            dtr_ref[g, ci] = dt_r[lo:lo + SSM_HPG, ci * q:(ci + 1) * q]
            acr_ref[g, ci] = ac_r[lo:lo + SSM_HPG, ci * q:(ci + 1) * q]


def _dt_proj(h, w_dt, w_dt_t, dt_bias, a_log, *, tm, q, valid):
    n, d = h.shape
    hh = SSM_HEADS
    col = jax.ShapeDtypeStruct((SSM_GROUPS, n, SSM_HPG), F32)
    row = jax.ShapeDtypeStruct((SSM_GROUPS, n // q, SSM_HPG, q), F32)
    return pl.pallas_call(
        functools.partial(_dt_kernel, q=q, valid=valid),
        out_shape=(col, col, row, row),
        grid=(n // tm,),
        in_specs=[
            pl.BlockSpec((tm, d), lambda i: (i, 0)),
            pl.BlockSpec((d, hh), lambda i: (0, 0)),
            pl.BlockSpec((hh, d), lambda i: (0, 0)),
            pl.BlockSpec((1, hh), lambda i: (0, 0)),
            pl.BlockSpec((hh, 1), lambda i: (0, 0)),
            pl.BlockSpec((1, hh), lambda i: (0, 0)),
            pl.BlockSpec((hh, 1), lambda i: (0, 0)),
        ],
        out_specs=(
            pl.BlockSpec((SSM_GROUPS, tm, SSM_HPG), lambda i: (0, i, 0)),
            pl.BlockSpec((SSM_GROUPS, tm, SSM_HPG), lambda i: (0, i, 0)),
            pl.BlockSpec((SSM_GROUPS, tm // q, SSM_HPG, q), lambda i: (0, i, 0, 0)),
            pl.BlockSpec((SSM_GROUPS, tm // q, SSM_HPG, q), lambda i: (0, i, 0, 0)),
        ),
        compiler_params=_cparams(("parallel",)),
        name="dt_proj",
    )(h, w_dt, w_dt_t, dt_bias.reshape(1, hh), dt_bias.reshape(hh, 1),
      a_log.reshape(1, hh), a_log.reshape(hh, 1))


def _ret_kernel(*refs, c, nchunk, sb, has_s0):
    if has_s0:
        (q_ref, k_ref, v_ref, rg_ref, dm_ref, qd_ref, kd_ref, cd_ref, gg_ref, gb_ref, s0_ref,
         o_ref, s_ref) = refs
    else:
        (q_ref, k_ref, v_ref, rg_ref, dm_ref, qd_ref, kd_ref, cd_ref, gg_ref, gb_ref,
         o_ref, s_ref) = refs
    t = pl.program_id(2)

    @pl.when(t == 0)
    def _():
        if has_s0:
            s_ref[...] = s0_ref[...]
        else:
            s_ref[...] = jnp.zeros_like(s_ref)

    dm = dm_ref[0]
    qd = qd_ref[0]
    kd = kd_ref[0]
    cd = cd_ref[0]
    gg = gg_ref[...]
    gb = gb_ref[...]
    for si, ci in ((si, ci) for si in range(sb) for ci in range(nchunk)):
        sl = pl.ds((si * nchunk + ci) * c, c)
        q = q_ref[sl, :]
        k = k_ref[sl, :]
        v = v_ref[sl, :]
        s = s_ref[si, 0]
        att = _dot_nt(q, k) * dm
        o = _dot(att.astype(BF16), v) + _dot(q, s.astype(BF16)) * qd
        kdk = (k.astype(F32) * kd).astype(BF16)
        s_ref[si, 0] = s * cd + _dot_tn(kdk, v)
        mu = jnp.mean(o, axis=-1, keepdims=True)
        dev = o - mu
        var = jnp.mean(dev * dev, axis=-1, keepdims=True)
        on = dev * lax.rsqrt(var + EPS)
        rg = rg_ref[sl, :].astype(F32)
        o_ref[sl, :] = (_silu(rg) * (on * gg + gb)).astype(BF16)


def _ret_tables(c_true, c):
    hs = np.arange(RET_HEADS, dtype=np.float64)
    log_g = np.log1p(-np.exp2(-5.0 - hs))
    idx = np.arange(c, dtype=np.float64)
    diff = idx[:, None] - idx[None, :]
    dmat = np.where(diff[None] >= 0, np.exp(np.maximum(diff, 0.0)[None] * log_g[:, None, None]), 0.0)
    q_dec = np.exp((idx[None, :] + 1.0) * log_g[:, None])
    k_dec = np.where(idx[None, :] < c_true,
                     np.exp(np.maximum(c_true - 1.0 - idx[None, :], 0.0) * log_g[:, None]), 0.0)
    c_dec = np.exp(c_true * log_g)
    qd = np.broadcast_to(q_dec[:, :, None], (RET_HEADS, c, RET_DV))
    kd = np.broadcast_to(k_dec[:, :, None], (RET_HEADS, c, RET_DK))
    cd = np.broadcast_to(c_dec[:, None, None], (RET_HEADS, 1, RET_DV))
    f = lambda a: jnp.asarray(np.ascontiguousarray(a), dtype=F32)
    return f(dmat), f(qd), f(kd), f(cd)


def _retention(big, gn_g, gn_b, s0, *, nseq, seqlen, c_true, c, tb, sb):
    n = big.shape[0]
    nt = seqlen // tb
    assert sb == 1 or nt == 1
    nchunk = tb // c
    rb = sb * tb
    dm, qd, kd, cd = _ret_tables(c_true, c)
    has_s0 = s0 is not None
    kq, kk = COL_Q // RET_DK, COL_K // RET_DK
    kv, kr = COL_V // RET_DV, COL_RG // RET_DV
    row = lambda b, h, t: b * nt + t
    in_specs = [
        pl.BlockSpec((rb, RET_DK), lambda b, h, t: (row(b, h, t), kq + h)),
        pl.BlockSpec((rb, RET_DK), lambda b, h, t: (row(b, h, t), kk + h)),
        pl.BlockSpec((rb, RET_DV), lambda b, h, t: (row(b, h, t), kv + h)),
        pl.BlockSpec((rb, RET_DV), lambda b, h, t: (row(b, h, t), kr + h)),
        pl.BlockSpec((1, c, c), lambda b, h, t: (h, 0, 0)),
        pl.BlockSpec((1, c, RET_DV), lambda b, h, t: (h, 0, 0)),
        pl.BlockSpec((1, c, RET_DK), lambda b, h, t: (h, 0, 0)),
        pl.BlockSpec((1, 1, RET_DV), lambda b, h, t: (h, 0, 0)),
        pl.BlockSpec((1, RET_DV), lambda b, h, t: (0, h)),
        pl.BlockSpec((1, RET_DV), lambda b, h, t: (0, h)),
    ]
    args = [big, big, big, big, dm, qd, kd, cd, gn_g.reshape(1, RET_V), gn_b.reshape(1, RET_V)]
    st_spec = pl.BlockSpec((sb, 1, RET_DK, RET_DV), lambda b, h, t: (b, h, 0, 0))
    if has_s0:
        in_specs.append(st_spec)
        args.append(s0)
    return pl.pallas_call(
        functools.partial(_ret_kernel, c=c, nchunk=nchunk, sb=sb, has_s0=has_s0),
        out_shape=(jax.ShapeDtypeStruct((n, RET_V), BF16),
                   jax.ShapeDtypeStruct((nseq, RET_HEADS, RET_DK, RET_DV), F32)),
        grid=(nseq // sb, RET_HEADS, nt),
        in_specs=in_specs,
        out_specs=(pl.BlockSpec((rb, RET_DV), lambda b, h, t: (row(b, h, t), h)), st_spec),
        compiler_params=_cparams(("parallel", "parallel", "arbitrary")),
        name="retention",
    )(*args)


def _conv_silu(raw_scr, cw, bias, tb):
    taps = SSM_CONV
    acc = bias
    for s in range(taps):
        acc = acc + raw_scr[pl.ds(SUBLANES - s, tb), :] * cw[taps - 1 - s:taps - s, :]
    return _silu(acc)


def _lane_expand(v, width):
    rows = v.shape[0]
    lane = lax.broadcasted_iota(jnp.int32, (rows, 2 * width), 1)
    parts = []
    for m in range(v.shape[1] // 2):
        a = jnp.broadcast_to(v[:, 2 * m:2 * m + 1], (rows, 2 * width))
        b = jnp.broadcast_to(v[:, 2 * m + 1:2 * m + 2], (rows, 2 * width))
        parts.append(jnp.where(lane < width, a, b))
    return jnp.concatenate(parts, axis=1)


def _ssd_kernel(*refs, q, nchunk, sb, has_s0, conv_done):
    if conv_done:
        (xs_ref, b_ref, c_ref, z_ref, dtc_ref, acc_ref, dtr_ref, acr_ref, dsk_ref, ng_ref,
         y_ref, so_ref, st_scr) = refs
        assert not has_s0
    elif has_s0:
        (xs_ref, b_ref, c_ref, z_ref, dtc_ref, acc_ref, dtr_ref, acr_ref,
         cwx_ref, cwb_ref, cwc_ref, cbx_ref, cbb_ref, cbc_ref, dsk_ref, ng_ref,
         s0_ref, hx0_ref, hb0_ref, hc0_ref,
         y_ref, so_ref, hx, hb, hc, xc_scr, bc_scr, cc_scr, st_scr) = refs
        hist0 = (hx0_ref, hb0_ref, hc0_ref)
    else:
        (xs_ref, b_ref, c_ref, z_ref, dtc_ref, acc_ref, dtr_ref, acr_ref,
         cwx_ref, cwb_ref, cwc_ref, cbx_ref, cbb_ref, cbc_ref, dsk_ref, ng_ref,
         y_ref, so_ref, hx, hb, hc, xc_scr, bc_scr, cc_scr, st_scr) = refs
        hist0 = (None, None, None)
    t = pl.program_id(2)
    tb = xs_ref.shape[0] // sb

    @pl.when(t == 0)
    def _():
        for si in range(sb):
            if has_s0:
                st_scr[si] = s0_ref[si, 0].T
            else:
                st_scr[si] = jnp.zeros(st_scr.shape[1:], F32)

    for si in range(0 if conv_done else sb):
        rows = pl.ds(si * tb, tb)
        for raw_ref, raw_scr, h0_ref, cw_ref, cb_ref, dst in (
                (xs_ref, hx, hist0[0], cwx_ref, cbx_ref, xc_scr),
                (b_ref, hb, hist0[1], cwb_ref, cbb_ref, bc_scr),
                (c_ref, hc, hist0[2], cwc_ref, cbc_ref, cc_scr)):
            @pl.when(t == 0)
            def _():
                if has_s0:
                    raw_scr[0:SUBLANES, :] = h0_ref[si]
                else:
                    raw_scr[0:SUBLANES, :] = jnp.zeros((SUBLANES, raw_scr.shape[1]), F32)
            raw_scr[SUBLANES:SUBLANES + tb, :] = raw_ref[rows, :].astype(F32)
            dst[rows, :] = _conv_silu(raw_scr, cw_ref[...], cb_ref[...], tb)
            raw_scr[0:SUBLANES, :] = raw_scr[tb:tb + SUBLANES, :]

    ii = lax.broadcasted_iota(jnp.int32, (q, q), 0)
    jj = lax.broadcasted_iota(jnp.int32, (q, q), 1)
    causal = ii >= jj
    lane = lax.broadcasted_iota(jnp.int32, (q, 2 * HALF_LANES), 1)
    dsk = dsk_ref[...]
    ng = ng_ref[...]
    def chunk(si, ci):
        cidx = si * nchunk + ci
        sl = pl.ds(pl.multiple_of(cidx * q, q), q)
        if conv_done:
            x = xs_ref[sl, :].astype(F32)
            bq = b_ref[sl, :]
            cq = c_ref[sl, :]
        else:
            x = xc_scr[sl, :]
            bq = bc_scr[sl, :].astype(BF16)
            cq = cc_scr[sl, :].astype(BF16)
        dtc = dtc_ref[0, sl, :]
        acc = acc_ref[0, sl, :]
        dtr = dtr_ref[0, cidx]
        acr = acr_ref[0, cidx]
        state = st_scr[si]

        cb = jnp.where(causal, _dot_nt(cq, bq), 0.0)
        pairs = []
        for m in range(SSM_HPG // 2):
            ws = []
            for r in (2 * m, 2 * m + 1):
                seg = acc[:, r:r + 1] - acr[r:r + 1, :]
                ws.append((jnp.exp(jnp.minimum(seg, 0.0)) * (cb * dtr[r:r + 1, :])).astype(BF16))
            xp = x[:, m * 2 * HALF_LANES:(m + 1) * 2 * HALF_LANES]
            x_lo = jnp.where(lane < HALF_LANES, xp, 0.0).astype(BF16)
            x_hi = jnp.where(lane >= HALF_LANES, xp, 0.0).astype(BF16)
            pairs.append(_dot(jnp.concatenate(ws, axis=1), jnp.concatenate([x_lo, x_hi], axis=0)))
        y = jnp.concatenate(pairs, axis=1)

        ea = _lane_expand(jnp.exp(acc), SSM_HEADDIM)
        y = y + _dot(cq, state.astype(BF16)) * ea
        wend = jnp.exp(acc[q - 1:q, :] - acc) * dtc
        xw = (x * _lane_expand(wend, SSM_HEADDIM)).astype(BF16)
        st_scr[si] = state * ea[q - 1:q, :] + _dot_tn(bq, xw)

        y = y + dsk * x
        y = y * _silu(z_ref[sl, :].astype(F32))
        ms = jnp.mean(y * y, axis=-1, keepdims=True)
        y_ref[sl, :] = (y * lax.rsqrt(ms + EPS) * ng).astype(BF16)

    for si in range(sb):
        if nchunk == 1:
            chunk(si, 0)
        else:
            lax.fori_loop(0, nchunk, lambda ci, carry, si=si: (chunk(si, ci), carry)[1], 0)

    @pl.when(t == pl.num_programs(2) - 1)
    def _():
        for si in range(sb):
            so_ref[si, 0] = st_scr[si].T


def _ssd(big, dts, conv_w, conv_b, d_skip_x, norm_g, s0, hist0, *, nseq, seqlen, q, tb, sb, conv_done):
    n = big.shape[0]
    nt = seqlen // tb
    assert sb == 1 or nt == 1
    nchunk = tb // q
    rb = sb * tb
    has_s0 = s0 is not None
    dtc, acc, dtr, acr = dts
    gw, ns = SSM_GW, SSM_STATE
    kz, kx = COL_Z // gw, COL_XBC // gw
    kb = (COL_XBC + SSM_INNER) // ns
    kc = kb + SSM_GROUPS
    cb_b = SSM_INNER // ns
    cb_c = cb_b + SSM_GROUPS
    row = lambda b, g, t: b * nt + t
    in_specs = [
        pl.BlockSpec((rb, gw), lambda b, g, t: (row(b, g, t), kx + g)),
        pl.BlockSpec((rb, ns), lambda b, g, t: (row(b, g, t), kb + g)),
        pl.BlockSpec((rb, ns), lambda b, g, t: (row(b, g, t), kc + g)),
        pl.BlockSpec((rb, gw), lambda b, g, t: (row(b, g, t), kz + g)),
        pl.BlockSpec((1, rb, SSM_HPG), lambda b, g, t: (g, row(b, g, t), 0)),
        pl.BlockSpec((1, rb, SSM_HPG), lambda b, g, t: (g, row(b, g, t), 0)),
        pl.BlockSpec((1, rb // q, SSM_HPG, q), lambda b, g, t: (g, row(b, g, t), 0, 0)),
        pl.BlockSpec((1, rb // q, SSM_HPG, q), lambda b, g, t: (g, row(b, g, t), 0, 0)),
    ]
    args = [big, big, big, big, dtc, acc, dtr, acr]
    if not conv_done:
        in_specs += [
            pl.BlockSpec((SSM_CONV, gw), lambda b, g, t: (0, g)),
            pl.BlockSpec((SSM_CONV, ns), lambda b, g, t: (0, cb_b + g)),
            pl.BlockSpec((SSM_CONV, ns), lambda b, g, t: (0, cb_c + g)),
            pl.BlockSpec((1, gw), lambda b, g, t: (0, g)),
            pl.BlockSpec((1, ns), lambda b, g, t: (0, cb_b + g)),
            pl.BlockSpec((1, ns), lambda b, g, t: (0, cb_c + g)),
        ]
        cbias = conv_b.reshape(1, SSM_CONV_DIM)
        args += [conv_w, conv_w, conv_w, cbias, cbias, cbias]
    in_specs += [pl.BlockSpec((1, gw), lambda b, g, t: (0, g)), pl.BlockSpec((1, gw), lambda b, g, t: (0, g))]
    args += [d_skip_x, norm_g.reshape(1, SSM_INNER)]
    st_spec = pl.BlockSpec((sb, 1, gw, ns), lambda b, g, t: (b, g, 0, 0))
    if has_s0:
        in_specs += [
            st_spec,
            pl.BlockSpec((sb, SUBLANES, gw), lambda b, g, t: (b, 0, g)),
            pl.BlockSpec((sb, SUBLANES, ns), lambda b, g, t: (b, 0, cb_b + g)),
            pl.BlockSpec((sb, SUBLANES, ns), lambda b, g, t: (b, 0, cb_c + g)),
        ]
        args += [s0, hist0, hist0, hist0]
    return pl.pallas_call(
        functools.partial(_ssd_kernel, q=q, nchunk=nchunk, sb=sb, has_s0=has_s0, conv_done=conv_done),
        out_shape=(jax.ShapeDtypeStruct((n, SSM_INNER), BF16),
                   jax.ShapeDtypeStruct((nseq, SSM_GROUPS, gw, ns), F32)),
        grid=(nseq // sb, SSM_GROUPS, nt),
        in_specs=in_specs,
        out_specs=(pl.BlockSpec((rb, gw), lambda b, g, t: (row(b, g, t), g)), st_spec),
        scratch_shapes=([] if conv_done else
                        [pltpu.VMEM((SUBLANES + tb, gw), F32), pltpu.VMEM((SUBLANES + tb, ns), F32),
                         pltpu.VMEM((SUBLANES + tb, ns), F32), pltpu.VMEM((rb, gw), F32),
                         pltpu.VMEM((rb, ns), F32), pltpu.VMEM((rb, ns), F32)])
        + [pltpu.VMEM((sb, ns, gw), F32)],
        compiler_params=_cparams(("parallel", "parallel", "arbitrary")),
        name="ssd",
    )(*args)


def _merge_kernel(h_ref, ret_ref, ssm_ref, wgr_ref, wgs_ref, wr_ref, ws_ref, o_ref):
    h = h_ref[...]
    gr = jax.nn.sigmoid(_dot(h, wgr_ref[...]))
    gs = jax.nn.sigmoid(_dot(h, wgs_ref[...]))
    a = _dot(ret_ref[...], wr_ref[...])
    b = _dot(ssm_ref[...], ws_ref[...])
    o_ref[...] = (gr * a + gs * b).astype(BF16)


def _merge(h, ret, ssm, w_gr, w_gs, w_r, w_s, *, tm, tn):
    n, d = h.shape
    kr, ks = ret.shape[1], ssm.shape[1]
    return pl.pallas_call(
        _merge_kernel,
        out_shape=jax.ShapeDtypeStruct((n, d), BF16),
        grid=(n // tm, d // tn),
        in_specs=[
            pl.BlockSpec((tm, d), lambda i, j: (i, 0)),
            pl.BlockSpec((tm, kr), lambda i, j: (i, 0)),
            pl.BlockSpec((tm, ks), lambda i, j: (i, 0)),
            pl.BlockSpec((d, tn), lambda i, j: (0, j)),
            pl.BlockSpec((d, tn), lambda i, j: (0, j)),
            pl.BlockSpec((kr, tn), lambda i, j: (0, j)),
            pl.BlockSpec((ks, tn), lambda i, j: (0, j)),
        ],
        out_specs=pl.BlockSpec((tm, tn), lambda i, j: (i, j)),
        compiler_params=_cparams(("parallel", "arbitrary")),
        name="merge",
    )(h, ret, ssm, w_gr, w_gs, w_r, w_s)


def _outproj_kernel(m_ref, w_ref, x_ref, o_ref, *, rc):
    for r in range(m_ref.shape[0] // rc):
        rows = pl.ds(r * rc, rc)
        o_ref[rows, :] = x_ref[rows, :] + _dot(m_ref[rows, :], w_ref[...])


def _outproj(m, w, x, *, tm, tn):
    n, d = x.shape
    k = m.shape[1]
    return pl.pallas_call(
        functools.partial(_outproj_kernel, rc=min(ROW_CHUNK, tm)),
        out_shape=jax.ShapeDtypeStruct((n, d), F32),
        grid=(n // tm, d // tn),
        in_specs=[
            pl.BlockSpec((tm, k), lambda i, j: (i, 0)),
            pl.BlockSpec((k, tn), lambda i, j: (0, j)),
            pl.BlockSpec((tm, tn), lambda i, j: (i, j)),
        ],
        out_specs=pl.BlockSpec((tm, tn), lambda i, j: (i, j)),
        compiler_params=_cparams(("parallel", "arbitrary")),
        name="outproj",
    )(m, w, x)


def _ple_kernel(x_ref, h_ref, p_ref, wp_ref, wg_ref, gf_ref, o_ref):
    pe = _dot(p_ref[...].astype(BF16), wp_ref[...])
    gt = jax.nn.sigmoid(_dot(h_ref[...], wg_ref[...]))
    x = x_ref[...] + pe * gt
    ms = jnp.mean(x * x, axis=-1, keepdims=True)
    o_ref[...] = x * lax.rsqrt(ms + EPS) * gf_ref[...]


def _ple(x, h, p, w_ple, w_gate, g_final, *, tm):
    n, d = x.shape
    pd = p.shape[1]
    return pl.pallas_call(
        _ple_kernel,
        out_shape=jax.ShapeDtypeStruct((n, d), F32),
        grid=(n // tm,),
        in_specs=[
            pl.BlockSpec((tm, d), lambda i: (i, 0)),
            pl.BlockSpec((tm, d), lambda i: (i, 0)),
            pl.BlockSpec((tm, pd), lambda i: (i, 0)),
            pl.BlockSpec((pd, d), lambda i: (0, 0)),
            pl.BlockSpec((d, d), lambda i: (0, 0)),
            pl.BlockSpec((1, d), lambda i: (0, 0)),
        ],
        out_specs=pl.BlockSpec((tm, d), lambda i: (i, 0)),
        compiler_params=_cparams(("parallel",)),
        name="ple",
    )(x, h, p, w_ple, w_gate, g_final)


def _rope_tables(pos0, seqlen, rows):
    half = RET_DK // 2
    inv = ROPE_THETA ** (-jnp.arange(half, dtype=F32) / half)
    pos = (pos0 + jnp.arange(seqlen, dtype=jnp.int32)).astype(F32)
    ang = pos[:, None] * inv[None, :]
    reps = max(rows // seqlen, 1)
    return jnp.tile(jnp.cos(ang), (reps, 1)), jnp.tile(jnp.sin(ang), (reps, 1))


def _pick(n, pref):
    t = pref
    while n % t:
        t //= 2
    return t


def _trunk(x, p, pos0, s_ret, s_ssm, s_conv, w, cfg):
    nseq, seqlen, d = x.shape
    n = nseq * seqlen
    x = x.reshape(n, d)
    p = p.reshape(n, p.shape[-1])
    tm = _pick(n, cfg["tm"])

    x1, h = _ffn(x, w["g_ffn1"], w["g_mix"], w["w1_gu"], w["w1_down"], tm=_pick(n, cfg["tm_ffn"]), tf=cfg["tf"])

    cos, sin = _rope_tables(pos0, seqlen, tm)
    big, tail = _inproj(h, w["w_in"], cos, sin, w["conv_w"], w["conv_b"], tm=tm, tn=cfg["tn_in"], seqlen=seqlen)

    q = cfg["q_ssd"]
    lpad = -(-seqlen // q) * q
    if lpad != seqlen:
        pad = lambda a: jnp.pad(a.reshape(nseq, seqlen, -1), ((0, 0), (0, lpad - seqlen), (0, 0))).reshape(nseq * lpad, -1)
        big_m, h_m = pad(big), pad(h)
    else:
        big_m, h_m = big, h
    npad = nseq * lpad
    valid = min(seqlen, q)

    dts = _dt_proj(h_m, w["w_dt"], w["w_dt_t"], w["dt_bias"], w["a_log"], tm=_pick(npad, 512), q=q, valid=valid)

    c = cfg["c_ret"] if lpad % cfg["c_ret"] == 0 else q
    tb_ret, tb_ssd = _pick(lpad, cfg["tb_ret"]), _pick(lpad, cfg["tb_ssd"])
    sb = _pick(nseq, cfg["seq_batch"]) if max(tb_ret, tb_ssd) == lpad else 1
    ret, s_ret_new = _retention(big_m, w["ret_gn_g"], w["ret_gn_b"], s_ret, nseq=nseq, seqlen=lpad,
                                c_true=min(seqlen, c), c=c, tb=tb_ret, sb=sb)

    if s_ssm is not None:
        s0 = s_ssm.reshape(nseq, SSM_GROUPS, SSM_GW, SSM_STATE)
        hist0 = jnp.pad(s_conv, ((0, 0), (SUBLANES - (SSM_CONV - 1), 0), (0, 0)))
    else:
        s0, hist0 = None, None
    ssm, s_ssm_new = _ssd(big_m, dts, w["conv_w"], w["conv_b"], w["d_skip_x"], w["ssm_norm_g"], s0, hist0,
                          nseq=nseq, seqlen=lpad, q=q, tb=tb_ssd, sb=sb, conv_done=tail is not None)
    s_ssm_new = s_ssm_new.reshape(nseq, SSM_HEADS, SSM_HEADDIM, SSM_STATE)

    if lpad != seqlen:
        unpad = lambda a: a.reshape(nseq, lpad, -1)[:, :seqlen].reshape(n, -1)
        ret, ssm = unpad(ret), unpad(ssm)

    if tail is not None:
        xbc_tail = tail.reshape(nseq, seqlen // tm, SUBLANES, SSM_CONV_DIM)[:, -1, SUBLANES - (SSM_CONV - 1):]
    else:
        keep = min(seqlen, SSM_CONV - 1)
        xbc_tail = big.reshape(nseq, seqlen, N_BIG)[:, seqlen - keep:, COL_XBC:].astype(F32)
        if keep < SSM_CONV - 1:
            prev = jnp.zeros((nseq, SSM_CONV - 1, SSM_CONV_DIM), F32) if s_conv is None else s_conv.astype(F32)
            xbc_tail = jnp.concatenate([prev, xbc_tail], axis=1)[:, -(SSM_CONV - 1):]

    merged = _merge(h, ret, ssm, w["w_gr"], w["w_gs"], w["w_br_ret"], w["w_br_ssm"], tm=_pick(n, cfg["tm_mg"]), tn=cfg["tn_mg"])
    x2 = _outproj(merged, w["w_out"], x1, tm=tm, tn=cfg["tn_out"])
    x3, h3 = _ffn(x2, w["g_ffn2"], w["g_ple"], w["w2_gu"], w["w2_down"], tm=_pick(n, cfg["tm_ffn"]), tf=cfg["tf"])
    y = _ple(x3, h3, p, w["w_ple"], w["w_ple_gate"], w["g_final"], tm=_pick(n, cfg["tm_ple"]))
    return y.reshape(nseq, seqlen, d), s_ret_new, s_ssm_new, xbc_tail


CFG = dict(tm=1024, tm_ffn=512, tf=512, tn_in=1024, q_ssd=128, c_ret=256, tb_ret=1024, tb_ssd=1024, seq_batch=4,
           tm_mg=512, tn_mg=512, tn_out=1024, tm_ple=256)


def _prep_weights(g_ffn1, w1_gu, w1_down, g_mix, w_in, ret_gn_g, ret_gn_b, conv_w, conv_b, dt_bias, a_log,
                  d_skip, ssm_norm_g, w_br_ret, w_br_ssm, w_out, g_ffn2, w2_gu, w2_down, g_ple, w_ple,
                  w_ple_gate, g_final):
    b = lambda a: a[0].astype(BF16)
    r = lambda a: a[0].reshape(1, -1).astype(F32)
    win = w_in[0]
    w_dt = win[:, COL_DT:COL_DT + SSM_HEADS].astype(BF16)
    return dict(
        g_ffn1=r(g_ffn1), w1_gu=b(w1_gu), w1_down=b(w1_down), g_mix=r(g_mix),
        w_in=win.astype(BF16), w_dt=w_dt, w_dt_t=w_dt.T,
        w_gr=win[:, COL_GR:COL_GR + D_MODEL].astype(BF16), w_gs=win[:, COL_GS:COL_GS + D_MODEL].astype(BF16),
        ret_gn_g=ret_gn_g[0].astype(F32), ret_gn_b=ret_gn_b[0].astype(F32),
        conv_w=conv_w[0].astype(F32), conv_b=conv_b[0].astype(F32),
        dt_bias=dt_bias[0].astype(F32), a_log=a_log[0].astype(F32),
        d_skip_x=jnp.repeat(d_skip[0].astype(F32), SSM_HEADDIM).reshape(1, SSM_INNER),
        ssm_norm_g=ssm_norm_g[0].astype(F32),
        w_br_ret=b(w_br_ret), w_br_ssm=b(w_br_ssm), w_out=b(w_out),
        g_ffn2=r(g_ffn2), w2_gu=b(w2_gu), w2_down=b(w2_down), g_ple=r(g_ple),
        w_ple=b(w_ple), w_ple_gate=b(w_ple_gate), g_final=g_final.reshape(1, -1).astype(F32),
    )


def kernel(x_prompt, x_sample, state_ret, state_ssm, state_conv, p_prompt, p_sample, g_ffn1, w1_gu, w1_down, g_mix, w_in, ret_gn_g, ret_gn_b, conv_w, conv_b, dt_bias, a_log, d_skip, ssm_norm_g, w_br_ret, w_br_ssm, w_out, g_ffn2, w2_gu, w2_down, g_ple, w_ple, w_ple_gate, g_final):
    assert g_ffn1.shape[0] == 1, "single-layer trunk"
    w = _prep_weights(g_ffn1, w1_gu, w1_down, g_mix, w_in, ret_gn_g, ret_gn_b, conv_w, conv_b, dt_bias, a_log,
                      d_skip, ssm_norm_g, w_br_ret, w_br_ssm, w_out, g_ffn2, w2_gu, w2_down, g_ple, w_ple,
                      w_ple_gate, g_final)
    y_p, ret_p, ssm_p, conv_p = _trunk(x_prompt, p_prompt[0], 0, None, None, None, w, CFG)
    y_s, ret_s, ssm_s, conv_s = _trunk(x_sample, p_sample[0], PAST_LEN, state_ret[0], state_ssm[0],
                                       state_conv[0], w, CFG)
    e = lambda a: a[None]
    return (y_p, y_s, e(ret_p), e(ssm_p), e(conv_p), e(ret_s), e(ssm_s), e(conv_s))
```

```python
import functools

import numpy as np
import jax
import jax.numpy as jnp
from jax import lax
from jax.experimental import pallas as pl
from jax.experimental.pallas import tpu as pltpu

F32 = jnp.float32
BF16 = jnp.bfloat16

D_MODEL = 2048
PAST_LEN = 4096
EPS = 1e-6
RET_HEADS = 8
RET_DK = 256
RET_DV = 512
RET_QK = RET_HEADS * RET_DK
RET_V = RET_HEADS * RET_DV
ROPE_THETA = 10000.0
SSM_INNER = 2 * D_MODEL
SSM_HEADDIM = 64
SSM_HEADS = SSM_INNER // SSM_HEADDIM
SSM_GROUPS = 8
SSM_HPG = SSM_HEADS // SSM_GROUPS
SSM_GW = SSM_HPG * SSM_HEADDIM
SSM_STATE = 128
SSM_CONV = 4
SSM_CONV_DIM = SSM_INNER + 2 * SSM_GROUPS * SSM_STATE
PLE_DIM = 256

COL_Q = 0
COL_K = COL_Q + RET_QK
COL_V = COL_K + RET_QK
COL_RG = COL_V + RET_V
COL_Z = COL_RG + RET_V
COL_XBC = COL_Z + SSM_INNER
N_BIG = COL_XBC + SSM_CONV_DIM
COL_DT = N_BIG
COL_GR = COL_DT + SSM_HEADS
COL_GS = COL_GR + D_MODEL

SUBLANES = 8
HALF_LANES = 64
VMEM_LIMIT = 56 * 1024 * 1024
ROW_CHUNK = 256


def _cparams(sem):
    return pltpu.CompilerParams(dimension_semantics=sem, vmem_limit_bytes=VMEM_LIMIT)


def _rms_bf16(x, g):
    ms = jnp.mean(x * x, axis=-1, keepdims=True)
    return (x * lax.rsqrt(ms + EPS) * g).astype(BF16)


def _silu(x):
    return x * jax.nn.sigmoid(x)


def _dot(a, b):
    return jnp.dot(a, b, preferred_element_type=F32)


def _dot_nt(a, b):
    return lax.dot_general(a, b, (((1,), (1,)), ((), ())), preferred_element_type=F32)


def _dot_tn(a, b):
    return lax.dot_general(a, b, (((0,), (0,)), ((), ())), preferred_element_type=F32)


def _ffn_kernel(x_ref, g_ref, gn_ref, wg_ref, wu_ref, wd_ref, o_ref, hn_ref, h_scr, *, nj):
    j = pl.program_id(1)

    @pl.when(j == 0)
    def _():
        h_scr[...] = _rms_bf16(x_ref[...], g_ref[...])
        o_ref[...] = jnp.zeros_like(o_ref)

    h = h_scr[...]
    gate = _dot(h, wg_ref[...])
    up = _dot(h, wu_ref[...])
    act = (_silu(gate) * up).astype(BF16)
    o_ref[...] += _dot(act, wd_ref[...])

    @pl.when(j == nj - 1)
    def _():
        xn = x_ref[...] + 0.5 * o_ref[...]
        o_ref[...] = xn
        hn_ref[...] = _rms_bf16(xn, gn_ref[...])


def _ffn(x, g, g_next, w_gu, w_down, *, tm, tf):
    n, d = x.shape
    f = w_down.shape[0]
    nj = f // tf
    return pl.pallas_call(
        functools.partial(_ffn_kernel, nj=nj),
        out_shape=(jax.ShapeDtypeStruct((n, d), F32), jax.ShapeDtypeStruct((n, d), BF16)),
        grid=(n // tm, nj),
        in_specs=[
            pl.BlockSpec((tm, d), lambda i, j: (i, 0)),
            pl.BlockSpec((1, d), lambda i, j: (0, 0)),
            pl.BlockSpec((1, d), lambda i, j: (0, 0)),
            pl.BlockSpec((d, tf), lambda i, j: (0, j)),
            pl.BlockSpec((d, tf), lambda i, j: (0, j + nj)),
            pl.BlockSpec((tf, d), lambda i, j: (j, 0)),
        ],
        out_specs=(pl.BlockSpec((tm, d), lambda i, j: (i, 0)),
                   pl.BlockSpec((tm, d), lambda i, j: (i, 0))),
        scratch_shapes=[pltpu.VMEM((tm, d), BF16)],
        compiler_params=_cparams(("parallel", "arbitrary")),
        name="ffn",
    )(x, g, g_next, w_gu, w_gu, w_down)


def _inproj_kernel(h_ref, w_ref, cos_ref, sin_ref, o_ref, *, tn, nq, nrope, rc):
    j = pl.program_id(1)
    nr = h_ref.shape[0] // rc

    @pl.when(j >= nrope)
    def _():
        for r in range(nr):
            rows = pl.ds(r * rc, rc)
            o_ref[rows, :] = _dot(h_ref[rows, :], w_ref[...]).astype(BF16)

    @pl.when(j < nrope)
    def _():
        scale = jnp.where(j >= nq, RET_DK ** -0.5, 1.0).astype(F32)
        half = RET_DK // 2
        for r in range(nr):
            rows = pl.ds(r * rc, rc)
            acc = _dot(h_ref[rows, :], w_ref[...])
            c = cos_ref[rows, :] * scale
            s = sin_ref[rows, :] * scale
            for hh in range(tn // RET_DK):
                lo = hh * RET_DK
                x1 = acc[:, lo:lo + half]
                x2 = acc[:, lo + half:lo + RET_DK]
                o_ref[rows, lo:lo + half] = (x1 * c - x2 * s).astype(BF16)
                o_ref[rows, lo + half:lo + RET_DK] = (x2 * c + x1 * s).astype(BF16)


def _inproj(h, w_in, cos, sin, *, tm, tn):
    n, d = h.shape
    nrow = cos.shape[0] // tm
    return pl.pallas_call(
        functools.partial(_inproj_kernel, tn=tn, nq=RET_QK // tn, nrope=2 * RET_QK // tn, rc=min(ROW_CHUNK, tm)),
        out_shape=jax.ShapeDtypeStruct((n, N_BIG), BF16),
        grid=(n // tm, N_BIG // tn),
        in_specs=[
            pl.BlockSpec((tm, d), lambda i, j: (i, 0)),
            pl.BlockSpec((d, tn), lambda i, j: (0, j)),
            pl.BlockSpec((tm, RET_DK // 2), lambda i, j: (i % nrow, 0)),
            pl.BlockSpec((tm, RET_DK // 2), lambda i, j: (i % nrow, 0)),
        ],
        out_specs=pl.BlockSpec((tm, tn), lambda i, j: (i, j)),
        compiler_params=_cparams(("parallel", "arbitrary")),
        name="inproj",
    )(h, w_in, cos, sin)


def _softplus(x):
    return jnp.maximum(x, 0.0) + jnp.log1p(jnp.exp(-jnp.abs(x)))


def _dt_kernel(h_ref, w_ref, wt_ref, br_ref, bc_ref, ar_ref, ac_ref,
               dtc_ref, acc_ref, dtr_ref, acr_ref, *, q, valid):
    h = h_ref[...]
    tm = h.shape[0]
    raw_c = _dot(h, w_ref[...])
    raw_r = _dot_nt(wt_ref[...], h)
    dt_c = _softplus(raw_c + br_ref[...])
    dt_r = _softplus(raw_r + bc_ref[...])
    if valid < q:
        row = lax.broadcasted_iota(jnp.int32, dt_c.shape, 0) % q
        col = lax.broadcasted_iota(jnp.int32, dt_r.shape, 1) % q
        dt_c = jnp.where(row < valid, dt_c, 0.0)
        dt_r = jnp.where(col < valid, dt_r, 0.0)
    dta_c = dt_c * (-jnp.exp(ar_ref[...]))
    dta_r = dt_r * (-jnp.exp(ac_ref[...]))
    i = lax.broadcasted_iota(jnp.int32, (tm, tm), 0)
    j = lax.broadcasted_iota(jnp.int32, (tm, tm), 1)
    same = (i // q) == (j // q)
    tri = jnp.where(same & (j <= i), 1.0, 0.0).astype(F32)
    trit = jnp.where(same & (i <= j), 1.0, 0.0).astype(F32)
    ac_c = jnp.dot(tri, dta_c, preferred_element_type=F32, precision=lax.Precision.HIGHEST)
    ac_r = jnp.dot(dta_r, trit, preferred_element_type=F32, precision=lax.Precision.HIGHEST)
    for g in range(SSM_GROUPS):
        lo = g * SSM_HPG
        dtc_ref[g] = dt_c[:, lo:lo + SSM_HPG]
        acc_ref[g] = ac_c[:, lo:lo + SSM_HPG]
        for ci in range(tm // q):
            dtr_ref[g, ci] = dt_r[lo:lo + SSM_HPG, ci * q:(ci + 1) * q]
            acr_ref[g, ci] = ac_r[lo:lo + SSM_HPG, ci * q:(ci + 1) * q]


def _dt_proj(h, w_dt, w_dt_t, dt_bias, a_log, *, tm, q, valid):
    n, d = h.shape
    hh = SSM_HEADS
    col = jax.ShapeDtypeStruct((SSM_GROUPS, n, SSM_HPG), F32)
    row = jax.ShapeDtypeStruct((SSM_GROUPS, n // q, SSM_HPG, q), F32)
    return pl.pallas_call(
        functools.partial(_dt_kernel, q=q, valid=valid),
        out_shape=(col, col, row, row),
        grid=(n // tm,),
        in_specs=[
            pl.BlockSpec((tm, d), lambda i: (i, 0)),
            pl.BlockSpec((d, hh), lambda i: (0, 0)),
            pl.BlockSpec((hh, d), lambda i: (0, 0)),
            pl.BlockSpec((1, hh), lambda i: (0, 0)),
            pl.BlockSpec((hh, 1), lambda i: (0, 0)),
            pl.BlockSpec((1, hh), lambda i: (0, 0)),
            pl.BlockSpec((hh, 1), lambda i: (0, 0)),
        ],
        out_specs=(
            pl.BlockSpec((SSM_GROUPS, tm, SSM_HPG), lambda i: (0, i, 0)),
            pl.BlockSpec((SSM_GROUPS, tm, SSM_HPG), lambda i: (0, i, 0)),
            pl.BlockSpec((SSM_GROUPS, tm // q, SSM_HPG, q), lambda i: (0, i, 0, 0)),
            pl.BlockSpec((SSM_GROUPS, tm // q, SSM_HPG, q), lambda i: (0, i, 0, 0)),
        ),
        compiler_params=_cparams(("parallel",)),
        name="dt_proj",
    )(h, w_dt, w_dt_t, dt_bias.reshape(1, hh), dt_bias.reshape(hh, 1),
      a_log.reshape(1, hh), a_log.reshape(hh, 1))


def _ret_kernel(*refs, c, nchunk, sb, has_s0):
    if has_s0:
        (q_ref, k_ref, v_ref, rg_ref, dm_ref, qd_ref, kd_ref, cd_ref, gg_ref, gb_ref, s0_ref,
         o_ref, s_ref) = refs
    else:
        (q_ref, k_ref, v_ref, rg_ref, dm_ref, qd_ref, kd_ref, cd_ref, gg_ref, gb_ref,
         o_ref, s_ref) = refs
    t = pl.program_id(2)

    @pl.when(t == 0)
    def _():
        if has_s0:
            s_ref[...] = s0_ref[...]
        else:
            s_ref[...] = jnp.zeros_like(s_ref)

    dm = dm_ref[0]
    qd = qd_ref[0]
    kd = kd_ref[0]
    cd = cd_ref[0]
    gg = gg_ref[...]
    gb = gb_ref[...]
    for si, ci in ((si, ci) for si in range(sb) for ci in range(nchunk)):
        sl = pl.ds((si * nchunk + ci) * c, c)
        q = q_ref[sl, :]
        k = k_ref[sl, :]
        v = v_ref[sl, :]
        s = s_ref[si, 0]
        att = _dot_nt(q, k) * dm
        o = _dot(att.astype(BF16), v) + _dot(q, s.astype(BF16)) * qd
        kdk = (k.astype(F32) * kd).astype(BF16)
        s_ref[si, 0] = s * cd + _dot_tn(kdk, v)
        mu = jnp.mean(o, axis=-1, keepdims=True)
        dev = o - mu
        var = jnp.mean(dev * dev, axis=-1, keepdims=True)
        on = dev * lax.rsqrt(var + EPS)
        rg = rg_ref[sl, :].astype(F32)
        o_ref[sl, :] = (_silu(rg) * (on * gg + gb)).astype(BF16)


def _ret_tables(c_true, c):
    hs = np.arange(RET_HEADS, dtype=np.float64)
    log_g = np.log1p(-np.exp2(-5.0 - hs))
    idx = np.arange(c, dtype=np.float64)
    diff = idx[:, None] - idx[None, :]
    dmat = np.where(diff[None] >= 0, np.exp(np.maximum(diff, 0.0)[None] * log_g[:, None, None]), 0.0)
    q_dec = np.exp((idx[None, :] + 1.0) * log_g[:, None])
    k_dec = np.where(idx[None, :] < c_true,
                     np.exp(np.maximum(c_true - 1.0 - idx[None, :], 0.0) * log_g[:, None]), 0.0)
    c_dec = np.exp(c_true * log_g)
    qd = np.broadcast_to(q_dec[:, :, None], (RET_HEADS, c, RET_DV))
    kd = np.broadcast_to(k_dec[:, :, None], (RET_HEADS, c, RET_DK))
    cd = np.broadcast_to(c_dec[:, None, None], (RET_HEADS, 1, RET_DV))
    f = lambda a: jnp.asarray(np.ascontiguousarray(a), dtype=F32)
    return f(dmat), f(qd), f(kd), f(cd)


def _retention(big, gn_g, gn_b, s0, *, nseq, seqlen, c_true, c, tb, sb):
    n = big.shape[0]
    nt = seqlen // tb
    assert sb == 1 or nt == 1
    nchunk = tb // c
    rb = sb * tb
    dm, qd, kd, cd = _ret_tables(c_true, c)
    has_s0 = s0 is not None
    kq, kk = COL_Q // RET_DK, COL_K // RET_DK
    kv, kr = COL_V // RET_DV, COL_RG // RET_DV
    row = lambda b, h, t: b * nt + t
    in_specs = [
        pl.BlockSpec((rb, RET_DK), lambda b, h, t: (row(b, h, t), kq + h)),
        pl.BlockSpec((rb, RET_DK), lambda b, h, t: (row(b, h, t), kk + h)),
        pl.BlockSpec((rb, RET_DV), lambda b, h, t: (row(b, h, t), kv + h)),
        pl.BlockSpec((rb, RET_DV), lambda b, h, t: (row(b, h, t), kr + h)),
        pl.BlockSpec((1, c, c), lambda b, h, t: (h, 0, 0)),
        pl.BlockSpec((1, c, RET_DV), lambda b, h, t: (h, 0, 0)),
        pl.BlockSpec((1, c, RET_DK), lambda b, h, t: (h, 0, 0)),
        pl.BlockSpec((1, 1, RET_DV), lambda b, h, t: (h, 0, 0)),
        pl.BlockSpec((1, RET_DV), lambda b, h, t: (0, h)),
        pl.BlockSpec((1, RET_DV), lambda b, h, t: (0, h)),
    ]
    args = [big, big, big, big, dm, qd, kd, cd, gn_g.reshape(1, RET_V), gn_b.reshape(1, RET_V)]
    st_spec = pl.BlockSpec((sb, 1, RET_DK, RET_DV), lambda b, h, t: (b, h, 0, 0))
    if has_s0:
        in_specs.append(st_spec)
        args.append(s0)
    return pl.pallas_call(
        functools.partial(_ret_kernel, c=c, nchunk=nchunk, sb=sb, has_s0=has_s0),
        out_shape=(jax.ShapeDtypeStruct((n, RET_V), BF16),
                   jax.ShapeDtypeStruct((nseq, RET_HEADS, RET_DK, RET_DV), F32)),
        grid=(nseq // sb, RET_HEADS, nt),
        in_specs=in_specs,
        out_specs=(pl.BlockSpec((rb, RET_DV), lambda b, h, t: (row(b, h, t), h)), st_spec),
        compiler_params=_cparams(("parallel", "parallel", "arbitrary")),
        name="retention",
    )(*args)


def _conv_silu(raw_scr, cw, bias, tb):
    taps = SSM_CONV
    acc = bias
    for s in range(taps):
        acc = acc + raw_scr[pl.ds(SUBLANES - s, tb), :] * cw[taps - 1 - s:taps - s, :]
    return _silu(acc)


def _lane_expand(v, width):
    rows = v.shape[0]
    lane = lax.broadcasted_iota(jnp.int32, (rows, 2 * width), 1)
    parts = []
    for m in range(v.shape[1] // 2):
        a = jnp.broadcast_to(v[:, 2 * m:2 * m + 1], (rows, 2 * width))
        b = jnp.broadcast_to(v[:, 2 * m + 1:2 * m + 2], (rows, 2 * width))
        parts.append(jnp.where(lane < width, a, b))
    return jnp.concatenate(parts, axis=1)


def _ssd_kernel(*refs, q, nchunk, sb, has_s0):
    if has_s0:
        (xs_ref, b_ref, c_ref, z_ref, dtc_ref, acc_ref, dtr_ref, acr_ref,
         cwx_ref, cwb_ref, cwc_ref, cbx_ref, cbb_ref, cbc_ref, dsk_ref, ng_ref,
         s0_ref, hx0_ref, hb0_ref, hc0_ref,
         y_ref, so_ref, hx, hb, hc, st_scr) = refs
        hist0 = (hx0_ref, hb0_ref, hc0_ref)
    else:
        (xs_ref, b_ref, c_ref, z_ref, dtc_ref, acc_ref, dtr_ref, acr_ref,
         cwx_ref, cwb_ref, cwc_ref, cbx_ref, cbb_ref, cbc_ref, dsk_ref, ng_ref,
         y_ref, so_ref, hx, hb, hc, st_scr) = refs
        hist0 = (None, None, None)
    t = pl.program_id(2)
    conv_in = ((xs_ref, hx, hist0[0], cwx_ref, cbx_ref),
               (b_ref, hb, hist0[1], cwb_ref, cbb_ref),
               (c_ref, hc, hist0[2], cwc_ref, cbc_ref))

    @pl.when(t == 0)
    def _():
        for si in range(sb):
            if has_s0:
                st_scr[si] = s0_ref[si, 0].T
            else:
                st_scr[si] = jnp.zeros(st_scr.shape[1:], F32)

    def start_sequence(si):
        @pl.when(t == 0)
        def _():
            for _, raw_scr, h0_ref, _, _ in conv_in:
                if has_s0:
                    raw_scr[0:SUBLANES, :] = h0_ref[si]
                else:
                    raw_scr[0:SUBLANES, :] = jnp.zeros((SUBLANES, raw_scr.shape[1]), F32)

    ii = lax.broadcasted_iota(jnp.int32, (q, q), 0)
    jj = lax.broadcasted_iota(jnp.int32, (q, q), 1)
    causal = ii >= jj
    lane = lax.broadcasted_iota(jnp.int32, (q, 2 * HALF_LANES), 1)
    dsk = dsk_ref[...]
    ng = ng_ref[...]

    def chunk(si, ci):
        cidx = si * nchunk + ci
        sl = pl.ds(pl.multiple_of(cidx * q, q), q)
        conv = []
        for raw_ref, raw_scr, _, cw_ref, cb_ref in conv_in:
            raw_scr[SUBLANES:SUBLANES + q, :] = raw_ref[sl, :].astype(F32)
            conv.append(_conv_silu(raw_scr, cw_ref[...], cb_ref[...], q))
            raw_scr[0:SUBLANES, :] = raw_scr[q:q + SUBLANES, :]
        x = conv[0]
        bq = conv[1].astype(BF16)
        cq = conv[2].astype(BF16)
        dtc = dtc_ref[0, sl, :]
        acc = acc_ref[0, sl, :]
        dtr = dtr_ref[0, cidx]
        acr = acr_ref[0, cidx]
        state = st_scr[si]

        cb = jnp.where(causal, _dot_nt(cq, bq), 0.0)
        pairs = []
        for m in range(SSM_HPG // 2):
            ws = []
            for r in (2 * m, 2 * m + 1):
                seg = acc[:, r:r + 1] - acr[r:r + 1, :]
                ws.append((jnp.exp(jnp.minimum(seg, 0.0)) * (cb * dtr[r:r + 1, :])).astype(BF16))
            xp = x[:, m * 2 * HALF_LANES:(m + 1) * 2 * HALF_LANES]
            x_lo = jnp.where(lane < HALF_LANES, xp, 0.0).astype(BF16)
            x_hi = jnp.where(lane >= HALF_LANES, xp, 0.0).astype(BF16)
            pairs.append(_dot(jnp.concatenate(ws, axis=1), jnp.concatenate([x_lo, x_hi], axis=0)))
        y = jnp.concatenate(pairs, axis=1)

        ea = _lane_expand(jnp.exp(acc), SSM_HEADDIM)
        y = y + _dot(cq, state.astype(BF16)) * ea
        wend = jnp.exp(acc[q - 1:q, :] - acc) * dtc
        xw = (x * _lane_expand(wend, SSM_HEADDIM)).astype(BF16)
        st_scr[si] = state * ea[q - 1:q, :] + _dot_tn(bq, xw)

        y = y + dsk * x
        y = y * _silu(z_ref[sl, :].astype(F32))
        ms = jnp.mean(y * y, axis=-1, keepdims=True)
        y_ref[sl, :] = (y * lax.rsqrt(ms + EPS) * ng).astype(BF16)

    for si in range(sb):
        start_sequence(si)
        if nchunk == 1:
            chunk(si, 0)
        else:
            lax.fori_loop(0, nchunk, lambda ci, carry, si=si: (chunk(si, ci), carry)[1], 0)

    @pl.when(t == pl.num_programs(2) - 1)
    def _():
        for si in range(sb):
            so_ref[si, 0] = st_scr[si].T


def _ssd(big, dts, conv_w, conv_b, d_skip_x, norm_g, s0, hist0, *, nseq, seqlen, q, tb, sb):
    n = big.shape[0]
    nt = seqlen // tb
    assert sb == 1 or nt == 1
    nchunk = tb // q
    rb = sb * tb
    has_s0 = s0 is not None
    dtc, acc, dtr, acr = dts
    gw, ns = SSM_GW, SSM_STATE
    kz, kx = COL_Z // gw, COL_XBC // gw
    kb = (COL_XBC + SSM_INNER) // ns
    kc = kb + SSM_GROUPS
    cb_b = SSM_INNER // ns
    cb_c = cb_b + SSM_GROUPS
    row = lambda b, g, t: b * nt + t
    in_specs = [
        pl.BlockSpec((rb, gw), lambda b, g, t: (row(b, g, t), kx + g)),
        pl.BlockSpec((rb, ns), lambda b, g, t: (row(b, g, t), kb + g)),
        pl.BlockSpec((rb, ns), lambda b, g, t: (row(b, g, t), kc + g)),
        pl.BlockSpec((rb, gw), lambda b, g, t: (row(b, g, t), kz + g)),
        pl.BlockSpec((1, rb, SSM_HPG), lambda b, g, t: (g, row(b, g, t), 0)),
        pl.BlockSpec((1, rb, SSM_HPG), lambda b, g, t: (g, row(b, g, t), 0)),
        pl.BlockSpec((1, rb // q, SSM_HPG, q), lambda b, g, t: (g, row(b, g, t), 0, 0)),
        pl.BlockSpec((1, rb // q, SSM_HPG, q), lambda b, g, t: (g, row(b, g, t), 0, 0)),
        pl.BlockSpec((SSM_CONV, gw), lambda b, g, t: (0, g)),
        pl.BlockSpec((SSM_CONV, ns), lambda b, g, t: (0, cb_b + g)),
        pl.BlockSpec((SSM_CONV, ns), lambda b, g, t: (0, cb_c + g)),
        pl.BlockSpec((1, gw), lambda b, g, t: (0, g)),
        pl.BlockSpec((1, ns), lambda b, g, t: (0, cb_b + g)),
        pl.BlockSpec((1, ns), lambda b, g, t: (0, cb_c + g)),
        pl.BlockSpec((1, gw), lambda b, g, t: (0, g)),
        pl.BlockSpec((1, gw), lambda b, g, t: (0, g)),
    ]
    cbias = conv_b.reshape(1, SSM_CONV_DIM)
    args = [big, big, big, big, dtc, acc, dtr, acr, conv_w, conv_w, conv_w, cbias, cbias, cbias,
            d_skip_x, norm_g.reshape(1, SSM_INNER)]
    st_spec = pl.BlockSpec((sb, 1, gw, ns), lambda b, g, t: (b, g, 0, 0))
    if has_s0:
        in_specs += [
            st_spec,
            pl.BlockSpec((sb, SUBLANES, gw), lambda b, g, t: (b, 0, g)),
            pl.BlockSpec((sb, SUBLANES, ns), lambda b, g, t: (b, 0, cb_b + g)),
            pl.BlockSpec((sb, SUBLANES, ns), lambda b, g, t: (b, 0, cb_c + g)),
        ]
        args += [s0, hist0, hist0, hist0]
    return pl.pallas_call(
        functools.partial(_ssd_kernel, q=q, nchunk=nchunk, sb=sb, has_s0=has_s0),
        out_shape=(jax.ShapeDtypeStruct((n, SSM_INNER), BF16),
                   jax.ShapeDtypeStruct((nseq, SSM_GROUPS, gw, ns), F32)),
        grid=(nseq // sb, SSM_GROUPS, nt),
        in_specs=in_specs,
        out_specs=(pl.BlockSpec((rb, gw), lambda b, g, t: (row(b, g, t), g)), st_spec),
        scratch_shapes=[pltpu.VMEM((SUBLANES + q, gw), F32), pltpu.VMEM((SUBLANES + q, ns), F32),
                        pltpu.VMEM((SUBLANES + q, ns), F32), pltpu.VMEM((sb, ns, gw), F32)],
        compiler_params=_cparams(("parallel", "parallel", "arbitrary")),
        name="ssd",
    )(*args)


def _merge_kernel(h_ref, ret_ref, ssm_ref, wgr_ref, wgs_ref, wr_ref, ws_ref, o_ref):
    h = h_ref[...]
    gr = jax.nn.sigmoid(_dot(h, wgr_ref[...]))
    gs = jax.nn.sigmoid(_dot(h, wgs_ref[...]))
    a = _dot(ret_ref[...], wr_ref[...])
    b = _dot(ssm_ref[...], ws_ref[...])
    o_ref[...] = (gr * a + gs * b).astype(BF16)


def _merge(h, ret, ssm, w_gr, w_gs, w_r, w_s, *, tm, tn):
    n, d = h.shape
    kr, ks = ret.shape[1], ssm.shape[1]
    return pl.pallas_call(
        _merge_kernel,
        out_shape=jax.ShapeDtypeStruct((n, d), BF16),
        grid=(n // tm, d // tn),
        in_specs=[
            pl.BlockSpec((tm, d), lambda i, j: (i, 0)),
            pl.BlockSpec((tm, kr), lambda i, j: (i, 0)),
            pl.BlockSpec((tm, ks), lambda i, j: (i, 0)),
            pl.BlockSpec((d, tn), lambda i, j: (0, j)),
            pl.BlockSpec((d, tn), lambda i, j: (0, j)),
            pl.BlockSpec((kr, tn), lambda i, j: (0, j)),
            pl.BlockSpec((ks, tn), lambda i, j: (0, j)),
        ],
        out_specs=pl.BlockSpec((tm, tn), lambda i, j: (i, j)),
        compiler_params=_cparams(("parallel", "arbitrary")),
        name="merge",
    )(h, ret, ssm, w_gr, w_gs, w_r, w_s)


def _outproj_kernel(m_ref, w_ref, x_ref, o_ref, *, rc):
    for r in range(m_ref.shape[0] // rc):
        rows = pl.ds(r * rc, rc)
        o_ref[rows, :] = x_ref[rows, :] + _dot(m_ref[rows, :], w_ref[...])


def _outproj(m, w, x, *, tm, tn):
    n, d = x.shape
    k = m.shape[1]
    return pl.pallas_call(
        functools.partial(_outproj_kernel, rc=min(ROW_CHUNK, tm)),
        out_shape=jax.ShapeDtypeStruct((n, d), F32),
        grid=(n // tm, d // tn),
        in_specs=[
            pl.BlockSpec((tm, k), lambda i, j: (i, 0)),
            pl.BlockSpec((k, tn), lambda i, j: (0, j)),
            pl.BlockSpec((tm, tn), lambda i, j: (i, j)),
        ],
        out_specs=pl.BlockSpec((tm, tn), lambda i, j: (i, j)),
        compiler_params=_cparams(("parallel", "arbitrary")),
        name="outproj",
    )(m, w, x)


def _ple_kernel(x_ref, h_ref, p_ref, wp_ref, wg_ref, gf_ref, o_ref):
    pe = _dot(p_ref[...].astype(BF16), wp_ref[...])
    gt = jax.nn.sigmoid(_dot(h_ref[...], wg_ref[...]))
    x = x_ref[...] + pe * gt
    ms = jnp.mean(x * x, axis=-1, keepdims=True)
    o_ref[...] = x * lax.rsqrt(ms + EPS) * gf_ref[...]


def _ple(x, h, p, w_ple, w_gate, g_final, *, tm):
    n, d = x.shape
    pd = p.shape[1]
    return pl.pallas_call(
        _ple_kernel,
        out_shape=jax.ShapeDtypeStruct((n, d), F32),
        grid=(n // tm,),
        in_specs=[
            pl.BlockSpec((tm, d), lambda i: (i, 0)),
            pl.BlockSpec((tm, d), lambda i: (i, 0)),
            pl.BlockSpec((tm, pd), lambda i: (i, 0)),
            pl.BlockSpec((pd, d), lambda i: (0, 0)),
            pl.BlockSpec((d, d), lambda i: (0, 0)),
            pl.BlockSpec((1, d), lambda i: (0, 0)),
        ],
        out_specs=pl.BlockSpec((tm, d), lambda i: (i, 0)),
        compiler_params=_cparams(("parallel",)),
        name="ple",
    )(x, h, p, w_ple, w_gate, g_final)


def _rope_tables(pos0, seqlen, rows):
    half = RET_DK // 2
    inv = ROPE_THETA ** (-jnp.arange(half, dtype=F32) / half)
    pos = (pos0 + jnp.arange(seqlen, dtype=jnp.int32)).astype(F32)
    ang = pos[:, None] * inv[None, :]
    reps = max(rows // seqlen, 1)
    return jnp.tile(jnp.cos(ang), (reps, 1)), jnp.tile(jnp.sin(ang), (reps, 1))


def _pick(n, pref):
    t = pref
    while n % t:
        t //= 2
    return t


def _trunk(x, p, pos0, s_ret, s_ssm, s_conv, w, cfg):
    nseq, seqlen, d = x.shape
    n = nseq * seqlen
    x = x.reshape(n, d)
    p = p.reshape(n, p.shape[-1])
    tm = _pick(n, cfg["tm"])

    x1, h = _ffn(x, w["g_ffn1"], w["g_mix"], w["w1_gu"], w["w1_down"], tm=_pick(n, cfg["tm_ffn"]), tf=cfg["tf"])

    cos, sin = _rope_tables(pos0, seqlen, tm)
    big = _inproj(h, w["w_in"], cos, sin, tm=tm, tn=cfg["tn_in"])

    q = cfg["q_ssd"]
    lpad = -(-seqlen // q) * q
    if lpad != seqlen:
        pad = lambda a: jnp.pad(a.reshape(nseq, seqlen, -1), ((0, 0), (0, lpad - seqlen), (0, 0))).reshape(nseq * lpad, -1)
        big_m, h_m = pad(big), pad(h)
    else:
        big_m, h_m = big, h
    npad = nseq * lpad
    valid = min(seqlen, q)

    dts = _dt_proj(h_m, w["w_dt"], w["w_dt_t"], w["dt_bias"], w["a_log"], tm=_pick(npad, 512), q=q, valid=valid)

    c = cfg["c_ret"] if lpad % cfg["c_ret"] == 0 else q
    tb_ret, tb_ssd = _pick(lpad, cfg["tb_ret"]), _pick(lpad, cfg["tb_ssd"])
    sb = _pick(nseq, cfg["seq_batch"]) if max(tb_ret, tb_ssd) == lpad else 1
    ret, s_ret_new = _retention(big_m, w["ret_gn_g"], w["ret_gn_b"], s_ret, nseq=nseq, seqlen=lpad,
                                c_true=min(seqlen, c), c=c, tb=tb_ret, sb=sb)

    if s_ssm is not None:
        s0 = s_ssm.reshape(nseq, SSM_GROUPS, SSM_GW, SSM_STATE)
        hist0 = jnp.pad(s_conv, ((0, 0), (SUBLANES - (SSM_CONV - 1), 0), (0, 0)))
    else:
        s0, hist0 = None, None
    ssm, s_ssm_new = _ssd(big_m, dts, w["conv_w"], w["conv_b"], w["d_skip_x"], w["ssm_norm_g"], s0, hist0,
                          nseq=nseq, seqlen=lpad, q=q, tb=tb_ssd, sb=sb)
    s_ssm_new = s_ssm_new.reshape(nseq, SSM_HEADS, SSM_HEADDIM, SSM_STATE)

    if lpad != seqlen:
        unpad = lambda a: a.reshape(nseq, lpad, -1)[:, :seqlen].reshape(n, -1)
        ret, ssm = unpad(ret), unpad(ssm)

    keep = min(seqlen, SSM_CONV - 1)
    xbc_tail = big.reshape(nseq, seqlen, N_BIG)[:, seqlen - keep:, COL_XBC:].astype(F32)
    if keep < SSM_CONV - 1:
        prev = jnp.zeros((nseq, SSM_CONV - 1, SSM_CONV_DIM), F32) if s_conv is None else s_conv.astype(F32)
        xbc_tail = jnp.concatenate([prev, xbc_tail], axis=1)[:, -(SSM_CONV - 1):]

    merged = _merge(h, ret, ssm, w["w_gr"], w["w_gs"], w["w_br_ret"], w["w_br_ssm"], tm=_pick(n, cfg["tm_mg"]), tn=cfg["tn_mg"])
    x2 = _outproj(merged, w["w_out"], x1, tm=tm, tn=cfg["tn_out"])
    x3, h3 = _ffn(x2, w["g_ffn2"], w["g_ple"], w["w2_gu"], w["w2_down"], tm=_pick(n, cfg["tm_ffn"]), tf=cfg["tf"])
    y = _ple(x3, h3, p, w["w_ple"], w["w_ple_gate"], w["g_final"], tm=_pick(n, cfg["tm_ple"]))
    return y.reshape(nseq, seqlen, d), s_ret_new, s_ssm_new, xbc_tail


CFG = dict(tm=1024, tm_ffn=512, tf=512, tn_in=1024, q_ssd=128, c_ret=256, tb_ret=1024, tb_ssd=1024, seq_batch=4,
           tm_mg=512, tn_mg=512, tn_out=1024, tm_ple=512)


def _prep_weights(g_ffn1, w1_gu, w1_down, g_mix, w_in, ret_gn_g, ret_gn_b, conv_w, conv_b, dt_bias, a_log,
                  d_skip, ssm_norm_g, w_br_ret, w_br_ssm, w_out, g_ffn2, w2_gu, w2_down, g_ple, w_ple,
                  w_ple_gate, g_final):
    b = lambda a: a[0].astype(BF16)
    r = lambda a: a[0].reshape(1, -1).astype(F32)
    win = w_in[0]
    w_dt = win[:, COL_DT:COL_DT + SSM_HEADS].astype(BF16)
    return dict(
        g_ffn1=r(g_ffn1), w1_gu=b(w1_gu), w1_down=b(w1_down), g_mix=r(g_mix),
        w_in=win.astype(BF16), w_dt=w_dt, w_dt_t=w_dt.T,
        w_gr=win[:, COL_GR:COL_GR + D_MODEL].astype(BF16), w_gs=win[:, COL_GS:COL_GS + D_MODEL].astype(BF16),
        ret_gn_g=ret_gn_g[0].astype(F32), ret_gn_b=ret_gn_b[0].astype(F32),
        conv_w=conv_w[0].astype(F32), conv_b=conv_b[0].astype(F32),
        dt_bias=dt_bias[0].astype(F32), a_log=a_log[0].astype(F32),
        d_skip_x=jnp.repeat(d_skip[0].astype(F32), SSM_HEADDIM).reshape(1, SSM_INNER),
        ssm_norm_g=ssm_norm_g[0].astype(F32),
        w_br_ret=b(w_br_ret), w_br_ssm=b(w_br_ssm), w_out=b(w_out),
        g_ffn2=r(g_ffn2), w2_gu=b(w2_gu), w2_down=b(w2_down), g_ple=r(g_ple),
        w_ple=b(w_ple), w_ple_gate=b(w_ple_gate), g_final=g_final.reshape(1, -1).astype(F32),
    )


def kernel(x_prompt, x_sample, state_ret, state_ssm, state_conv, p_prompt, p_sample, g_ffn1, w1_gu, w1_down, g_mix, w_in, ret_gn_g, ret_gn_b, conv_w, conv_b, dt_bias, a_log, d_skip, ssm_norm_g, w_br_ret, w_br_ssm, w_out, g_ffn2, w2_gu, w2_down, g_ple, w_ple, w_ple_gate, g_final):
    assert g_ffn1.shape[0] == 1, "single-layer trunk"
    w = _prep_weights(g_ffn1, w1_gu, w1_down, g_mix, w_in, ret_gn_g, ret_gn_b, conv_w, conv_b, dt_bias, a_log,
                      d_skip, ssm_norm_g, w_br_ret, w_br_ssm, w_out, g_ffn2, w2_gu, w2_down, g_ple, w_ple,
                      w_ple_gate, g_final)
    y_p, ret_p, ssm_p, conv_p = _trunk(x_prompt, p_prompt[0], 0, None, None, None, w, CFG)
    y_s, ret_s, ssm_s, conv_s = _trunk(x_sample, p_sample[0], PAST_LEN, state_ret[0], state_ssm[0],
                                       state_conv[0], w, CFG)
    e = lambda a: a[None]
    return (y_p, y_s, e(ret_p), e(ssm_p), e(conv_p), e(ret_s), e(ssm_s), e(conv_s))
```

```python
import functools

import numpy as np
import jax
import jax.numpy as jnp
from jax import lax
from jax.experimental import pallas as pl
from jax.experimental.pallas import tpu as pltpu

F32 = jnp.float32
BF16 = jnp.bfloat16

D_MODEL = 2048
PAST_LEN = 4096
EPS = 1e-6
RET_HEADS = 8
RET_DK = 256
RET_DV = 512
RET_QK = RET_HEADS * RET_DK
RET_V = RET_HEADS * RET_DV
ROPE_THETA = 10000.0
SSM_INNER = 2 * D_MODEL
SSM_HEADDIM = 64
SSM_HEADS = SSM_INNER // SSM_HEADDIM
SSM_GROUPS = 8
SSM_HPG = SSM_HEADS // SSM_GROUPS
SSM_GW = SSM_HPG * SSM_HEADDIM
SSM_STATE = 128
SSM_CONV = 4
SSM_CONV_DIM = SSM_INNER + 2 * SSM_GROUPS * SSM_STATE
PLE_DIM = 256

COL_Q = 0
COL_K = COL_Q + RET_QK
COL_V = COL_K + RET_QK
COL_RG = COL_V + RET_V
COL_Z = COL_RG + RET_V
COL_XBC = COL_Z + SSM_INNER
N_BIG = COL_XBC + SSM_CONV_DIM
COL_DT = N_BIG
COL_GR = COL_DT + SSM_HEADS
COL_GS = COL_GR + D_MODEL

SUBLANES = 8
HALF_LANES = 64
VMEM_LIMIT = 56 * 1024 * 1024
ROW_CHUNK = 256


def _cparams(sem):
    return pltpu.CompilerParams(dimension_semantics=sem, vmem_limit_bytes=VMEM_LIMIT)


def _rms_bf16(x, g):
    ms = jnp.mean(x * x, axis=-1, keepdims=True)
    return (x * lax.rsqrt(ms + EPS) * g).astype(BF16)


def _silu(x):
    return x * jax.nn.sigmoid(x)


def _dot(a, b):
    return jnp.dot(a, b, preferred_element_type=F32)


def _dot_nt(a, b):
    return lax.dot_general(a, b, (((1,), (1,)), ((), ())), preferred_element_type=F32)


def _dot_tn(a, b):
    return lax.dot_general(a, b, (((0,), (0,)), ((), ())), preferred_element_type=F32)


def _ffn_kernel(x_ref, g_ref, gn_ref, wg_ref, wu_ref, wd_ref, o_ref, hn_ref, h_scr, *, nj):
    j = pl.program_id(1)

    @pl.when(j == 0)
    def _():
        h_scr[...] = _rms_bf16(x_ref[...], g_ref[...])
        o_ref[...] = jnp.zeros_like(o_ref)

    h = h_scr[...]
    gate = _dot(h, wg_ref[...])
    up = _dot(h, wu_ref[...])
    act = (_silu(gate) * up).astype(BF16)
    o_ref[...] += _dot(act, wd_ref[...])

    @pl.when(j == nj - 1)
    def _():
        xn = x_ref[...] + 0.5 * o_ref[...]
        o_ref[...] = xn
        hn_ref[...] = _rms_bf16(xn, gn_ref[...])


def _ffn(x, g, g_next, w_gu, w_down, *, tm, tf):
    n, d = x.shape
    f = w_down.shape[0]
    nj = f // tf
    return pl.pallas_call(
        functools.partial(_ffn_kernel, nj=nj),
        out_shape=(jax.ShapeDtypeStruct((n, d), F32), jax.ShapeDtypeStruct((n, d), BF16)),
        grid=(n // tm, nj),
        in_specs=[
            pl.BlockSpec((tm, d), lambda i, j: (i, 0)),
            pl.BlockSpec((1, d), lambda i, j: (0, 0)),
            pl.BlockSpec((1, d), lambda i, j: (0, 0)),
            pl.BlockSpec((d, tf), lambda i, j: (0, j)),
            pl.BlockSpec((d, tf), lambda i, j: (0, j + nj)),
            pl.BlockSpec((tf, d), lambda i, j: (j, 0)),
        ],
        out_specs=(pl.BlockSpec((tm, d), lambda i, j: (i, 0)),
                   pl.BlockSpec((tm, d), lambda i, j: (i, 0))),
        scratch_shapes=[pltpu.VMEM((tm, d), BF16)],
        compiler_params=_cparams(("parallel", "arbitrary")),
        name="ffn",
    )(x, g, g_next, w_gu, w_gu, w_down)


def _inproj_kernel(h_ref, w_ref, cos_ref, sin_ref, o_ref, *, tn, nq, nrope, rc):
    j = pl.program_id(1)
    nr = h_ref.shape[0] // rc

    @pl.when(j >= nrope)
    def _():
        for r in range(nr):
            rows = pl.ds(r * rc, rc)
            o_ref[rows, :] = _dot(h_ref[rows, :], w_ref[...]).astype(BF16)

    @pl.when(j < nrope)
    def _():
        scale = jnp.where(j >= nq, RET_DK ** -0.5, 1.0).astype(F32)
        half = RET_DK // 2
        for r in range(nr):
            rows = pl.ds(r * rc, rc)
            acc = _dot(h_ref[rows, :], w_ref[...])
            c = cos_ref[rows, :] * scale
            s = sin_ref[rows, :] * scale
            for hh in range(tn // RET_DK):
                lo = hh * RET_DK
                x1 = acc[:, lo:lo + half]
                x2 = acc[:, lo + half:lo + RET_DK]
                o_ref[rows, lo:lo + half] = (x1 * c - x2 * s).astype(BF16)
                o_ref[rows, lo + half:lo + RET_DK] = (x2 * c + x1 * s).astype(BF16)


def _inproj(h, w_in, cos, sin, *, tm, tn):
    n, d = h.shape
    nrow = cos.shape[0] // tm
    return pl.pallas_call(
        functools.partial(_inproj_kernel, tn=tn, nq=RET_QK // tn, nrope=2 * RET_QK // tn, rc=min(ROW_CHUNK, tm)),
        out_shape=jax.ShapeDtypeStruct((n, N_BIG), BF16),
        grid=(n // tm, N_BIG // tn),
        in_specs=[
            pl.BlockSpec((tm, d), lambda i, j: (i, 0)),
            pl.BlockSpec((d, tn), lambda i, j: (0, j)),
            pl.BlockSpec((tm, RET_DK // 2), lambda i, j: (i % nrow, 0)),
            pl.BlockSpec((tm, RET_DK // 2), lambda i, j: (i % nrow, 0)),
        ],
        out_specs=pl.BlockSpec((tm, tn), lambda i, j: (i, j)),
        compiler_params=_cparams(("parallel", "arbitrary")),
        name="inproj",
    )(h, w_in, cos, sin)


def _softplus(x):
    return jnp.maximum(x, 0.0) + jnp.log1p(jnp.exp(-jnp.abs(x)))


def _dt_kernel(h_ref, w_ref, wt_ref, br_ref, bc_ref, ar_ref, ac_ref,
               acc_ref, dtr_ref, acr_ref, *, q, seg):
    h = h_ref[...]
    tm = h.shape[0]
    raw_c = _dot(h, w_ref[...])
    raw_r = _dot_nt(wt_ref[...], h)
    dt_c = _softplus(raw_c + br_ref[...])
    dt_r = _softplus(raw_r + bc_ref[...])
    dta_c = dt_c * (-jnp.exp(ar_ref[...]))
    dta_r = dt_r * (-jnp.exp(ac_ref[...]))
    i = lax.broadcasted_iota(jnp.int32, (tm, tm), 0)
    j = lax.broadcasted_iota(jnp.int32, (tm, tm), 1)
    same = (i // seg) == (j // seg)
    tri = jnp.where(same & (j <= i), 1.0, 0.0).astype(F32)
    trit = jnp.where(same & (i <= j), 1.0, 0.0).astype(F32)
    ac_c = jnp.dot(tri, dta_c, preferred_element_type=F32, precision=lax.Precision.HIGHEST)
    ac_r = jnp.dot(dta_r, trit, preferred_element_type=F32, precision=lax.Precision.HIGHEST)
    for g in range(SSM_GROUPS):
        lo = g * SSM_HPG
        acc_ref[g] = ac_c[:, lo:lo + SSM_HPG]
        for ci in range(tm // q):
            dtr_ref[g, ci] = dt_r[lo:lo + SSM_HPG, ci * q:(ci + 1) * q]
            acr_ref[g, ci] = ac_r[lo:lo + SSM_HPG, ci * q:(ci + 1) * q]


def _dt_proj(h, w_dt, w_dt_t, dt_bias, a_log, *, tm, q, seg):
    n, d = h.shape
    hh = SSM_HEADS
    col = jax.ShapeDtypeStruct((SSM_GROUPS, n, SSM_HPG), F32)
    row = jax.ShapeDtypeStruct((SSM_GROUPS, n // q, SSM_HPG, q), F32)
    return pl.pallas_call(
        functools.partial(_dt_kernel, q=q, seg=seg),
        out_shape=(col, row, row),
        grid=(n // tm,),
        in_specs=[
            pl.BlockSpec((tm, d), lambda i: (i, 0)),
            pl.BlockSpec((d, hh), lambda i: (0, 0)),
            pl.BlockSpec((hh, d), lambda i: (0, 0)),
            pl.BlockSpec((1, hh), lambda i: (0, 0)),
            pl.BlockSpec((hh, 1), lambda i: (0, 0)),
            pl.BlockSpec((1, hh), lambda i: (0, 0)),
            pl.BlockSpec((hh, 1), lambda i: (0, 0)),
        ],
        out_specs=(
            pl.BlockSpec((SSM_GROUPS, tm, SSM_HPG), lambda i: (0, i, 0)),
            pl.BlockSpec((SSM_GROUPS, tm // q, SSM_HPG, q), lambda i: (0, i, 0, 0)),
            pl.BlockSpec((SSM_GROUPS, tm // q, SSM_HPG, q), lambda i: (0, i, 0, 0)),
        ),
        compiler_params=_cparams(("parallel",)),
        name="dt_proj",
    )(h, w_dt, w_dt_t, dt_bias.reshape(1, hh), dt_bias.reshape(hh, 1),
      a_log.reshape(1, hh), a_log.reshape(hh, 1))


def _ret_kernel(*refs, c, seg, nck, has_s0):
    if has_s0:
        (q_ref, k_ref, v_ref, rg_ref, dm_ref, qd_ref, kd_ref, cd_ref, gg_ref, gb_ref, s0_ref,
         o_ref, s_ref) = refs
    else:
        (q_ref, k_ref, v_ref, rg_ref, dm_ref, qd_ref, kd_ref, cd_ref, gg_ref, gb_ref,
         o_ref, s_ref) = refs
    t = pl.program_id(2)
    spc = c // seg

    @pl.when(t == 0)
    def _():
        if has_s0:
            s_ref[...] = s0_ref[...]
        else:
            s_ref[...] = jnp.zeros_like(s_ref)

    dm = dm_ref[0]
    qd = qd_ref[0]
    cd = cd_ref[0]
    gg = gg_ref[...]
    gb = gb_ref[...]
    for ci in range(nck):
        sl = pl.ds(ci * c, c)
        q = q_ref[sl, :]
        k = k_ref[sl, :]
        v = v_ref[sl, :]
        att = _dot_nt(q, k) * dm
        o = _dot(att.astype(BF16), v)
        kf = k.astype(F32)
        inter = []
        for slot in range(spc):
            sidx = (ci * spc + slot) if spc > 1 else 0
            st = s_ref[sidx, 0]
            qs = q if spc == 1 else q[slot * seg:(slot + 1) * seg, :]
            inter.append(_dot(qs, st.astype(BF16)))
            s_ref[sidx, 0] = st * cd + _dot_tn((kf * kd_ref[0, slot]).astype(BF16), v)
        o = o + (inter[0] if spc == 1 else jnp.concatenate(inter, axis=0)) * qd
        mu = jnp.mean(o, axis=-1, keepdims=True)
        dev = o - mu
        var = jnp.mean(dev * dev, axis=-1, keepdims=True)
        on = dev * lax.rsqrt(var + EPS)
        rg = rg_ref[sl, :].astype(F32)
        o_ref[sl, :] = (_silu(rg) * (on * gg + gb)).astype(BF16)


def _ret_tables(c, seg):
    spc = c // seg
    hs = np.arange(RET_HEADS, dtype=np.float64)
    log_g = np.log1p(-np.exp2(-5.0 - hs))
    idx = np.arange(c)
    pos = (idx % seg).astype(np.float64)
    slot = idx // seg
    diff = pos[:, None] - pos[None, :]
    live = (slot[:, None] == slot[None, :]) & (diff >= 0)
    dmat = np.where(live[None], np.exp(np.maximum(diff, 0.0)[None] * log_g[:, None, None]), 0.0)
    q_dec = np.exp((pos[None, :] + 1.0) * log_g[:, None])
    k_dec = np.exp((seg - 1.0 - pos[None, :]) * log_g[:, None])
    c_dec = np.exp(seg * log_g)
    k_slot = np.where(slot[None, None, :] == np.arange(spc)[None, :, None], k_dec[:, None, :], 0.0)
    qd = np.broadcast_to(q_dec[:, :, None], (RET_HEADS, c, RET_DV))
    kd = np.broadcast_to(k_slot[:, :, :, None], (RET_HEADS, spc, c, RET_DK))
    cd = np.broadcast_to(c_dec[:, None, None], (RET_HEADS, 1, RET_DV))
    f = lambda a: jnp.asarray(np.ascontiguousarray(a), dtype=F32)
    return f(dmat), f(qd), f(kd), f(cd)


def _retention(big, gn_g, gn_b, s0, *, nseq, seqlen, c, seg, nck):
    n = big.shape[0]
    spc = c // seg
    rb = nck * c
    ns_step = spc * nck if spc > 1 else 1
    nt = 1 if spc > 1 else seqlen // rb
    dm, qd, kd, cd = _ret_tables(c, seg)
    has_s0 = s0 is not None
    kq, kk = COL_Q // RET_DK, COL_K // RET_DK
    kv, kr = COL_V // RET_DV, COL_RG // RET_DV
    row = lambda b, h, t: b * nt + t
    in_specs = [
        pl.BlockSpec((rb, RET_DK), lambda b, h, t: (row(b, h, t), kq + h)),
        pl.BlockSpec((rb, RET_DK), lambda b, h, t: (row(b, h, t), kk + h)),
        pl.BlockSpec((rb, RET_DV), lambda b, h, t: (row(b, h, t), kv + h)),
        pl.BlockSpec((rb, RET_DV), lambda b, h, t: (row(b, h, t), kr + h)),
        pl.BlockSpec((1, c, c), lambda b, h, t: (h, 0, 0)),
        pl.BlockSpec((1, c, RET_DV), lambda b, h, t: (h, 0, 0)),
        pl.BlockSpec((1, spc, c, RET_DK), lambda b, h, t: (h, 0, 0, 0)),
        pl.BlockSpec((1, 1, RET_DV), lambda b, h, t: (h, 0, 0)),
        pl.BlockSpec((1, RET_DV), lambda b, h, t: (0, h)),
        pl.BlockSpec((1, RET_DV), lambda b, h, t: (0, h)),
    ]
    args = [big, big, big, big, dm, qd, kd, cd, gn_g.reshape(1, RET_V), gn_b.reshape(1, RET_V)]
    st_spec = pl.BlockSpec((ns_step, 1, RET_DK, RET_DV), lambda b, h, t: (b, h, 0, 0))
    if has_s0:
        in_specs.append(st_spec)
        args.append(s0)
    return pl.pallas_call(
        functools.partial(_ret_kernel, c=c, seg=seg, nck=nck, has_s0=has_s0),
        out_shape=(jax.ShapeDtypeStruct((n, RET_V), BF16),
                   jax.ShapeDtypeStruct((nseq, RET_HEADS, RET_DK, RET_DV), F32)),
        grid=(nseq // ns_step, RET_HEADS, nt),
        in_specs=in_specs,
        out_specs=(pl.BlockSpec((rb, RET_DV), lambda b, h, t: (row(b, h, t), h)), st_spec),
        compiler_params=_cparams(("parallel", "parallel", "arbitrary")),
        name="retention",
    )(*args)


def _conv_silu(raw_scr, cw, bias, tb):
    taps = SSM_CONV
    acc = bias
    for s in range(taps):
        acc = acc + raw_scr[pl.ds(SUBLANES - s, tb), :] * cw[taps - 1 - s:taps - s, :]
    return _silu(acc)


def _bf16_terms(v):
    t1 = v.astype(BF16).astype(F32)
    r1 = v - t1
    t2 = r1.astype(BF16).astype(F32)
    t3 = (r1 - t2).astype(BF16).astype(F32)
    return t1, t2, t3


def _expand_matrix(heads, width):
    k = lax.broadcasted_iota(jnp.int32, (3 * heads, heads * width), 0) % heads
    c = lax.broadcasted_iota(jnp.int32, (3 * heads, heads * width), 1) // width
    return jnp.where(k == c, 1.0, 0.0).astype(F32)


def _head_expand(v, expand):
    return _dot_tn(jnp.concatenate(_bf16_terms(v), axis=0), expand)


def _ssd_kernel(*refs, q, seg, nck, has_s0):
    if has_s0:
        (xs_ref, b_ref, c_ref, z_ref, acc_ref, dtr_ref, acr_ref,
         cwx_ref, cwb_ref, cwc_ref, cbx_ref, cbb_ref, cbc_ref, dsk_ref, ng_ref,
         s0_ref, hx0_ref, hb0_ref, hc0_ref,
         y_ref, so_ref, hx, hb, hc, st_scr) = refs
        hist0 = (hx0_ref, hb0_ref, hc0_ref)
    else:
        (xs_ref, b_ref, c_ref, z_ref, acc_ref, dtr_ref, acr_ref,
         cwx_ref, cwb_ref, cwc_ref, cbx_ref, cbb_ref, cbc_ref, dsk_ref, ng_ref,
         y_ref, so_ref, hx, hb, hc, st_scr) = refs
        hist0 = (None, None, None)
    t = pl.program_id(2)
    spc = q // seg
    ns_step = st_scr.shape[0]
    conv_in = ((xs_ref, hx, hist0[0], cwx_ref, cbx_ref),
               (b_ref, hb, hist0[1], cwb_ref, cbb_ref),
               (c_ref, hc, hist0[2], cwc_ref, cbc_ref))

    def first_rows(raw_scr, h0_ref, sidx):
        if has_s0:
            raw_scr[0:SUBLANES, :] = h0_ref[sidx]
        else:
            raw_scr[0:SUBLANES, :] = jnp.zeros((SUBLANES, raw_scr.shape[1]), F32)

    @pl.when(t == 0)
    def _():
        for sidx in range(ns_step):
            if has_s0:
                st_scr[sidx] = s0_ref[sidx, 0].T
            else:
                st_scr[sidx] = jnp.zeros(st_scr.shape[1:], F32)
        if spc == 1:
            for _, raw_scr, h0_ref, _, _ in conv_in:
                first_rows(raw_scr, h0_ref, 0)

    ii = lax.broadcasted_iota(jnp.int32, (q, q), 0)
    jj = lax.broadcasted_iota(jnp.int32, (q, q), 1)
    causal = (ii >= jj) if spc == 1 else ((ii >= jj) & ((ii // seg) == (jj // seg)))
    lane = lax.broadcasted_iota(jnp.int32, (q, 2 * HALF_LANES), 1)
    row_slot = lax.broadcasted_iota(jnp.int32, (q, SSM_STATE), 0) // seg
    tok = lax.broadcasted_iota(jnp.int32, (SSM_HPG, q), 1)
    dsk = dsk_ref[...]
    ng = ng_ref[...]
    expand = _expand_matrix(SSM_HPG, SSM_HEADDIM)

    def conv_chunk(ci, sl):
        out = []
        for raw_ref, raw_scr, h0_ref, cw_ref, cb_ref in conv_in:
            if spc == 1:
                raw_scr[SUBLANES:SUBLANES + q, :] = raw_ref[sl, :].astype(F32)
                out.append(_conv_silu(raw_scr, cw_ref[...], cb_ref[...], q))
                raw_scr[0:SUBLANES, :] = raw_scr[q:q + SUBLANES, :]
            else:
                pieces = []
                for slot in range(spc):
                    first_rows(raw_scr, h0_ref, ci * spc + slot)
                    raw_scr[SUBLANES:SUBLANES + seg, :] = raw_ref[pl.ds(ci * q + slot * seg, seg), :].astype(F32)
                    pieces.append(_conv_silu(raw_scr, cw_ref[...], cb_ref[...], seg))
                out.append(jnp.concatenate(pieces, axis=0))
        return out

    def chunk(ci):
        sl = pl.ds(pl.multiple_of(ci * q, q), q)
        x, b_f32, c_f32 = conv_chunk(ci, sl)
        bq = b_f32.astype(BF16)
        cq = c_f32.astype(BF16)
        dtr = dtr_ref[0, ci]
        acr = acr_ref[0, ci]
        acc = acc_ref[0, sl, :]

        cb = jnp.where(causal, _dot_nt(cq, bq), 0.0)
        pairs = []
        for m in range(SSM_HPG // 2):
            ws = []
            for r in (2 * m, 2 * m + 1):
                seg_sum = acc[:, r:r + 1] - acr[r:r + 1, :]
                ws.append((jnp.exp(jnp.minimum(seg_sum, 0.0)) * (cb * dtr[r:r + 1, :])).astype(BF16))
            xp = x[:, m * 2 * HALF_LANES:(m + 1) * 2 * HALF_LANES]
            x_lo = jnp.where(lane < HALF_LANES, xp, 0.0).astype(BF16)
            x_hi = jnp.where(lane >= HALF_LANES, xp, 0.0).astype(BF16)
            pairs.append(_dot(jnp.concatenate(ws, axis=1), jnp.concatenate([x_lo, x_hi], axis=0)))
        y = jnp.concatenate(pairs, axis=1)

        a_last = acr[:, q - 1:q]
        for slot in range(spc - 2, -1, -1):
            end = (slot + 1) * seg
            a_last = jnp.where(tok < end, acr[:, end - 1:end], a_last)
        ea = _head_expand(jnp.exp(acr), expand)
        xw = (x * _head_expand(jnp.exp(a_last - acr) * dtr, expand)).astype(BF16)
        inter = []
        for slot in range(spc):
            sidx = (ci * spc + slot) if spc > 1 else 0
            state = st_scr[sidx]
            end = (slot + 1) * seg
            cs = cq if spc == 1 else cq[slot * seg:end, :]
            bs = bq if spc == 1 else jnp.where(row_slot == slot, b_f32, 0.0).astype(BF16)
            inter.append(_dot(cs, state.astype(BF16)))
            st_scr[sidx] = state * ea[end - 1:end, :] + _dot_tn(bs, xw)
        y = y + (inter[0] if spc == 1 else jnp.concatenate(inter, axis=0)) * ea

        y = y + dsk * x
        y = y * _silu(z_ref[sl, :].astype(F32))
        ms = jnp.mean(y * y, axis=-1, keepdims=True)
        y_ref[sl, :] = (y * lax.rsqrt(ms + EPS) * ng).astype(BF16)

    if spc > 1 or nck == 1:
        for ci in range(nck):
            chunk(ci)
    else:
        lax.fori_loop(0, nck, lambda ci, carry: (chunk(ci), carry)[1], 0)

    @pl.when(t == pl.num_programs(2) - 1)
    def _():
        for sidx in range(ns_step):
            so_ref[sidx, 0] = st_scr[sidx].T


def _ssd(big, dts, conv_w, conv_b, d_skip_x, norm_g, s0, hist0, *, nseq, seqlen, q, seg, nck):
    n = big.shape[0]
    spc = q // seg
    rb = nck * q
    ns_step = spc * nck if spc > 1 else 1
    nt = 1 if spc > 1 else seqlen // rb
    has_s0 = s0 is not None
    acc, dtr, acr = dts
    gw, ns = SSM_GW, SSM_STATE
    kz, kx = COL_Z // gw, COL_XBC // gw
    kb = (COL_XBC + SSM_INNER) // ns
    kc = kb + SSM_GROUPS
    cb_b = SSM_INNER // ns
    cb_c = cb_b + SSM_GROUPS
    row = lambda b, g, t: b * nt + t
    in_specs = [
        pl.BlockSpec((rb, gw), lambda b, g, t: (row(b, g, t), kx + g)),
        pl.BlockSpec((rb, ns), lambda b, g, t: (row(b, g, t), kb + g)),
        pl.BlockSpec((rb, ns), lambda b, g, t: (row(b, g, t), kc + g)),
        pl.BlockSpec((rb, gw), lambda b, g, t: (row(b, g, t), kz + g)),
        pl.BlockSpec((1, rb, SSM_HPG), lambda b, g, t: (g, row(b, g, t), 0)),
        pl.BlockSpec((1, nck, SSM_HPG, q), lambda b, g, t: (g, row(b, g, t), 0, 0)),
        pl.BlockSpec((1, nck, SSM_HPG, q), lambda b, g, t: (g, row(b, g, t), 0, 0)),
        pl.BlockSpec((SSM_CONV, gw), lambda b, g, t: (0, g)),
        pl.BlockSpec((SSM_CONV, ns), lambda b, g, t: (0, cb_b + g)),
        pl.BlockSpec((SSM_CONV, ns), lambda b, g, t: (0, cb_c + g)),
        pl.BlockSpec((1, gw), lambda b, g, t: (0, g)),
        pl.BlockSpec((1, ns), lambda b, g, t: (0, cb_b + g)),
        pl.BlockSpec((1, ns), lambda b, g, t: (0, cb_c + g)),
        pl.BlockSpec((1, gw), lambda b, g, t: (0, g)),
        pl.BlockSpec((1, gw), lambda b, g, t: (0, g)),
    ]
    cbias = conv_b.reshape(1, SSM_CONV_DIM)
    args = [big, big, big, big, acc, dtr, acr, conv_w, conv_w, conv_w, cbias, cbias, cbias,
            d_skip_x, norm_g.reshape(1, SSM_INNER)]
    st_spec = pl.BlockSpec((ns_step, 1, gw, ns), lambda b, g, t: (b, g, 0, 0))
    if has_s0:
        in_specs += [
            st_spec,
            pl.BlockSpec((ns_step, SUBLANES, gw), lambda b, g, t: (b, 0, g)),
            pl.BlockSpec((ns_step, SUBLANES, ns), lambda b, g, t: (b, 0, cb_b + g)),
            pl.BlockSpec((ns_step, SUBLANES, ns), lambda b, g, t: (b, 0, cb_c + g)),
        ]
        args += [s0, hist0, hist0, hist0]
    return pl.pallas_call(
        functools.partial(_ssd_kernel, q=q, seg=seg, nck=nck, has_s0=has_s0),
        out_shape=(jax.ShapeDtypeStruct((n, SSM_INNER), BF16),
                   jax.ShapeDtypeStruct((nseq, SSM_GROUPS, gw, ns), F32)),
        grid=(nseq // ns_step, SSM_GROUPS, nt),
        in_specs=in_specs,
        out_specs=(pl.BlockSpec((rb, gw), lambda b, g, t: (row(b, g, t), g)), st_spec),
        scratch_shapes=[pltpu.VMEM((SUBLANES + q, gw), F32), pltpu.VMEM((SUBLANES + q, ns), F32),
                        pltpu.VMEM((SUBLANES + q, ns), F32), pltpu.VMEM((ns_step, ns, gw), F32)],
        compiler_params=_cparams(("parallel", "parallel", "arbitrary")),
        name="ssd",
    )(*args)


def _merge_kernel(h_ref, ret_ref, ssm_ref, wgr_ref, wgs_ref, wr_ref, ws_ref, o_ref):
    h = h_ref[...]
    gr = jax.nn.sigmoid(_dot(h, wgr_ref[...]))
    gs = jax.nn.sigmoid(_dot(h, wgs_ref[...]))
    a = _dot(ret_ref[...], wr_ref[...])
    b = _dot(ssm_ref[...], ws_ref[...])
    o_ref[...] = (gr * a + gs * b).astype(BF16)


def _merge(h, ret, ssm, w_gr, w_gs, w_r, w_s, *, tm, tn):
    n, d = h.shape
    kr, ks = ret.shape[1], ssm.shape[1]
    return pl.pallas_call(
        _merge_kernel,
        out_shape=jax.ShapeDtypeStruct((n, d), BF16),
        grid=(n // tm, d // tn),
        in_specs=[
            pl.BlockSpec((tm, d), lambda i, j: (i, 0)),
            pl.BlockSpec((tm, kr), lambda i, j: (i, 0)),
            pl.BlockSpec((tm, ks), lambda i, j: (i, 0)),
            pl.BlockSpec((d, tn), lambda i, j: (0, j)),
            pl.BlockSpec((d, tn), lambda i, j: (0, j)),
            pl.BlockSpec((kr, tn), lambda i, j: (0, j)),
            pl.BlockSpec((ks, tn), lambda i, j: (0, j)),
        ],
        out_specs=pl.BlockSpec((tm, tn), lambda i, j: (i, j)),
        compiler_params=_cparams(("parallel", "arbitrary")),
        name="merge",
    )(h, ret, ssm, w_gr, w_gs, w_r, w_s)


def _outproj_kernel(m_ref, w_ref, x_ref, o_ref, *, rc):
    for r in range(m_ref.shape[0] // rc):
        rows = pl.ds(r * rc, rc)
        o_ref[rows, :] = x_ref[rows, :] + _dot(m_ref[rows, :], w_ref[...])


def _outproj(m, w, x, *, tm, tn):
    n, d = x.shape
    k = m.shape[1]
    return pl.pallas_call(
        functools.partial(_outproj_kernel, rc=min(ROW_CHUNK, tm)),
        out_shape=jax.ShapeDtypeStruct((n, d), F32),
        grid=(n // tm, d // tn),
        in_specs=[
            pl.BlockSpec((tm, k), lambda i, j: (i, 0)),
            pl.BlockSpec((k, tn), lambda i, j: (0, j)),
            pl.BlockSpec((tm, tn), lambda i, j: (i, j)),
        ],
        out_specs=pl.BlockSpec((tm, tn), lambda i, j: (i, j)),
        compiler_params=_cparams(("parallel", "arbitrary")),
        name="outproj",
    )(m, w, x)


def _ple_kernel(x_ref, h_ref, p_ref, wp_ref, wg_ref, gf_ref, o_ref):
    pe = _dot(p_ref[...].astype(BF16), wp_ref[...])
    gt = jax.nn.sigmoid(_dot(h_ref[...], wg_ref[...]))
    x = x_ref[...] + pe * gt
    ms = jnp.mean(x * x, axis=-1, keepdims=True)
    o_ref[...] = x * lax.rsqrt(ms + EPS) * gf_ref[...]


def _ple(x, h, p, w_ple, w_gate, g_final, *, tm):
    n, d = x.shape
    pd = p.shape[1]
    return pl.pallas_call(
        _ple_kernel,
        out_shape=jax.ShapeDtypeStruct((n, d), F32),
        grid=(n // tm,),
        in_specs=[
            pl.BlockSpec((tm, d), lambda i: (i, 0)),
            pl.BlockSpec((tm, d), lambda i: (i, 0)),
            pl.BlockSpec((tm, pd), lambda i: (i, 0)),
            pl.BlockSpec((pd, d), lambda i: (0, 0)),
            pl.BlockSpec((d, d), lambda i: (0, 0)),
            pl.BlockSpec((1, d), lambda i: (0, 0)),
        ],
        out_specs=pl.BlockSpec((tm, d), lambda i: (i, 0)),
        compiler_params=_cparams(("parallel",)),
        name="ple",
    )(x, h, p, w_ple, w_gate, g_final)


def _rope_tables(pos0, seqlen, rows):
    half = RET_DK // 2
    inv = ROPE_THETA ** (-jnp.arange(half, dtype=F32) / half)
    pos = (pos0 + jnp.arange(seqlen, dtype=jnp.int32)).astype(F32)
    ang = pos[:, None] * inv[None, :]
    reps = max(rows // seqlen, 1)
    return jnp.tile(jnp.cos(ang), (reps, 1)), jnp.tile(jnp.sin(ang), (reps, 1))


def _pick(n, pref):
    t = pref
    while n % t:
        t //= 2
    return t


def _trunk(x, p, pos0, s_ret, s_ssm, s_conv, w, cfg):
    nseq, seqlen, d = x.shape
    n = nseq * seqlen
    x = x.reshape(n, d)
    p = p.reshape(n, p.shape[-1])
    tm = _pick(n, cfg["tm"])

    x1, h = _ffn(x, w["g_ffn1"], w["g_mix"], w["w1_gu"], w["w1_down"], tm=_pick(n, cfg["tm_ffn"]), tf=cfg["tf"])

    cos, sin = _rope_tables(pos0, seqlen, tm)
    big = _inproj(h, w["w_in"], cos, sin, tm=tm, tn=cfg["tn_in"])

    q = cfg["q_ssd"]
    if seqlen % q == 0:
        seg, nck_ssd = q, _pick(seqlen, cfg["tb_ssd"]) // q
        c = cfg["c_ret"] if seqlen % cfg["c_ret"] == 0 else q
        seg_ret, nck_ret = c, _pick(seqlen, cfg["tb_ret"]) // c
    else:
        assert q % seqlen == 0 and n % q == 0, "short sequences must pack into whole chunks"
        c, seg, seg_ret = q, seqlen, seqlen
        nck_ssd = nck_ret = _pick(n // q, cfg["pack_chunks"])

    dts = _dt_proj(h, w["w_dt"], w["w_dt_t"], w["dt_bias"], w["a_log"], tm=_pick(n, 512), q=q, seg=seg)
    ret, s_ret_new = _retention(big, w["ret_gn_g"], w["ret_gn_b"], s_ret, nseq=nseq, seqlen=seqlen,
                                c=c, seg=seg_ret, nck=nck_ret)

    if s_ssm is not None:
        s0 = s_ssm.reshape(nseq, SSM_GROUPS, SSM_GW, SSM_STATE)
        hist0 = jnp.pad(s_conv, ((0, 0), (SUBLANES - (SSM_CONV - 1), 0), (0, 0)))
    else:
        s0, hist0 = None, None
    ssm, s_ssm_new = _ssd(big, dts, w["conv_w"], w["conv_b"], w["d_skip_x"], w["ssm_norm_g"], s0, hist0,
                          nseq=nseq, seqlen=seqlen, q=q, seg=seg, nck=nck_ssd)
    s_ssm_new = s_ssm_new.reshape(nseq, SSM_HEADS, SSM_HEADDIM, SSM_STATE)

    keep = min(seqlen, SSM_CONV - 1)
    xbc_tail = big.reshape(nseq, seqlen, N_BIG)[:, seqlen - keep:, COL_XBC:].astype(F32)
    if keep < SSM_CONV - 1:
        prev = jnp.zeros((nseq, SSM_CONV - 1, SSM_CONV_DIM), F32) if s_conv is None else s_conv.astype(F32)
        xbc_tail = jnp.concatenate([prev, xbc_tail], axis=1)[:, -(SSM_CONV - 1):]

    merged = _merge(h, ret, ssm, w["w_gr"], w["w_gs"], w["w_br_ret"], w["w_br_ssm"], tm=_pick(n, cfg["tm_mg"]), tn=cfg["tn_mg"])
    x2 = _outproj(merged, w["w_out"], x1, tm=tm, tn=cfg["tn_out"])
    x3, h3 = _ffn(x2, w["g_ffn2"], w["g_ple"], w["w2_gu"], w["w2_down"], tm=_pick(n, cfg["tm_ffn"]), tf=cfg["tf"])
    y = _ple(x3, h3, p, w["w_ple"], w["w_ple_gate"], w["g_final"], tm=_pick(n, cfg["tm_ple"]))
    return y.reshape(nseq, seqlen, d), s_ret_new, s_ssm_new, xbc_tail


CFG = dict(tm=1024, tm_ffn=512, tf=512, tn_in=1024, q_ssd=128, c_ret=256, tb_ret=1024, tb_ssd=1024, pack_chunks=2,
           tm_mg=512, tn_mg=512, tn_out=1024, tm_ple=512)


def _prep_weights(g_ffn1, w1_gu, w1_down, g_mix, w_in, ret_gn_g, ret_gn_b, conv_w, conv_b, dt_bias, a_log,
                  d_skip, ssm_norm_g, w_br_ret, w_br_ssm, w_out, g_ffn2, w2_gu, w2_down, g_ple, w_ple,
                  w_ple_gate, g_final):
    b = lambda a: a[0].astype(BF16)
    r = lambda a: a[0].reshape(1, -1).astype(F32)
    win = w_in[0]
    w_dt = win[:, COL_DT:COL_DT + SSM_HEADS].astype(BF16)
    return dict(
        g_ffn1=r(g_ffn1), w1_gu=b(w1_gu), w1_down=b(w1_down), g_mix=r(g_mix),
        w_in=win.astype(BF16), w_dt=w_dt, w_dt_t=w_dt.T,
        w_gr=win[:, COL_GR:COL_GR + D_MODEL].astype(BF16), w_gs=win[:, COL_GS:COL_GS + D_MODEL].astype(BF16),
        ret_gn_g=ret_gn_g[0].astype(F32), ret_gn_b=ret_gn_b[0].astype(F32),
        conv_w=conv_w[0].astype(F32), conv_b=conv_b[0].astype(F32),
        dt_bias=dt_bias[0].astype(F32), a_log=a_log[0].astype(F32),
        d_skip_x=jnp.repeat(d_skip[0].astype(F32), SSM_HEADDIM).reshape(1, SSM_INNER),
        ssm_norm_g=ssm_norm_g[0].astype(F32),
        w_br_ret=b(w_br_ret), w_br_ssm=b(w_br_ssm), w_out=b(w_out),
        g_ffn2=r(g_ffn2), w2_gu=b(w2_gu), w2_down=b(w2_down), g_ple=r(g_ple),
        w_ple=b(w_ple), w_ple_gate=b(w_ple_gate), g_final=g_final.reshape(1, -1).astype(F32),
    )


def kernel(x_prompt, x_sample, state_ret, state_ssm, state_conv, p_prompt, p_sample, g_ffn1, w1_gu, w1_down, g_mix, w_in, ret_gn_g, ret_gn_b, conv_w, conv_b, dt_bias, a_log, d_skip, ssm_norm_g, w_br_ret, w_br_ssm, w_out, g_ffn2, w2_gu, w2_down, g_ple, w_ple, w_ple_gate, g_final):
    assert g_ffn1.shape[0] == 1, "single-layer trunk"
    w = _prep_weights(g_ffn1, w1_gu, w1_down, g_mix, w_in, ret_gn_g, ret_gn_b, conv_w, conv_b, dt_bias, a_log,
                      d_skip, ssm_norm_g, w_br_ret, w_br_ssm, w_out, g_ffn2, w2_gu, w2_down, g_ple, w_ple,
                      w_ple_gate, g_final)
    y_p, ret_p, ssm_p, conv_p = _trunk(x_prompt, p_prompt[0], 0, None, None, None, w, CFG)
    y_s, ret_s, ssm_s, conv_s = _trunk(x_sample, p_sample[0], PAST_LEN, state_ret[0], state_ssm[0],
                                       state_conv[0], w, CFG)
    e = lambda a: a[None]
    return (y_p, y_s, e(ret_p), e(ssm_p), e(conv_p), e(ret_s), e(ssm_s), e(conv_s))
```

```python
import functools

import numpy as np
import jax
import jax.numpy as jnp
from jax import lax
from jax.experimental import pallas as pl
from jax.experimental.pallas import tpu as pltpu

F32 = jnp.float32
BF16 = jnp.bfloat16

D_MODEL = 2048
PAST_LEN = 4096
EPS = 1e-6
RET_HEADS = 8
RET_DK = 256
RET_DV = 512
RET_QK = RET_HEADS * RET_DK
RET_V = RET_HEADS * RET_DV
ROPE_THETA = 10000.0
SSM_INNER = 2 * D_MODEL
SSM_HEADDIM = 64
SSM_HEADS = SSM_INNER // SSM_HEADDIM
SSM_GROUPS = 8
SSM_HPG = SSM_HEADS // SSM_GROUPS
SSM_GW = SSM_HPG * SSM_HEADDIM
SSM_STATE = 128
SSM_CONV = 4
SSM_CONV_DIM = SSM_INNER + 2 * SSM_GROUPS * SSM_STATE
PLE_DIM = 256

COL_Q = 0
COL_K = COL_Q + RET_QK
COL_V = COL_K + RET_QK
COL_RG = COL_V + RET_V
COL_Z = COL_RG + RET_V
COL_XBC = COL_Z + SSM_INNER
N_BIG = COL_XBC + SSM_CONV_DIM
COL_DT = N_BIG
COL_GR = COL_DT + SSM_HEADS
COL_GS = COL_GR + D_MODEL

SUBLANES = 8
HALF_LANES = 64
VMEM_LIMIT = 56 * 1024 * 1024
ROW_CHUNK = 256


def _cparams(sem):
    return pltpu.CompilerParams(dimension_semantics=sem, vmem_limit_bytes=VMEM_LIMIT)


def _rms_bf16(x, g):
    ms = jnp.mean(x * x, axis=-1, keepdims=True)
    return (x * lax.rsqrt(ms + EPS) * g).astype(BF16)


def _silu(x):
    return x * jax.nn.sigmoid(x)


def _dot(a, b):
    return jnp.dot(a, b, preferred_element_type=F32)


def _dot_nt(a, b):
    return lax.dot_general(a, b, (((1,), (1,)), ((), ())), preferred_element_type=F32)


def _dot_tn(a, b):
    return lax.dot_general(a, b, (((0,), (0,)), ((), ())), preferred_element_type=F32)


def _ffn_kernel(x_ref, g_ref, gn_ref, wg_ref, wu_ref, wd_ref, o_ref, hn_ref, h_scr, *, nj):
    j = pl.program_id(1)

    @pl.when(j == 0)
    def _():
        h_scr[...] = _rms_bf16(x_ref[...], g_ref[...])
        o_ref[...] = jnp.zeros_like(o_ref)

    h = h_scr[...]
    tf = wg_ref.shape[1]
    acts = []
    for lo in range(0, tf, tf // 2):
        gate = _dot(h, wg_ref[:, lo:lo + tf // 2])
        up = _dot(h, wu_ref[:, lo:lo + tf // 2])
        acts.append((_silu(gate) * up).astype(BF16))
    act = jnp.concatenate(acts, axis=1)
    d = o_ref.shape[1]
    for lo in range(0, d, d // 2):
        o_ref[:, lo:lo + d // 2] += _dot(act, wd_ref[:, lo:lo + d // 2])

    @pl.when(j == nj - 1)
    def _():
        xn = x_ref[...] + 0.5 * o_ref[...]
        o_ref[...] = xn
        hn_ref[...] = _rms_bf16(xn, gn_ref[...])


def _ffn(x, g, g_next, w_gu, w_down, *, tm, tf):
    n, d = x.shape
    f = w_down.shape[0]
    nj = f // tf
    return pl.pallas_call(
        functools.partial(_ffn_kernel, nj=nj),
        out_shape=(jax.ShapeDtypeStruct((n, d), F32), jax.ShapeDtypeStruct((n, d), BF16)),
        grid=(n // tm, nj),
        in_specs=[
            pl.BlockSpec((tm, d), lambda i, j: (i, 0)),
            pl.BlockSpec((1, d), lambda i, j: (0, 0)),
            pl.BlockSpec((1, d), lambda i, j: (0, 0)),
            pl.BlockSpec((d, tf), lambda i, j: (0, j)),
            pl.BlockSpec((d, tf), lambda i, j: (0, j + nj)),
            pl.BlockSpec((tf, d), lambda i, j: (j, 0)),
        ],
        out_specs=(pl.BlockSpec((tm, d), lambda i, j: (i, 0)),
                   pl.BlockSpec((tm, d), lambda i, j: (i, 0))),
        scratch_shapes=[pltpu.VMEM((tm, d), BF16)],
        compiler_params=_cparams(("parallel", "arbitrary")),
        name="ffn",
    )(x, g, g_next, w_gu, w_gu, w_down)


def _inproj_kernel(h_ref, w_ref, cos_ref, sin_ref, o_ref, *, tn, nq, nrope, rc):
    j = pl.program_id(1)
    nr = h_ref.shape[0] // rc

    @pl.when(j >= nrope)
    def _():
        for r in range(nr):
            rows = pl.ds(r * rc, rc)
            o_ref[rows, :] = _dot(h_ref[rows, :], w_ref[...]).astype(BF16)

    @pl.when(j < nrope)
    def _():
        scale = jnp.where(j >= nq, RET_DK ** -0.5, 1.0).astype(F32)
        half = RET_DK // 2
        for r in range(nr):
            rows = pl.ds(r * rc, rc)
            acc = _dot(h_ref[rows, :], w_ref[...])
            c = cos_ref[rows, :] * scale
            s = sin_ref[rows, :] * scale
            for hh in range(tn // RET_DK):
                lo = hh * RET_DK
                x1 = acc[:, lo:lo + half]
                x2 = acc[:, lo + half:lo + RET_DK]
                o_ref[rows, lo:lo + half] = (x1 * c - x2 * s).astype(BF16)
                o_ref[rows, lo + half:lo + RET_DK] = (x2 * c + x1 * s).astype(BF16)


def _inproj(h, w_in, cos, sin, *, tm, tn):
    n, d = h.shape
    nrow = cos.shape[0] // tm
    return pl.pallas_call(
        functools.partial(_inproj_kernel, tn=tn, nq=RET_QK // tn, nrope=2 * RET_QK // tn, rc=min(ROW_CHUNK, tm)),
        out_shape=jax.ShapeDtypeStruct((n, N_BIG), BF16),
        grid=(n // tm, N_BIG // tn),
        in_specs=[
            pl.BlockSpec((tm, d), lambda i, j: (i, 0)),
            pl.BlockSpec((d, tn), lambda i, j: (0, j)),
            pl.BlockSpec((tm, RET_DK // 2), lambda i, j: (i % nrow, 0)),
            pl.BlockSpec((tm, RET_DK // 2), lambda i, j: (i % nrow, 0)),
        ],
        out_specs=pl.BlockSpec((tm, tn), lambda i, j: (i, j)),
        compiler_params=_cparams(("parallel", "arbitrary")),
        name="inproj",
    )(h, w_in, cos, sin)


def _softplus(x):
    return jnp.maximum(x, 0.0) + jnp.log1p(jnp.exp(-jnp.abs(x)))


def _dt_kernel(h_ref, w_ref, wt_ref, br_ref, bc_ref, ar_ref, ac_ref,
               acc_ref, dtr_ref, acr_ref, *, q, seg):
    h = h_ref[...]
    tm = h.shape[0]
    raw_c = _dot(h, w_ref[...])
    raw_r = _dot_nt(wt_ref[...], h)
    dt_c = _softplus(raw_c + br_ref[...])
    dt_r = _softplus(raw_r + bc_ref[...])
    dta_c = dt_c * (-jnp.exp(ar_ref[...]))
    dta_r = dt_r * (-jnp.exp(ac_ref[...]))
    i = lax.broadcasted_iota(jnp.int32, (q, q), 0)
    j = lax.broadcasted_iota(jnp.int32, (q, q), 1)
    same = (i // seg) == (j // seg)
    tri = jnp.where(same & (j <= i), 1.0, 0.0).astype(F32)
    trit = jnp.where(same & (i <= j), 1.0, 0.0).astype(F32)
    for ci in range(tm // q):
        lo_r, hi_r = ci * q, (ci + 1) * q
        ac_c = jnp.dot(tri, dta_c[lo_r:hi_r, :], preferred_element_type=F32, precision=lax.Precision.HIGHEST)
        ac_r = jnp.dot(dta_r[:, lo_r:hi_r], trit, preferred_element_type=F32, precision=lax.Precision.HIGHEST)
        for g in range(SSM_GROUPS):
            lo = g * SSM_HPG
            acc_ref[g, lo_r:hi_r, :] = ac_c[:, lo:lo + SSM_HPG]
            dtr_ref[g, ci] = dt_r[lo:lo + SSM_HPG, lo_r:hi_r]
            acr_ref[g, ci] = ac_r[lo:lo + SSM_HPG, :]


def _dt_proj(h, w_dt, w_dt_t, dt_bias, a_log, *, tm, q, seg):
    n, d = h.shape
    hh = SSM_HEADS
    col = jax.ShapeDtypeStruct((SSM_GROUPS, n, SSM_HPG), F32)
    row = jax.ShapeDtypeStruct((SSM_GROUPS, n // q, SSM_HPG, q), F32)
    return pl.pallas_call(
        functools.partial(_dt_kernel, q=q, seg=seg),
        out_shape=(col, row, row),
        grid=(n // tm,),
        in_specs=[
            pl.BlockSpec((tm, d), lambda i: (i, 0)),
            pl.BlockSpec((d, hh), lambda i: (0, 0)),
            pl.BlockSpec((hh, d), lambda i: (0, 0)),
            pl.BlockSpec((1, hh), lambda i: (0, 0)),
            pl.BlockSpec((hh, 1), lambda i: (0, 0)),
            pl.BlockSpec((1, hh), lambda i: (0, 0)),
            pl.BlockSpec((hh, 1), lambda i: (0, 0)),
        ],
        out_specs=(
            pl.BlockSpec((SSM_GROUPS, tm, SSM_HPG), lambda i: (0, i, 0)),
            pl.BlockSpec((SSM_GROUPS, tm // q, SSM_HPG, q), lambda i: (0, i, 0, 0)),
            pl.BlockSpec((SSM_GROUPS, tm // q, SSM_HPG, q), lambda i: (0, i, 0, 0)),
        ),
        compiler_params=_cparams(("parallel",)),
        name="dt_proj",
    )(h, w_dt, w_dt_t, dt_bias.reshape(1, hh), dt_bias.reshape(hh, 1),
      a_log.reshape(1, hh), a_log.reshape(hh, 1))


def _ret_kernel(*refs, c, seg, nck, has_s0):
    if has_s0:
        (q_ref, k_ref, v_ref, rg_ref, dm_ref, qd_ref, kd_ref, cd_ref, gg_ref, gb_ref, s0_ref,
         o_ref, s_ref) = refs
    else:
        (q_ref, k_ref, v_ref, rg_ref, dm_ref, qd_ref, kd_ref, cd_ref, gg_ref, gb_ref,
         o_ref, s_ref) = refs
    t = pl.program_id(2)
    spc = c // seg

    @pl.when(t == 0)
    def _():
        if has_s0:
            s_ref[...] = s0_ref[...]
        else:
            s_ref[...] = jnp.zeros_like(s_ref)

    dm = dm_ref[0]
    qd = qd_ref[0]
    cd = cd_ref[0]
    gg = gg_ref[...]
    gb = gb_ref[...]
    for ci in range(nck):
        sl = pl.ds(ci * c, c)
        q = q_ref[sl, :]
        k = k_ref[sl, :]
        v = v_ref[sl, :]
        att = _dot_nt(q, k) * dm
        o = _dot(att.astype(BF16), v)
        kf = k.astype(F32)
        inter = []
        for slot in range(spc):
            sidx = (ci * spc + slot) if spc > 1 else 0
            st = s_ref[sidx, 0]
            qs = q if spc == 1 else q[slot * seg:(slot + 1) * seg, :]
            inter.append(_dot(qs, st.astype(BF16)))
            s_ref[sidx, 0] = st * cd + _dot_tn((kf * kd_ref[0, slot]).astype(BF16), v)
        o = o + (inter[0] if spc == 1 else jnp.concatenate(inter, axis=0)) * qd
        mu = jnp.mean(o, axis=-1, keepdims=True)
        dev = o - mu
        var = jnp.mean(dev * dev, axis=-1, keepdims=True)
        on = dev * lax.rsqrt(var + EPS)
        rg = rg_ref[sl, :].astype(F32)
        o_ref[sl, :] = (_silu(rg) * (on * gg + gb)).astype(BF16)


def _ret_tables(c, seg):
    spc = c // seg
    hs = np.arange(RET_HEADS, dtype=np.float64)
    log_g = np.log1p(-np.exp2(-5.0 - hs))
    idx = np.arange(c)
    pos = (idx % seg).astype(np.float64)
    slot = idx // seg
    diff = pos[:, None] - pos[None, :]
    live = (slot[:, None] == slot[None, :]) & (diff >= 0)
    dmat = np.where(live[None], np.exp(np.maximum(diff, 0.0)[None] * log_g[:, None, None]), 0.0)
    q_dec = np.exp((pos[None, :] + 1.0) * log_g[:, None])
    k_dec = np.exp((seg - 1.0 - pos[None, :]) * log_g[:, None])
    c_dec = np.exp(seg * log_g)
    k_slot = np.where(slot[None, None, :] == np.arange(spc)[None, :, None], k_dec[:, None, :], 0.0)
    qd = np.broadcast_to(q_dec[:, :, None], (RET_HEADS, c, RET_DV))
    kd = np.broadcast_to(k_slot[:, :, :, None], (RET_HEADS, spc, c, RET_DK))
    cd = np.broadcast_to(c_dec[:, None, None], (RET_HEADS, 1, RET_DV))
    f = lambda a: jnp.asarray(np.ascontiguousarray(a), dtype=F32)
    return f(dmat), f(qd), f(kd), f(cd)


def _retention(big, gn_g, gn_b, s0, *, nseq, seqlen, c, seg, nck):
    n = big.shape[0]
    spc = c // seg
    rb = nck * c
    ns_step = spc * nck if spc > 1 else 1
    nt = 1 if spc > 1 else seqlen // rb
    dm, qd, kd, cd = _ret_tables(c, seg)
    has_s0 = s0 is not None
    kq, kk = COL_Q // RET_DK, COL_K // RET_DK
    kv, kr = COL_V // RET_DV, COL_RG // RET_DV
    row = lambda b, h, t: b * nt + t
    in_specs = [
        pl.BlockSpec((rb, RET_DK), lambda b, h, t: (row(b, h, t), kq + h)),
        pl.BlockSpec((rb, RET_DK), lambda b, h, t: (row(b, h, t), kk + h)),
        pl.BlockSpec((rb, RET_DV), lambda b, h, t: (row(b, h, t), kv + h)),
        pl.BlockSpec((rb, RET_DV), lambda b, h, t: (row(b, h, t), kr + h)),
        pl.BlockSpec((1, c, c), lambda b, h, t: (h, 0, 0)),
        pl.BlockSpec((1, c, RET_DV), lambda b, h, t: (h, 0, 0)),
        pl.BlockSpec((1, spc, c, RET_DK), lambda b, h, t: (h, 0, 0, 0)),
        pl.BlockSpec((1, 1, RET_DV), lambda b, h, t: (h, 0, 0)),
        pl.BlockSpec((1, RET_DV), lambda b, h, t: (0, h)),
        pl.BlockSpec((1, RET_DV), lambda b, h, t: (0, h)),
    ]
    args = [big, big, big, big, dm, qd, kd, cd, gn_g.reshape(1, RET_V), gn_b.reshape(1, RET_V)]
    st_spec = pl.BlockSpec((ns_step, 1, RET_DK, RET_DV), lambda b, h, t: (b, h, 0, 0))
    if has_s0:
        in_specs.append(st_spec)
        args.append(s0)
    return pl.pallas_call(
        functools.partial(_ret_kernel, c=c, seg=seg, nck=nck, has_s0=has_s0),
        out_shape=(jax.ShapeDtypeStruct((n, RET_V), BF16),
                   jax.ShapeDtypeStruct((nseq, RET_HEADS, RET_DK, RET_DV), F32)),
        grid=(nseq // ns_step, RET_HEADS, nt),
        in_specs=in_specs,
        out_specs=(pl.BlockSpec((rb, RET_DV), lambda b, h, t: (row(b, h, t), h)), st_spec),
        compiler_params=_cparams(("parallel", "parallel", "arbitrary")),
        name="retention",
    )(*args)


def _conv_silu(raw_scr, cw, bias, tb):
    taps = SSM_CONV
    acc = bias
    for s in range(taps):
        acc = acc + raw_scr[pl.ds(SUBLANES - s, tb), :] * cw[taps - 1 - s:taps - s, :]
    return _silu(acc)


def _bf16_terms(v):
    t1 = v.astype(BF16).astype(F32)
    r1 = v - t1
    t2 = r1.astype(BF16).astype(F32)
    t3 = (r1 - t2).astype(BF16).astype(F32)
    return t1, t2, t3


def _expand_matrix(heads, width):
    k = lax.broadcasted_iota(jnp.int32, (3 * heads, heads * width), 0) % heads
    c = lax.broadcasted_iota(jnp.int32, (3 * heads, heads * width), 1) // width
    return jnp.where(k == c, 1.0, 0.0).astype(F32)


def _head_expand(v, expand):
    return _dot_tn(jnp.concatenate(_bf16_terms(v), axis=0), expand)


def _ssd_kernel(*refs, q, seg, nck, has_s0):
    if has_s0:
        (xs_ref, b_ref, c_ref, z_ref, acc_ref, dtr_ref, acr_ref,
         cwx_ref, cwb_ref, cwc_ref, cbx_ref, cbb_ref, cbc_ref, dsk_ref, ng_ref,
         s0_ref, hx0_ref, hb0_ref, hc0_ref,
         y_ref, so_ref, hx, hb, hc, st_scr) = refs
        hist0 = (hx0_ref, hb0_ref, hc0_ref)
    else:
        (xs_ref, b_ref, c_ref, z_ref, acc_ref, dtr_ref, acr_ref,
         cwx_ref, cwb_ref, cwc_ref, cbx_ref, cbb_ref, cbc_ref, dsk_ref, ng_ref,
         y_ref, so_ref, hx, hb, hc, st_scr) = refs
        hist0 = (None, None, None)
    t = pl.program_id(2)
    spc = q // seg
    ns_step = st_scr.shape[0]
    conv_in = ((xs_ref, hx, hist0[0], cwx_ref, cbx_ref),
               (b_ref, hb, hist0[1], cwb_ref, cbb_ref),
               (c_ref, hc, hist0[2], cwc_ref, cbc_ref))

    def first_rows(raw_scr, h0_ref, sidx):
        if has_s0:
            raw_scr[0:SUBLANES, :] = h0_ref[sidx]
        else:
            raw_scr[0:SUBLANES, :] = jnp.zeros((SUBLANES, raw_scr.shape[1]), F32)

    @pl.when(t == 0)
    def _():
        for sidx in range(ns_step):
            if has_s0:
                st_scr[sidx] = s0_ref[sidx, 0].T
            else:
                st_scr[sidx] = jnp.zeros(st_scr.shape[1:], F32)
        if spc == 1:
            for _, raw_scr, h0_ref, _, _ in conv_in:
                first_rows(raw_scr, h0_ref, 0)

    ii = lax.broadcasted_iota(jnp.int32, (q, q), 0)
    jj = lax.broadcasted_iota(jnp.int32, (q, q), 1)
    causal = (ii >= jj) if spc == 1 else ((ii >= jj) & ((ii // seg) == (jj // seg)))
    lane = lax.broadcasted_iota(jnp.int32, (q, 2 * HALF_LANES), 1)
    row_slot = lax.broadcasted_iota(jnp.int32, (q, SSM_STATE), 0) // seg
    tok = lax.broadcasted_iota(jnp.int32, (SSM_HPG, q), 1)
    dsk = dsk_ref[...]
    ng = ng_ref[...]
    expand = _expand_matrix(SSM_HPG, SSM_HEADDIM)

    def conv_chunk(ci, sl):
        out = []
        for raw_ref, raw_scr, h0_ref, cw_ref, cb_ref in conv_in:
            if spc == 1:
                raw_scr[SUBLANES:SUBLANES + q, :] = raw_ref[sl, :].astype(F32)
                out.append(_conv_silu(raw_scr, cw_ref[...], cb_ref[...], q))
                raw_scr[0:SUBLANES, :] = raw_scr[q:q + SUBLANES, :]
            else:
                pieces = []
                for slot in range(spc):
                    first_rows(raw_scr, h0_ref, ci * spc + slot)
                    raw_scr[SUBLANES:SUBLANES + seg, :] = raw_ref[pl.ds(ci * q + slot * seg, seg), :].astype(F32)
                    pieces.append(_conv_silu(raw_scr, cw_ref[...], cb_ref[...], seg))
                out.append(jnp.concatenate(pieces, axis=0))
        return out

    def chunk(ci):
        sl = pl.ds(pl.multiple_of(ci * q, q), q)
        x, b_f32, c_f32 = conv_chunk(ci, sl)
        bq = b_f32.astype(BF16)
        cq = c_f32.astype(BF16)
        dtr = dtr_ref[0, ci]
        acr = acr_ref[0, ci]
        acc = acc_ref[0, sl, :]

        cb = jnp.where(causal, _dot_nt(cq, bq), 0.0)
        pairs = []
        for m in range(SSM_HPG // 2):
            ws = []
            for r in (2 * m, 2 * m + 1):
                seg_sum = acc[:, r:r + 1] - acr[r:r + 1, :]
                ws.append((jnp.exp(jnp.minimum(seg_sum, 0.0)) * (cb * dtr[r:r + 1, :])).astype(BF16))
            xp = x[:, m * 2 * HALF_LANES:(m + 1) * 2 * HALF_LANES]
            x_lo = jnp.where(lane < HALF_LANES, xp, 0.0).astype(BF16)
            x_hi = jnp.where(lane >= HALF_LANES, xp, 0.0).astype(BF16)
            pairs.append(_dot(jnp.concatenate(ws, axis=1), jnp.concatenate([x_lo, x_hi], axis=0)))
        y = jnp.concatenate(pairs, axis=1)

        a_last = acr[:, q - 1:q]
        for slot in range(spc - 2, -1, -1):
            end = (slot + 1) * seg
            a_last = jnp.where(tok < end, acr[:, end - 1:end], a_last)
        ea = _head_expand(jnp.exp(acr), expand)
        xw = (x * _head_expand(jnp.exp(a_last - acr) * dtr, expand)).astype(BF16)
        inter = []
        for slot in range(spc):
            sidx = (ci * spc + slot) if spc > 1 else 0
            state = st_scr[sidx]
            end = (slot + 1) * seg
            cs = cq if spc == 1 else cq[slot * seg:end, :]
            bs = bq if spc == 1 else jnp.where(row_slot == slot, b_f32, 0.0).astype(BF16)
            inter.append(_dot(cs, state.astype(BF16)))
            st_scr[sidx] = state * ea[end - 1:end, :] + _dot_tn(bs, xw)
        y = y + (inter[0] if spc == 1 else jnp.concatenate(inter, axis=0)) * ea

        y = y + dsk * x
        y = y * _silu(z_ref[sl, :].astype(F32))
        ms = jnp.mean(y * y, axis=-1, keepdims=True)
        y_ref[sl, :] = (y * lax.rsqrt(ms + EPS) * ng).astype(BF16)

    if spc > 1 or nck == 1:
        for ci in range(nck):
            chunk(ci)
    else:
        lax.fori_loop(0, nck, lambda ci, carry: (chunk(ci), carry)[1], 0)

    @pl.when(t == pl.num_programs(2) - 1)
    def _():
        for sidx in range(ns_step):
            so_ref[sidx, 0] = st_scr[sidx].T


def _ssd(big, dts, conv_w, conv_b, d_skip_x, norm_g, s0, hist0, *, nseq, seqlen, q, seg, nck):
    n = big.shape[0]
    spc = q // seg
    rb = nck * q
    ns_step = spc * nck if spc > 1 else 1
    nt = 1 if spc > 1 else seqlen // rb
    has_s0 = s0 is not None
    acc, dtr, acr = dts
    gw, ns = SSM_GW, SSM_STATE
    kz, kx = COL_Z // gw, COL_XBC // gw
    kb = (COL_XBC + SSM_INNER) // ns
    kc = kb + SSM_GROUPS
    cb_b = SSM_INNER // ns
    cb_c = cb_b + SSM_GROUPS
    row = lambda b, g, t: b * nt + t
    in_specs = [
        pl.BlockSpec((rb, gw), lambda b, g, t: (row(b, g, t), kx + g)),
        pl.BlockSpec((rb, ns), lambda b, g, t: (row(b, g, t), kb + g)),
        pl.BlockSpec((rb, ns), lambda b, g, t: (row(b, g, t), kc + g)),
        pl.BlockSpec((rb, gw), lambda b, g, t: (row(b, g, t), kz + g)),
        pl.BlockSpec((1, rb, SSM_HPG), lambda b, g, t: (g, row(b, g, t), 0)),
        pl.BlockSpec((1, nck, SSM_HPG, q), lambda b, g, t: (g, row(b, g, t), 0, 0)),
        pl.BlockSpec((1, nck, SSM_HPG, q), lambda b, g, t: (g, row(b, g, t), 0, 0)),
        pl.BlockSpec((SSM_CONV, gw), lambda b, g, t: (0, g)),
        pl.BlockSpec((SSM_CONV, ns), lambda b, g, t: (0, cb_b + g)),
        pl.BlockSpec((SSM_CONV, ns), lambda b, g, t: (0, cb_c + g)),
        pl.BlockSpec((1, gw), lambda b, g, t: (0, g)),
        pl.BlockSpec((1, ns), lambda b, g, t: (0, cb_b + g)),
        pl.BlockSpec((1, ns), lambda b, g, t: (0, cb_c + g)),
        pl.BlockSpec((1, gw), lambda b, g, t: (0, g)),
        pl.BlockSpec((1, gw), lambda b, g, t: (0, g)),
    ]
    cbias = conv_b.reshape(1, SSM_CONV_DIM)
    args = [big, big, big, big, acc, dtr, acr, conv_w, conv_w, conv_w, cbias, cbias, cbias,
            d_skip_x, norm_g.reshape(1, SSM_INNER)]
    st_spec = pl.BlockSpec((ns_step, 1, gw, ns), lambda b, g, t: (b, g, 0, 0))
    if has_s0:
        in_specs += [
            st_spec,
            pl.BlockSpec((ns_step, SUBLANES, gw), lambda b, g, t: (b, 0, g)),
            pl.BlockSpec((ns_step, SUBLANES, ns), lambda b, g, t: (b, 0, cb_b + g)),
            pl.BlockSpec((ns_step, SUBLANES, ns), lambda b, g, t: (b, 0, cb_c + g)),
        ]
        args += [s0, hist0, hist0, hist0]
    return pl.pallas_call(
        functools.partial(_ssd_kernel, q=q, seg=seg, nck=nck, has_s0=has_s0),
        out_shape=(jax.ShapeDtypeStruct((n, SSM_INNER), BF16),
                   jax.ShapeDtypeStruct((nseq, SSM_GROUPS, gw, ns), F32)),
        grid=(nseq // ns_step, SSM_GROUPS, nt),
        in_specs=in_specs,
        out_specs=(pl.BlockSpec((rb, gw), lambda b, g, t: (row(b, g, t), g)), st_spec),
        scratch_shapes=[pltpu.VMEM((SUBLANES + q, gw), F32), pltpu.VMEM((SUBLANES + q, ns), F32),
                        pltpu.VMEM((SUBLANES + q, ns), F32), pltpu.VMEM((ns_step, ns, gw), F32)],
        compiler_params=_cparams(("parallel", "parallel", "arbitrary")),
        name="ssd",
    )(*args)


def _merge_kernel(h_ref, ret_ref, ssm_ref, wgr_ref, wgs_ref, wr_ref, ws_ref, o_ref):
    h = h_ref[...]
    gr = jax.nn.sigmoid(_dot(h, wgr_ref[...]))
    gs = jax.nn.sigmoid(_dot(h, wgs_ref[...]))
    a = _dot(ret_ref[...], wr_ref[...])
    b = _dot(ssm_ref[...], ws_ref[...])
    o_ref[...] = (gr * a + gs * b).astype(BF16)


def _merge(h, ret, ssm, w_gr, w_gs, w_r, w_s, *, tm, tn):
    n, d = h.shape
    kr, ks = ret.shape[1], ssm.shape[1]
    return pl.pallas_call(
        _merge_kernel,
        out_shape=jax.ShapeDtypeStruct((n, d), BF16),
        grid=(n // tm, d // tn),
        in_specs=[
            pl.BlockSpec((tm, d), lambda i, j: (i, 0)),
            pl.BlockSpec((tm, kr), lambda i, j: (i, 0)),
            pl.BlockSpec((tm, ks), lambda i, j: (i, 0)),
            pl.BlockSpec((d, tn), lambda i, j: (0, j)),
            pl.BlockSpec((d, tn), lambda i, j: (0, j)),
            pl.BlockSpec((kr, tn), lambda i, j: (0, j)),
            pl.BlockSpec((ks, tn), lambda i, j: (0, j)),
        ],
        out_specs=pl.BlockSpec((tm, tn), lambda i, j: (i, j)),
        compiler_params=_cparams(("parallel", "arbitrary")),
        name="merge",
    )(h, ret, ssm, w_gr, w_gs, w_r, w_s)


def _outproj_kernel(m_ref, w_ref, x_ref, o_ref, *, rc):
    for r in range(m_ref.shape[0] // rc):
        rows = pl.ds(r * rc, rc)
        o_ref[rows, :] = x_ref[rows, :] + _dot(m_ref[rows, :], w_ref[...])


def _outproj(m, w, x, *, tm, tn):
    n, d = x.shape
    k = m.shape[1]
    return pl.pallas_call(
        functools.partial(_outproj_kernel, rc=min(ROW_CHUNK, tm)),
        out_shape=jax.ShapeDtypeStruct((n, d), F32),
        grid=(n // tm, d // tn),
        in_specs=[
            pl.BlockSpec((tm, k), lambda i, j: (i, 0)),
            pl.BlockSpec((k, tn), lambda i, j: (0, j)),
            pl.BlockSpec((tm, tn), lambda i, j: (i, j)),
        ],
        out_specs=pl.BlockSpec((tm, tn), lambda i, j: (i, j)),
        compiler_params=_cparams(("parallel", "arbitrary")),
        name="outproj",
    )(m, w, x)


def _ple_kernel(x_ref, h_ref, p_ref, wp_ref, wg_ref, gf_ref, o_ref):
    pe = _dot(p_ref[...].astype(BF16), wp_ref[...])
    gt = jax.nn.sigmoid(_dot(h_ref[...], wg_ref[...]))
    x = x_ref[...] + pe * gt
    ms = jnp.mean(x * x, axis=-1, keepdims=True)
    o_ref[...] = x * lax.rsqrt(ms + EPS) * gf_ref[...]


def _ple(x, h, p, w_ple, w_gate, g_final, *, tm):
    n, d = x.shape
    pd = p.shape[1]
    return pl.pallas_call(
        _ple_kernel,
        out_shape=jax.ShapeDtypeStruct((n, d), F32),
        grid=(n // tm,),
        in_specs=[
            pl.BlockSpec((tm, d), lambda i: (i, 0)),
            pl.BlockSpec((tm, d), lambda i: (i, 0)),
            pl.BlockSpec((tm, pd), lambda i: (i, 0)),
            pl.BlockSpec((pd, d), lambda i: (0, 0)),
            pl.BlockSpec((d, d), lambda i: (0, 0)),
            pl.BlockSpec((1, d), lambda i: (0, 0)),
        ],
        out_specs=pl.BlockSpec((tm, d), lambda i: (i, 0)),
        compiler_params=_cparams(("parallel",)),
        name="ple",
    )(x, h, p, w_ple, w_gate, g_final)


def _rope_tables(pos0, seqlen, rows):
    half = RET_DK // 2
    inv = ROPE_THETA ** (-jnp.arange(half, dtype=F32) / half)
    pos = (pos0 + jnp.arange(seqlen, dtype=jnp.int32)).astype(F32)
    ang = pos[:, None] * inv[None, :]
    reps = max(rows // seqlen, 1)
    return jnp.tile(jnp.cos(ang), (reps, 1)), jnp.tile(jnp.sin(ang), (reps, 1))


def _pick(n, pref):
    t = pref
    while n % t:
        t //= 2
    return t


def _trunk(x, p, pos0, s_ret, s_ssm, s_conv, w, cfg):
    nseq, seqlen, d = x.shape
    n = nseq * seqlen
    x = x.reshape(n, d)
    p = p.reshape(n, p.shape[-1])
    tm = _pick(n, cfg["tm"])

    x1, h = _ffn(x, w["g_ffn1"], w["g_mix"], w["w1_gu"], w["w1_down"], tm=_pick(n, cfg["tm_ffn"]), tf=cfg["tf"])

    cos, sin = _rope_tables(pos0, seqlen, tm)
    big = _inproj(h, w["w_in"], cos, sin, tm=tm, tn=cfg["tn_in"])

    q = cfg["q_ssd"]
    if seqlen % q == 0:
        seg, nck_ssd = q, _pick(seqlen, cfg["tb_ssd"]) // q
        c = cfg["c_ret"] if seqlen % cfg["c_ret"] == 0 else q
        seg_ret, nck_ret = c, _pick(seqlen, cfg["tb_ret"]) // c
    else:
        assert q % seqlen == 0 and n % q == 0, "short sequences must pack into whole chunks"
        c, seg, seg_ret = q, seqlen, seqlen
        nck_ssd = nck_ret = _pick(n // q, cfg["pack_chunks"])

    dts = _dt_proj(h, w["w_dt"], w["w_dt_t"], w["dt_bias"], w["a_log"], tm=_pick(n, 512), q=q, seg=seg)
    ret, s_ret_new = _retention(big, w["ret_gn_g"], w["ret_gn_b"], s_ret, nseq=nseq, seqlen=seqlen,
                                c=c, seg=seg_ret, nck=nck_ret)

    if s_ssm is not None:
        s0 = s_ssm.reshape(nseq, SSM_GROUPS, SSM_GW, SSM_STATE)
        hist0 = jnp.pad(s_conv, ((0, 0), (SUBLANES - (SSM_CONV - 1), 0), (0, 0)))
    else:
        s0, hist0 = None, None
    ssm, s_ssm_new = _ssd(big, dts, w["conv_w"], w["conv_b"], w["d_skip_x"], w["ssm_norm_g"], s0, hist0,
                          nseq=nseq, seqlen=seqlen, q=q, seg=seg, nck=nck_ssd)
    s_ssm_new = s_ssm_new.reshape(nseq, SSM_HEADS, SSM_HEADDIM, SSM_STATE)

    keep = min(seqlen, SSM_CONV - 1)
    xbc_tail = big.reshape(nseq, seqlen, N_BIG)[:, seqlen - keep:, COL_XBC:].astype(F32)
    if keep < SSM_CONV - 1:
        prev = jnp.zeros((nseq, SSM_CONV - 1, SSM_CONV_DIM), F32) if s_conv is None else s_conv.astype(F32)
        xbc_tail = jnp.concatenate([prev, xbc_tail], axis=1)[:, -(SSM_CONV - 1):]

    merged = _merge(h, ret, ssm, w["w_gr"], w["w_gs"], w["w_br_ret"], w["w_br_ssm"], tm=_pick(n, cfg["tm_mg"]), tn=cfg["tn_mg"])
    x2 = _outproj(merged, w["w_out"], x1, tm=tm, tn=cfg["tn_out"])
    x3, h3 = _ffn(x2, w["g_ffn2"], w["g_ple"], w["w2_gu"], w["w2_down"], tm=_pick(n, cfg["tm_ffn"]), tf=cfg["tf"])
    y = _ple(x3, h3, p, w["w_ple"], w["w_ple_gate"], w["g_final"], tm=_pick(n, cfg["tm_ple"]))
    return y.reshape(nseq, seqlen, d), s_ret_new, s_ssm_new, xbc_tail


CFG = dict(tm=1024, tm_ffn=512, tf=512, tn_in=2048, q_ssd=128, c_ret=256, tb_ret=1024, tb_ssd=1024, pack_chunks=2,
           tm_mg=512, tn_mg=512, tn_out=1024, tm_ple=512)


def _prep_weights(g_ffn1, w1_gu, w1_down, g_mix, w_in, ret_gn_g, ret_gn_b, conv_w, conv_b, dt_bias, a_log,
                  d_skip, ssm_norm_g, w_br_ret, w_br_ssm, w_out, g_ffn2, w2_gu, w2_down, g_ple, w_ple,
                  w_ple_gate, g_final):
    b = lambda a: a[0].astype(BF16)
    r = lambda a: a[0].reshape(1, -1).astype(F32)
    win = w_in[0]
    w_dt = win[:, COL_DT:COL_DT + SSM_HEADS].astype(BF16)
    return dict(
        g_ffn1=r(g_ffn1), w1_gu=b(w1_gu), w1_down=b(w1_down), g_mix=r(g_mix),
        w_in=win.astype(BF16), w_dt=w_dt, w_dt_t=w_dt.T,
        w_gr=win[:, COL_GR:COL_GR + D_MODEL].astype(BF16), w_gs=win[:, COL_GS:COL_GS + D_MODEL].astype(BF16),
        ret_gn_g=ret_gn_g[0].astype(F32), ret_gn_b=ret_gn_b[0].astype(F32),
        conv_w=conv_w[0].astype(F32), conv_b=conv_b[0].astype(F32),
        dt_bias=dt_bias[0].astype(F32), a_log=a_log[0].astype(F32),
        d_skip_x=jnp.repeat(d_skip[0].astype(F32), SSM_HEADDIM).reshape(1, SSM_INNER),
        ssm_norm_g=ssm_norm_g[0].astype(F32),
        w_br_ret=b(w_br_ret), w_br_ssm=b(w_br_ssm), w_out=b(w_out),
        g_ffn2=r(g_ffn2), w2_gu=b(w2_gu), w2_down=b(w2_down), g_ple=r(g_ple),
        w_ple=b(w_ple), w_ple_gate=b(w_ple_gate), g_final=g_final.reshape(1, -1).astype(F32),
    )


def kernel(x_prompt, x_sample, state_ret, state_ssm, state_conv, p_prompt, p_sample, g_ffn1, w1_gu, w1_down, g_mix, w_in, ret_gn_g, ret_gn_b, conv_w, conv_b, dt_bias, a_log, d_skip, ssm_norm_g, w_br_ret, w_br_ssm, w_out, g_ffn2, w2_gu, w2_down, g_ple, w_ple, w_ple_gate, g_final):
    assert g_ffn1.shape[0] == 1, "single-layer trunk"
    w = _prep_weights(g_ffn1, w1_gu, w1_down, g_mix, w_in, ret_gn_g, ret_gn_b, conv_w, conv_b, dt_bias, a_log,
                      d_skip, ssm_norm_g, w_br_ret, w_br_ssm, w_out, g_ffn2, w2_gu, w2_down, g_ple, w_ple,
                      w_ple_gate, g_final)
    y_p, ret_p, ssm_p, conv_p = _trunk(x_prompt, p_prompt[0], 0, None, None, None, w, CFG)
    y_s, ret_s, ssm_s, conv_s = _trunk(x_sample, p_sample[0], PAST_LEN, state_ret[0], state_ssm[0],
                                       state_conv[0], w, CFG)
    e = lambda a: a[None]
    return (y_p, y_s, e(ret_p), e(ssm_p), e(conv_p), e(ret_s), e(ssm_s), e(conv_s))
```

```python
import functools

import numpy as np
import jax
import jax.numpy as jnp
from jax import lax
from jax.experimental import pallas as pl
from jax.experimental.pallas import tpu as pltpu

F32 = jnp.float32
BF16 = jnp.bfloat16

D_MODEL = 2048
PAST_LEN = 4096
EPS = 1e-6
RET_HEADS = 8
RET_DK = 256
RET_DV = 512
RET_QK = RET_HEADS * RET_DK
RET_V = RET_HEADS * RET_DV
ROPE_THETA = 10000.0
SSM_INNER = 2 * D_MODEL
SSM_HEADDIM = 64
SSM_HEADS = SSM_INNER // SSM_HEADDIM
SSM_GROUPS = 8
SSM_HPG = SSM_HEADS // SSM_GROUPS
SSM_GW = SSM_HPG * SSM_HEADDIM
SSM_STATE = 128
SSM_CONV = 4
SSM_CONV_DIM = SSM_INNER + 2 * SSM_GROUPS * SSM_STATE
PLE_DIM = 256

COL_Q = 0
COL_K = COL_Q + RET_QK
COL_V = COL_K + RET_QK
COL_RG = COL_V + RET_V
COL_Z = COL_RG + RET_V
COL_XBC = COL_Z + SSM_INNER
N_BIG = COL_XBC + SSM_CONV_DIM
COL_DT = N_BIG
COL_GR = COL_DT + SSM_HEADS
COL_GS = COL_GR + D_MODEL

SUBLANES = 8
HALF_LANES = 64
VMEM_LIMIT = 62 * 1024 * 1024
FFN_COLS = 256
NORM_ROWS = 128
ROW_CHUNK = 256


def _cparams(sem):
    return pltpu.CompilerParams(dimension_semantics=sem, vmem_limit_bytes=VMEM_LIMIT)


def _rms_bf16(x, g):
    ms = jnp.mean(x * x, axis=-1, keepdims=True)
    return (x * lax.rsqrt(ms + EPS) * g).astype(BF16)


def _silu(x):
    return x * jax.nn.sigmoid(x)


def _dot(a, b):
    return jnp.dot(a, b, preferred_element_type=F32)


def _dot_nt(a, b):
    return lax.dot_general(a, b, (((1,), (1,)), ((), ())), preferred_element_type=F32)


def _dot_tn(a, b):
    return lax.dot_general(a, b, (((0,), (0,)), ((), ())), preferred_element_type=F32)


def _ffn_kernel(x_ref, g_ref, gn_ref, wg_ref, wu_ref, wd_ref, o_ref, hn_ref, h_scr, *, nj):
    j = pl.program_id(1)

    tm = x_ref.shape[0]

    @pl.when(j == 0)
    def _():
        for r in range(0, tm, NORM_ROWS):
            h_scr[r:r + NORM_ROWS, :] = _rms_bf16(x_ref[r:r + NORM_ROWS, :], g_ref[...])
        o_ref[...] = jnp.zeros_like(o_ref)

    h = h_scr[...]
    tf = wg_ref.shape[1]
    acts = []
    for lo in range(0, tf, FFN_COLS):
        hi = min(lo + FFN_COLS, tf)
        gate = _dot(h, wg_ref[:, lo:hi])
        up = _dot(h, wu_ref[:, lo:hi])
        acts.append((_silu(gate) * up).astype(BF16))
    act = jnp.concatenate(acts, axis=1)
    d = o_ref.shape[1]
    for lo in range(0, d, d // 4):
        o_ref[:, lo:lo + d // 4] += _dot(act, wd_ref[:, lo:lo + d // 4])

    @pl.when(j == nj - 1)
    def _():
        for r in range(0, tm, NORM_ROWS):
            xn = x_ref[r:r + NORM_ROWS, :] + 0.5 * o_ref[r:r + NORM_ROWS, :]
            o_ref[r:r + NORM_ROWS, :] = xn
            hn_ref[r:r + NORM_ROWS, :] = _rms_bf16(xn, gn_ref[...])


def _ffn(x, g, g_next, w_gu, w_down, *, tm, tf):
    n, d = x.shape
    f = w_down.shape[0]
    nj = f // tf
    return pl.pallas_call(
        functools.partial(_ffn_kernel, nj=nj),
        out_shape=(jax.ShapeDtypeStruct((n, d), F32), jax.ShapeDtypeStruct((n, d), BF16)),
        grid=(n // tm, nj),
        in_specs=[
            pl.BlockSpec((tm, d), lambda i, j: (i, 0)),
            pl.BlockSpec((1, d), lambda i, j: (0, 0)),
            pl.BlockSpec((1, d), lambda i, j: (0, 0)),
            pl.BlockSpec((d, tf), lambda i, j: (0, j)),
            pl.BlockSpec((d, tf), lambda i, j: (0, j + nj)),
            pl.BlockSpec((tf, d), lambda i, j: (j, 0)),
        ],
        out_specs=(pl.BlockSpec((tm, d), lambda i, j: (i, 0)),
                   pl.BlockSpec((tm, d), lambda i, j: (i, 0))),
        scratch_shapes=[pltpu.VMEM((tm, d), BF16)],
        compiler_params=_cparams(("parallel", "arbitrary")),
        name="ffn",
    )(x, g, g_next, w_gu, w_gu, w_down)


def _inproj_kernel(h_ref, w_ref, cos_ref, sin_ref, o_ref, *, tn, nq, nrope, rc):
    j = pl.program_id(1)
    nr = h_ref.shape[0] // rc

    @pl.when(j >= nrope)
    def _():
        for r in range(nr):
            rows = pl.ds(r * rc, rc)
            o_ref[rows, :] = _dot(h_ref[rows, :], w_ref[...]).astype(BF16)

    @pl.when(j < nrope)
    def _():
        scale = jnp.where(j >= nq, RET_DK ** -0.5, 1.0).astype(F32)
        half = RET_DK // 2
        for r in range(nr):
            rows = pl.ds(r * rc, rc)
            acc = _dot(h_ref[rows, :], w_ref[...])
            c = cos_ref[rows, :] * scale
            s = sin_ref[rows, :] * scale
            for hh in range(tn // RET_DK):
                lo = hh * RET_DK
                x1 = acc[:, lo:lo + half]
                x2 = acc[:, lo + half:lo + RET_DK]
                o_ref[rows, lo:lo + half] = (x1 * c - x2 * s).astype(BF16)
                o_ref[rows, lo + half:lo + RET_DK] = (x2 * c + x1 * s).astype(BF16)


def _inproj(h, w_in, cos, sin, *, tm, tn):
    n, d = h.shape
    nrow = cos.shape[0] // tm
    return pl.pallas_call(
        functools.partial(_inproj_kernel, tn=tn, nq=RET_QK // tn, nrope=2 * RET_QK // tn, rc=min(ROW_CHUNK, tm)),
        out_shape=jax.ShapeDtypeStruct((n, N_BIG), BF16),
        grid=(n // tm, N_BIG // tn),
        in_specs=[
            pl.BlockSpec((tm, d), lambda i, j: (i, 0)),
            pl.BlockSpec((d, tn), lambda i, j: (0, j)),
            pl.BlockSpec((tm, RET_DK // 2), lambda i, j: (i % nrow, 0)),
            pl.BlockSpec((tm, RET_DK // 2), lambda i, j: (i % nrow, 0)),
        ],
        out_specs=pl.BlockSpec((tm, tn), lambda i, j: (i, j)),
        compiler_params=_cparams(("parallel", "arbitrary")),
        name="inproj",
    )(h, w_in, cos, sin)


def _softplus(x):
    return jnp.maximum(x, 0.0) + jnp.log1p(jnp.exp(-jnp.abs(x)))


def _dt_kernel(h_ref, w_ref, wt_ref, br_ref, bc_ref, ar_ref, ac_ref,
               acc_ref, dtr_ref, acr_ref, *, q, seg):
    h = h_ref[...]
    tm = h.shape[0]
    raw_c = _dot(h, w_ref[...])
    raw_r = _dot_nt(wt_ref[...], h)
    dt_c = _softplus(raw_c + br_ref[...])
    dt_r = _softplus(raw_r + bc_ref[...])
    dta_c = dt_c * (-jnp.exp(ar_ref[...]))
    dta_r = dt_r * (-jnp.exp(ac_ref[...]))
    i = lax.broadcasted_iota(jnp.int32, (q, q), 0)
    j = lax.broadcasted_iota(jnp.int32, (q, q), 1)
    same = (i // seg) == (j // seg)
    tri = jnp.where(same & (j <= i), 1.0, 0.0).astype(F32)
    trit = jnp.where(same & (i <= j), 1.0, 0.0).astype(F32)
    for ci in range(tm // q):
        lo_r, hi_r = ci * q, (ci + 1) * q
        ac_c = jnp.dot(tri, dta_c[lo_r:hi_r, :], preferred_element_type=F32, precision=lax.Precision.HIGHEST)
        ac_r = jnp.dot(dta_r[:, lo_r:hi_r], trit, preferred_element_type=F32, precision=lax.Precision.HIGHEST)
        for g in range(SSM_GROUPS):
            lo = g * SSM_HPG
            acc_ref[g, lo_r:hi_r, :] = ac_c[:, lo:lo + SSM_HPG]
            dtr_ref[g, ci] = dt_r[lo:lo + SSM_HPG, lo_r:hi_r]
            acr_ref[g, ci] = ac_r[lo:lo + SSM_HPG, :]


def _dt_proj(h, w_dt, w_dt_t, dt_bias, a_log, *, tm, q, seg):
    n, d = h.shape
    hh = SSM_HEADS
    col = jax.ShapeDtypeStruct((SSM_GROUPS, n, SSM_HPG), F32)
    row = jax.ShapeDtypeStruct((SSM_GROUPS, n // q, SSM_HPG, q), F32)
    return pl.pallas_call(
        functools.partial(_dt_kernel, q=q, seg=seg),
        out_shape=(col, row, row),
        grid=(n // tm,),
        in_specs=[
            pl.BlockSpec((tm, d), lambda i: (i, 0)),
            pl.BlockSpec((d, hh), lambda i: (0, 0)),
            pl.BlockSpec((hh, d), lambda i: (0, 0)),
            pl.BlockSpec((1, hh), lambda i: (0, 0)),
            pl.BlockSpec((hh, 1), lambda i: (0, 0)),
            pl.BlockSpec((1, hh), lambda i: (0, 0)),
            pl.BlockSpec((hh, 1), lambda i: (0, 0)),
        ],
        out_specs=(
            pl.BlockSpec((SSM_GROUPS, tm, SSM_HPG), lambda i: (0, i, 0)),
            pl.BlockSpec((SSM_GROUPS, tm // q, SSM_HPG, q), lambda i: (0, i, 0, 0)),
            pl.BlockSpec((SSM_GROUPS, tm // q, SSM_HPG, q), lambda i: (0, i, 0, 0)),
        ),
        compiler_params=_cparams(("parallel",)),
        name="dt_proj",
    )(h, w_dt, w_dt_t, dt_bias.reshape(1, hh), dt_bias.reshape(hh, 1),
      a_log.reshape(1, hh), a_log.reshape(hh, 1))


def _ret_kernel(*refs, c, seg, nck, has_s0):
    if has_s0:
        (q_ref, k_ref, v_ref, rg_ref, dm_ref, qd_ref, kd_ref, cd_ref, gg_ref, gb_ref, s0_ref,
         o_ref, s_ref) = refs
    else:
        (q_ref, k_ref, v_ref, rg_ref, dm_ref, qd_ref, kd_ref, cd_ref, gg_ref, gb_ref,
         o_ref, s_ref) = refs
    t = pl.program_id(2)
    spc = c // seg

    @pl.when(t == 0)
    def _():
        if has_s0:
            s_ref[...] = s0_ref[...]
        else:
            s_ref[...] = jnp.zeros_like(s_ref)

    dm = dm_ref[0]
    qd = qd_ref[0]
    cd = cd_ref[0]
    gg = gg_ref[...]
    gb = gb_ref[...]
    for ci in range(nck):
        sl = pl.ds(ci * c, c)
        q = q_ref[sl, :]
        k = k_ref[sl, :]
        v = v_ref[sl, :]
        att = _dot_nt(q, k) * dm
        o = _dot(att.astype(BF16), v)
        kf = k.astype(F32)
        inter = []
        for slot in range(spc):
            sidx = (ci * spc + slot) if spc > 1 else 0
            st = s_ref[sidx, 0]
            qs = q if spc == 1 else q[slot * seg:(slot + 1) * seg, :]
            inter.append(_dot(qs, st.astype(BF16)))
            s_ref[sidx, 0] = st * cd + _dot_tn((kf * kd_ref[0, slot]).astype(BF16), v)
        o = o + (inter[0] if spc == 1 else jnp.concatenate(inter, axis=0)) * qd
        mu = jnp.mean(o, axis=-1, keepdims=True)
        dev = o - mu
        var = jnp.mean(dev * dev, axis=-1, keepdims=True)
        on = dev * lax.rsqrt(var + EPS)
        rg = rg_ref[sl, :].astype(F32)
        o_ref[sl, :] = (_silu(rg) * (on * gg + gb)).astype(BF16)


def _ret_tables(c, seg):
    spc = c // seg
    hs = np.arange(RET_HEADS, dtype=np.float64)
    log_g = np.log1p(-np.exp2(-5.0 - hs))
    idx = np.arange(c)
    pos = (idx % seg).astype(np.float64)
    slot = idx // seg
    diff = pos[:, None] - pos[None, :]
    live = (slot[:, None] == slot[None, :]) & (diff >= 0)
    dmat = np.where(live[None], np.exp(np.maximum(diff, 0.0)[None] * log_g[:, None, None]), 0.0)
    q_dec = np.exp((pos[None, :] + 1.0) * log_g[:, None])
    k_dec = np.exp((seg - 1.0 - pos[None, :]) * log_g[:, None])
    c_dec = np.exp(seg * log_g)
    k_slot = np.where(slot[None, None, :] == np.arange(spc)[None, :, None], k_dec[:, None, :], 0.0)
    qd = np.broadcast_to(q_dec[:, :, None], (RET_HEADS, c, RET_DV))
    kd = np.broadcast_to(k_slot[:, :, :, None], (RET_HEADS, spc, c, RET_DK))
    cd = np.broadcast_to(c_dec[:, None, None], (RET_HEADS, 1, RET_DV))
    f = lambda a: jnp.asarray(np.ascontiguousarray(a), dtype=F32)
    return f(dmat), f(qd), f(kd), f(cd)


def _retention(big, gn_g, gn_b, s0, *, nseq, seqlen, c, seg, nck):
    n = big.shape[0]
    spc = c // seg
    rb = nck * c
    ns_step = spc * nck if spc > 1 else 1
    nt = 1 if spc > 1 else seqlen // rb
    dm, qd, kd, cd = _ret_tables(c, seg)
    has_s0 = s0 is not None
    kq, kk = COL_Q // RET_DK, COL_K // RET_DK
    kv, kr = COL_V // RET_DV, COL_RG // RET_DV
    row = lambda b, h, t: b * nt + t
    in_specs = [
        pl.BlockSpec((rb, RET_DK), lambda b, h, t: (row(b, h, t), kq + h)),
        pl.BlockSpec((rb, RET_DK), lambda b, h, t: (row(b, h, t), kk + h)),
        pl.BlockSpec((rb, RET_DV), lambda b, h, t: (row(b, h, t), kv + h)),
        pl.BlockSpec((rb, RET_DV), lambda b, h, t: (row(b, h, t), kr + h)),
        pl.BlockSpec((1, c, c), lambda b, h, t: (h, 0, 0)),
        pl.BlockSpec((1, c, RET_DV), lambda b, h, t: (h, 0, 0)),
        pl.BlockSpec((1, spc, c, RET_DK), lambda b, h, t: (h, 0, 0, 0)),
        pl.BlockSpec((1, 1, RET_DV), lambda b, h, t: (h, 0, 0)),
        pl.BlockSpec((1, RET_DV), lambda b, h, t: (0, h)),
        pl.BlockSpec((1, RET_DV), lambda b, h, t: (0, h)),
    ]
    args = [big, big, big, big, dm, qd, kd, cd, gn_g.reshape(1, RET_V), gn_b.reshape(1, RET_V)]
    st_spec = pl.BlockSpec((ns_step, 1, RET_DK, RET_DV), lambda b, h, t: (b, h, 0, 0))
    if has_s0:
        in_specs.append(st_spec)
        args.append(s0)
    return pl.pallas_call(
        functools.partial(_ret_kernel, c=c, seg=seg, nck=nck, has_s0=has_s0),
        out_shape=(jax.ShapeDtypeStruct((n, RET_V), BF16),
                   jax.ShapeDtypeStruct((nseq, RET_HEADS, RET_DK, RET_DV), F32)),
        grid=(nseq // ns_step, RET_HEADS, nt),
        in_specs=in_specs,
        out_specs=(pl.BlockSpec((rb, RET_DV), lambda b, h, t: (row(b, h, t), h)), st_spec),
        compiler_params=_cparams(("parallel", "parallel", "arbitrary")),
        name="retention",
    )(*args)


def _conv_silu(raw_scr, cw, bias, tb):
    taps = SSM_CONV
    acc = bias
    for s in range(taps):
        acc = acc + raw_scr[pl.ds(SUBLANES - s, tb), :] * cw[taps - 1 - s:taps - s, :]
    return _silu(acc)


def _bf16_terms(v):
    t1 = v.astype(BF16).astype(F32)
    r1 = v - t1
    t2 = r1.astype(BF16).astype(F32)
    t3 = (r1 - t2).astype(BF16).astype(F32)
    return t1, t2, t3


def _expand_matrix(heads, width):
    k = lax.broadcasted_iota(jnp.int32, (3 * heads, heads * width), 0) % heads
    c = lax.broadcasted_iota(jnp.int32, (3 * heads, heads * width), 1) // width
    return jnp.where(k == c, 1.0, 0.0).astype(F32)


def _head_expand(v, expand):
    return _dot_tn(jnp.concatenate(_bf16_terms(v), axis=0), expand)


def _ssd_kernel(*refs, q, seg, nck, has_s0):
    if has_s0:
        (xs_ref, b_ref, c_ref, z_ref, acc_ref, dtr_ref, acr_ref,
         cwx_ref, cwb_ref, cwc_ref, cbx_ref, cbb_ref, cbc_ref, dsk_ref, ng_ref,
         s0_ref, hx0_ref, hb0_ref, hc0_ref,
         y_ref, so_ref, hx, hb, hc, st_scr) = refs
        hist0 = (hx0_ref, hb0_ref, hc0_ref)
    else:
        (xs_ref, b_ref, c_ref, z_ref, acc_ref, dtr_ref, acr_ref,
         cwx_ref, cwb_ref, cwc_ref, cbx_ref, cbb_ref, cbc_ref, dsk_ref, ng_ref,
         y_ref, so_ref, hx, hb, hc, st_scr) = refs
        hist0 = (None, None, None)
    t = pl.program_id(2)
    spc = q // seg
    ns_step = st_scr.shape[0]
    conv_in = ((xs_ref, hx, hist0[0], cwx_ref, cbx_ref),
               (b_ref, hb, hist0[1], cwb_ref, cbb_ref),
               (c_ref, hc, hist0[2], cwc_ref, cbc_ref))

    def first_rows(raw_scr, h0_ref, sidx):
        if has_s0:
            raw_scr[0:SUBLANES, :] = h0_ref[sidx]
        else:
            raw_scr[0:SUBLANES, :] = jnp.zeros((SUBLANES, raw_scr.shape[1]), F32)

    @pl.when(t == 0)
    def _():
        for sidx in range(ns_step):
            if has_s0:
                st_scr[sidx] = s0_ref[sidx, 0].T
            else:
                st_scr[sidx] = jnp.zeros(st_scr.shape[1:], F32)
        if spc == 1:
            for _, raw_scr, h0_ref, _, _ in conv_in:
                first_rows(raw_scr, h0_ref, 0)

    ii = lax.broadcasted_iota(jnp.int32, (q, q), 0)
    jj = lax.broadcasted_iota(jnp.int32, (q, q), 1)
    causal = (ii >= jj) if spc == 1 else ((ii >= jj) & ((ii // seg) == (jj // seg)))
    lane = lax.broadcasted_iota(jnp.int32, (q, 2 * HALF_LANES), 1)
    row_slot = lax.broadcasted_iota(jnp.int32, (q, SSM_STATE), 0) // seg
    tok = lax.broadcasted_iota(jnp.int32, (SSM_HPG, q), 1)
    dsk = dsk_ref[...]
    ng = ng_ref[...]
    expand = _expand_matrix(SSM_HPG, SSM_HEADDIM)

    def conv_chunk(ci, sl):
        out = []
        for raw_ref, raw_scr, h0_ref, cw_ref, cb_ref in conv_in:
            if spc == 1:
                raw_scr[SUBLANES:SUBLANES + q, :] = raw_ref[sl, :].astype(F32)
                out.append(_conv_silu(raw_scr, cw_ref[...], cb_ref[...], q))
                raw_scr[0:SUBLANES, :] = raw_scr[q:q + SUBLANES, :]
            else:
                pieces = []
                for slot in range(spc):
                    first_rows(raw_scr, h0_ref, ci * spc + slot)
                    raw_scr[SUBLANES:SUBLANES + seg, :] = raw_ref[pl.ds(ci * q + slot * seg, seg), :].astype(F32)
                    pieces.append(_conv_silu(raw_scr, cw_ref[...], cb_ref[...], seg))
                out.append(jnp.concatenate(pieces, axis=0))
        return out

    def chunk(ci):
        sl = pl.ds(pl.multiple_of(ci * q, q), q)
        x, b_f32, c_f32 = conv_chunk(ci, sl)
        bq = b_f32.astype(BF16)
        cq = c_f32.astype(BF16)
        dtr = dtr_ref[0, ci]
        acr = acr_ref[0, ci]
        acc = acc_ref[0, sl, :]

        cb = jnp.where(causal, _dot_nt(cq, bq), 0.0)
        pairs = []
        for m in range(SSM_HPG // 2):
            ws = []
            for r in (2 * m, 2 * m + 1):
                seg_sum = acc[:, r:r + 1] - acr[r:r + 1, :]
                ws.append((jnp.exp(jnp.minimum(seg_sum, 0.0)) * (cb * dtr[r:r + 1, :])).astype(BF16))
            xp = x[:, m * 2 * HALF_LANES:(m + 1) * 2 * HALF_LANES]
            x_lo = jnp.where(lane < HALF_LANES, xp, 0.0).astype(BF16)
            x_hi = jnp.where(lane >= HALF_LANES, xp, 0.0).astype(BF16)
            pairs.append(_dot(jnp.concatenate(ws, axis=1), jnp.concatenate([x_lo, x_hi], axis=0)))
        y = jnp.concatenate(pairs, axis=1)

        a_last = acr[:, q - 1:q]
        for slot in range(spc - 2, -1, -1):
            end = (slot + 1) * seg
            a_last = jnp.where(tok < end, acr[:, end - 1:end], a_last)
        ea = _head_expand(jnp.exp(acr), expand)
        xw = (x * _head_expand(jnp.exp(a_last - acr) * dtr, expand)).astype(BF16)
        inter = []
        for slot in range(spc):
            sidx = (ci * spc + slot) if spc > 1 else 0
            state = st_scr[sidx]
            end = (slot + 1) * seg
            cs = cq if spc == 1 else cq[slot * seg:end, :]
            bs = bq if spc == 1 else jnp.where(row_slot == slot, b_f32, 0.0).astype(BF16)
            inter.append(_dot(cs, state.astype(BF16)))
            st_scr[sidx] = state * ea[end - 1:end, :] + _dot_tn(bs, xw)
        y = y + (inter[0] if spc == 1 else jnp.concatenate(inter, axis=0)) * ea

        y = y + dsk * x
        y = y * _silu(z_ref[sl, :].astype(F32))
        ms = jnp.mean(y * y, axis=-1, keepdims=True)
        y_ref[sl, :] = (y * lax.rsqrt(ms + EPS) * ng).astype(BF16)

    if spc > 1 or nck == 1:
        for ci in range(nck):
            chunk(ci)
    else:
        lax.fori_loop(0, nck, lambda ci, carry: (chunk(ci), carry)[1], 0)

    @pl.when(t == pl.num_programs(2) - 1)
    def _():
        for sidx in range(ns_step):
            so_ref[sidx, 0] = st_scr[sidx].T


def _ssd(big, dts, conv_w, conv_b, d_skip_x, norm_g, s0, hist0, *, nseq, seqlen, q, seg, nck):
    n = big.shape[0]
    spc = q // seg
    rb = nck * q
    ns_step = spc * nck if spc > 1 else 1
    nt = 1 if spc > 1 else seqlen // rb
    has_s0 = s0 is not None
    acc, dtr, acr = dts
    gw, ns = SSM_GW, SSM_STATE
    kz, kx = COL_Z // gw, COL_XBC // gw
    kb = (COL_XBC + SSM_INNER) // ns
    kc = kb + SSM_GROUPS
    cb_b = SSM_INNER // ns
    cb_c = cb_b + SSM_GROUPS
    row = lambda b, g, t: b * nt + t
    in_specs = [
        pl.BlockSpec((rb, gw), lambda b, g, t: (row(b, g, t), kx + g)),
        pl.BlockSpec((rb, ns), lambda b, g, t: (row(b, g, t), kb + g)),
        pl.BlockSpec((rb, ns), lambda b, g, t: (row(b, g, t), kc + g)),
        pl.BlockSpec((rb, gw), lambda b, g, t: (row(b, g, t), kz + g)),
        pl.BlockSpec((1, rb, SSM_HPG), lambda b, g, t: (g, row(b, g, t), 0)),
        pl.BlockSpec((1, nck, SSM_HPG, q), lambda b, g, t: (g, row(b, g, t), 0, 0)),
        pl.BlockSpec((1, nck, SSM_HPG, q), lambda b, g, t: (g, row(b, g, t), 0, 0)),
        pl.BlockSpec((SSM_CONV, gw), lambda b, g, t: (0, g)),
        pl.BlockSpec((SSM_CONV, ns), lambda b, g, t: (0, cb_b + g)),
        pl.BlockSpec((SSM_CONV, ns), lambda b, g, t: (0, cb_c + g)),
        pl.BlockSpec((1, gw), lambda b, g, t: (0, g)),
        pl.BlockSpec((1, ns), lambda b, g, t: (0, cb_b + g)),
        pl.BlockSpec((1, ns), lambda b, g, t: (0, cb_c + g)),
        pl.BlockSpec((1, gw), lambda b, g, t: (0, g)),
        pl.BlockSpec((1, gw), lambda b, g, t: (0, g)),
    ]
    cbias = conv_b.reshape(1, SSM_CONV_DIM)
    args = [big, big, big, big, acc, dtr, acr, conv_w, conv_w, conv_w, cbias, cbias, cbias,
            d_skip_x, norm_g.reshape(1, SSM_INNER)]
    st_spec = pl.BlockSpec((ns_step, 1, gw, ns), lambda b, g, t: (b, g, 0, 0))
    if has_s0:
        in_specs += [
            st_spec,
            pl.BlockSpec((ns_step, SUBLANES, gw), lambda b, g, t: (b, 0, g)),
            pl.BlockSpec((ns_step, SUBLANES, ns), lambda b, g, t: (b, 0, cb_b + g)),
            pl.BlockSpec((ns_step, SUBLANES, ns), lambda b, g, t: (b, 0, cb_c + g)),
        ]
        args += [s0, hist0, hist0, hist0]
    return pl.pallas_call(
        functools.partial(_ssd_kernel, q=q, seg=seg, nck=nck, has_s0=has_s0),
        out_shape=(jax.ShapeDtypeStruct((n, SSM_INNER), BF16),
                   jax.ShapeDtypeStruct((nseq, SSM_GROUPS, gw, ns), F32)),
        grid=(nseq // ns_step, SSM_GROUPS, nt),
        in_specs=in_specs,
        out_specs=(pl.BlockSpec((rb, gw), lambda b, g, t: (row(b, g, t), g)), st_spec),
        scratch_shapes=[pltpu.VMEM((SUBLANES + q, gw), F32), pltpu.VMEM((SUBLANES + q, ns), F32),
                        pltpu.VMEM((SUBLANES + q, ns), F32), pltpu.VMEM((ns_step, ns, gw), F32)],
        compiler_params=_cparams(("parallel", "parallel", "arbitrary")),
        name="ssd",
    )(*args)


def _merge_kernel(h_ref, ret_ref, ssm_ref, wgr_ref, wgs_ref, wr_ref, ws_ref, o_ref):
    h = h_ref[...]
    gr = jax.nn.sigmoid(_dot(h, wgr_ref[...]))
    gs = jax.nn.sigmoid(_dot(h, wgs_ref[...]))
    a = _dot(ret_ref[...], wr_ref[...])
    b = _dot(ssm_ref[...], ws_ref[...])
    o_ref[...] = (gr * a + gs * b).astype(BF16)


def _merge(h, ret, ssm, w_gr, w_gs, w_r, w_s, *, tm, tn):
    n, d = h.shape
    kr, ks = ret.shape[1], ssm.shape[1]
    return pl.pallas_call(
        _merge_kernel,
        out_shape=jax.ShapeDtypeStruct((n, d), BF16),
        grid=(n // tm, d // tn),
        in_specs=[
            pl.BlockSpec((tm, d), lambda i, j: (i, 0)),
            pl.BlockSpec((tm, kr), lambda i, j: (i, 0)),
            pl.BlockSpec((tm, ks), lambda i, j: (i, 0)),
            pl.BlockSpec((d, tn), lambda i, j: (0, j)),
            pl.BlockSpec((d, tn), lambda i, j: (0, j)),
            pl.BlockSpec((kr, tn), lambda i, j: (0, j)),
            pl.BlockSpec((ks, tn), lambda i, j: (0, j)),
        ],
        out_specs=pl.BlockSpec((tm, tn), lambda i, j: (i, j)),
        compiler_params=_cparams(("parallel", "arbitrary")),
        name="merge",
    )(h, ret, ssm, w_gr, w_gs, w_r, w_s)


def _outproj_kernel(m_ref, w_ref, x_ref, o_ref, *, rc):
    for r in range(m_ref.shape[0] // rc):
        rows = pl.ds(r * rc, rc)
        o_ref[rows, :] = x_ref[rows, :] + _dot(m_ref[rows, :], w_ref[...])


def _outproj(m, w, x, *, tm, tn):
    n, d = x.shape
    k = m.shape[1]
    return pl.pallas_call(
        functools.partial(_outproj_kernel, rc=min(ROW_CHUNK, tm)),
        out_shape=jax.ShapeDtypeStruct((n, d), F32),
        grid=(n // tm, d // tn),
        in_specs=[
            pl.BlockSpec((tm, k), lambda i, j: (i, 0)),
            pl.BlockSpec((k, tn), lambda i, j: (0, j)),
            pl.BlockSpec((tm, tn), lambda i, j: (i, j)),
        ],
        out_specs=pl.BlockSpec((tm, tn), lambda i, j: (i, j)),
        compiler_params=_cparams(("parallel", "arbitrary")),
        name="outproj",
    )(m, w, x)


def _ple_kernel(x_ref, h_ref, p_ref, wp_ref, wg_ref, gf_ref, o_ref):
    pe = _dot(p_ref[...].astype(BF16), wp_ref[...])
    gt = jax.nn.sigmoid(_dot(h_ref[...], wg_ref[...]))
    x = x_ref[...] + pe * gt
    ms = jnp.mean(x * x, axis=-1, keepdims=True)
    o_ref[...] = x * lax.rsqrt(ms + EPS) * gf_ref[...]


def _ple(x, h, p, w_ple, w_gate, g_final, *, tm):
    n, d = x.shape
    pd = p.shape[1]
    return pl.pallas_call(
        _ple_kernel,
        out_shape=jax.ShapeDtypeStruct((n, d), F32),
        grid=(n // tm,),
        in_specs=[
            pl.BlockSpec((tm, d), lambda i: (i, 0)),
            pl.BlockSpec((tm, d), lambda i: (i, 0)),
            pl.BlockSpec((tm, pd), lambda i: (i, 0)),
            pl.BlockSpec((pd, d), lambda i: (0, 0)),
            pl.BlockSpec((d, d), lambda i: (0, 0)),
            pl.BlockSpec((1, d), lambda i: (0, 0)),
        ],
        out_specs=pl.BlockSpec((tm, d), lambda i: (i, 0)),
        compiler_params=_cparams(("parallel",)),
        name="ple",
    )(x, h, p, w_ple, w_gate, g_final)


def _rope_tables(pos0, seqlen, rows):
    half = RET_DK // 2
    inv = ROPE_THETA ** (-jnp.arange(half, dtype=F32) / half)
    pos = (pos0 + jnp.arange(seqlen, dtype=jnp.int32)).astype(F32)
    ang = pos[:, None] * inv[None, :]
    reps = max(rows // seqlen, 1)
    return jnp.tile(jnp.cos(ang), (reps, 1)), jnp.tile(jnp.sin(ang), (reps, 1))


def _pick(n, pref):
    t = pref
    while n % t:
        t //= 2
    return t


def _trunk(x, p, pos0, s_ret, s_ssm, s_conv, w, cfg):
    nseq, seqlen, d = x.shape
    n = nseq * seqlen
    x = x.reshape(n, d)
    p = p.reshape(n, p.shape[-1])
    tm = _pick(n, cfg["tm"])

    x1, h = _ffn(x, w["g_ffn1"], w["g_mix"], w["w1_gu"], w["w1_down"], tm=_pick(n, cfg["tm_ffn"]), tf=cfg["tf"])

    cos, sin = _rope_tables(pos0, seqlen, tm)
    big = _inproj(h, w["w_in"], cos, sin, tm=tm, tn=cfg["tn_in"])

    q = cfg["q_ssd"]
    if seqlen % q == 0:
        seg, nck_ssd = q, _pick(seqlen, cfg["tb_ssd"]) // q
        c = cfg["c_ret"] if seqlen % cfg["c_ret"] == 0 else q
        seg_ret, nck_ret = c, _pick(seqlen, cfg["tb_ret"]) // c
    else:
        assert q % seqlen == 0 and n % q == 0, "short sequences must pack into whole chunks"
        c, seg, seg_ret = q, seqlen, seqlen
        nck_ssd = nck_ret = _pick(n // q, cfg["pack_chunks"])

    dts = _dt_proj(h, w["w_dt"], w["w_dt_t"], w["dt_bias"], w["a_log"], tm=_pick(n, 512), q=q, seg=seg)
    ret, s_ret_new = _retention(big, w["ret_gn_g"], w["ret_gn_b"], s_ret, nseq=nseq, seqlen=seqlen,
                                c=c, seg=seg_ret, nck=nck_ret)

    if s_ssm is not None:
        s0 = s_ssm.reshape(nseq, SSM_GROUPS, SSM_GW, SSM_STATE)
        hist0 = jnp.pad(s_conv, ((0, 0), (SUBLANES - (SSM_CONV - 1), 0), (0, 0)))
    else:
        s0, hist0 = None, None
    ssm, s_ssm_new = _ssd(big, dts, w["conv_w"], w["conv_b"], w["d_skip_x"], w["ssm_norm_g"], s0, hist0,
                          nseq=nseq, seqlen=seqlen, q=q, seg=seg, nck=nck_ssd)
    s_ssm_new = s_ssm_new.reshape(nseq, SSM_HEADS, SSM_HEADDIM, SSM_STATE)

    keep = min(seqlen, SSM_CONV - 1)
    xbc_tail = big.reshape(nseq, seqlen, N_BIG)[:, seqlen - keep:, COL_XBC:].astype(F32)
    if keep < SSM_CONV - 1:
        prev = jnp.zeros((nseq, SSM_CONV - 1, SSM_CONV_DIM), F32) if s_conv is None else s_conv.astype(F32)
        xbc_tail = jnp.concatenate([prev, xbc_tail], axis=1)[:, -(SSM_CONV - 1):]

    merged = _merge(h, ret, ssm, w["w_gr"], w["w_gs"], w["w_br_ret"], w["w_br_ssm"], tm=_pick(n, cfg["tm_mg"]), tn=cfg["tn_mg"])
    x2 = _outproj(merged, w["w_out"], x1, tm=tm, tn=cfg["tn_out"])
    x3, h3 = _ffn(x2, w["g_ffn2"], w["g_ple"], w["w2_gu"], w["w2_down"], tm=_pick(n, cfg["tm_ffn"]), tf=cfg["tf"])
    y = _ple(x3, h3, p, w["w_ple"], w["w_ple_gate"], w["g_final"], tm=_pick(n, cfg["tm_ple"]))
    return y.reshape(nseq, seqlen, d), s_ret_new, s_ssm_new, xbc_tail


CFG = dict(tm=1024, tm_ffn=512, tf=1408, tn_in=2048, q_ssd=128, c_ret=256, tb_ret=1024, tb_ssd=1024, pack_chunks=2,
           tm_mg=512, tn_mg=512, tn_out=1024, tm_ple=512)


def _prep_weights(g_ffn1, w1_gu, w1_down, g_mix, w_in, ret_gn_g, ret_gn_b, conv_w, conv_b, dt_bias, a_log,
                  d_skip, ssm_norm_g, w_br_ret, w_br_ssm, w_out, g_ffn2, w2_gu, w2_down, g_ple, w_ple,
                  w_ple_gate, g_final):
    b = lambda a: a[0].astype(BF16)
    r = lambda a: a[0].reshape(1, -1).astype(F32)
    win = w_in[0]
    w_dt = win[:, COL_DT:COL_DT + SSM_HEADS].astype(BF16)
    return dict(
        g_ffn1=r(g_ffn1), w1_gu=b(w1_gu), w1_down=b(w1_down), g_mix=r(g_mix),
        w_in=win.astype(BF16), w_dt=w_dt, w_dt_t=w_dt.T,
        w_gr=win[:, COL_GR:COL_GR + D_MODEL].astype(BF16), w_gs=win[:, COL_GS:COL_GS + D_MODEL].astype(BF16),
        ret_gn_g=ret_gn_g[0].astype(F32), ret_gn_b=ret_gn_b[0].astype(F32),
        conv_w=conv_w[0].astype(F32), conv_b=conv_b[0].astype(F32),
        dt_bias=dt_bias[0].astype(F32), a_log=a_log[0].astype(F32),
        d_skip_x=jnp.repeat(d_skip[0].astype(F32), SSM_HEADDIM).reshape(1, SSM_INNER),
        ssm_norm_g=ssm_norm_g[0].astype(F32),
        w_br_ret=b(w_br_ret), w_br_ssm=b(w_br_ssm), w_out=b(w_out),
        g_ffn2=r(g_ffn2), w2_gu=b(w2_gu), w2_down=b(w2_down), g_ple=r(g_ple),
        w_ple=b(w_ple), w_ple_gate=b(w_ple_gate), g_final=g_final.reshape(1, -1).astype(F32),
    )


def kernel(x_prompt, x_sample, state_ret, state_ssm, state_conv, p_prompt, p_sample, g_ffn1, w1_gu, w1_down, g_mix, w_in, ret_gn_g, ret_gn_b, conv_w, conv_b, dt_bias, a_log, d_skip, ssm_norm_g, w_br_ret, w_br_ssm, w_out, g_ffn2, w2_gu, w2_down, g_ple, w_ple, w_ple_gate, g_final):
    assert g_ffn1.shape[0] == 1, "single-layer trunk"
    w = _prep_weights(g_ffn1, w1_gu, w1_down, g_mix, w_in, ret_gn_g, ret_gn_b, conv_w, conv_b, dt_bias, a_log,
                      d_skip, ssm_norm_g, w_br_ret, w_br_ssm, w_out, g_ffn2, w2_gu, w2_down, g_ple, w_ple,
                      w_ple_gate, g_final)
    y_p, ret_p, ssm_p, conv_p = _trunk(x_prompt, p_prompt[0], 0, None, None, None, w, CFG)
    y_s, ret_s, ssm_s, conv_s = _trunk(x_sample, p_sample[0], PAST_LEN, state_ret[0], state_ssm[0],
                                       state_conv[0], w, CFG)
    e = lambda a: a[None]
    return (y_p, y_s, e(ret_p), e(ssm_p), e(conv_p), e(ret_s), e(ssm_s), e(conv_s))
```

```python
import functools

import numpy as np
import jax
import jax.numpy as jnp
from jax import lax
from jax.experimental import pallas as pl
from jax.experimental.pallas import tpu as pltpu

F32 = jnp.float32
BF16 = jnp.bfloat16

D_MODEL = 2048
PAST_LEN = 4096
EPS = 1e-6
LOG2E = 1.4426950408889634
RET_HEADS = 8
RET_DK = 256
RET_DV = 512
RET_QK = RET_HEADS * RET_DK
RET_V = RET_HEADS * RET_DV
ROPE_THETA = 10000.0
SSM_INNER = 2 * D_MODEL
SSM_HEADDIM = 64
SSM_HEADS = SSM_INNER // SSM_HEADDIM
SSM_GROUPS = 8
SSM_HPG = SSM_HEADS // SSM_GROUPS
SSM_GW = SSM_HPG * SSM_HEADDIM
SSM_STATE = 128
SSM_CONV = 4
SSM_CONV_DIM = SSM_INNER + 2 * SSM_GROUPS * SSM_STATE
PLE_DIM = 256

COL_Q = 0
COL_K = COL_Q + RET_QK
COL_V = COL_K + RET_QK
COL_RG = COL_V + RET_V
COL_Z = COL_RG + RET_V
COL_XBC = COL_Z + SSM_INNER
N_BIG = COL_XBC + SSM_CONV_DIM
COL_DT = N_BIG
COL_GR = COL_DT + SSM_HEADS
COL_GS = COL_GR + D_MODEL

SUBLANES = 8
HALF_LANES = 64
VMEM_LIMIT = 56 * 1024 * 1024
FFN_COLS = 256
NORM_ROWS = 128
ROW_CHUNK = 256


def _cparams(sem):
    return pltpu.CompilerParams(dimension_semantics=sem, vmem_limit_bytes=VMEM_LIMIT)


def _rms_bf16(x, g):
    ms = jnp.mean(x * x, axis=-1, keepdims=True)
    return (x * lax.rsqrt(ms + EPS) * g).astype(BF16)


def _silu(x):
    return x * jax.nn.sigmoid(x)


def _dot(a, b):
    return jnp.dot(a, b, preferred_element_type=F32)


def _dot_nt(a, b):
    return lax.dot_general(a, b, (((1,), (1,)), ((), ())), preferred_element_type=F32)


def _dot_tn(a, b):
    return lax.dot_general(a, b, (((0,), (0,)), ((), ())), preferred_element_type=F32)


def _ffn_kernel(x_ref, g_ref, gn_ref, wg_ref, wu_ref, wd_ref, o_ref, hn_ref, h_scr, *, nj):
    j = pl.program_id(1)

    tm = x_ref.shape[0]

    @pl.when(j == 0)
    def _():
        for r in range(0, tm, NORM_ROWS):
            h_scr[r:r + NORM_ROWS, :] = _rms_bf16(x_ref[r:r + NORM_ROWS, :], g_ref[...])
        o_ref[...] = jnp.zeros_like(o_ref)

    h = h_scr[...]
    tf = wg_ref.shape[1]
    acts = []
    for lo in range(0, tf, FFN_COLS):
        hi = min(lo + FFN_COLS, tf)
        gate = _dot(h, wg_ref[:, lo:hi])
        up = _dot(h, wu_ref[:, lo:hi])
        acts.append((_silu(gate) * up).astype(BF16))
    act = jnp.concatenate(acts, axis=1)
    d = o_ref.shape[1]
    for lo in range(0, d, d // 4):
        o_ref[:, lo:lo + d // 4] += _dot(act, wd_ref[:, lo:lo + d // 4])

    @pl.when(j == nj - 1)
    def _():
        for r in range(0, tm, NORM_ROWS):
            xn = x_ref[r:r + NORM_ROWS, :] + 0.5 * o_ref[r:r + NORM_ROWS, :]
            o_ref[r:r + NORM_ROWS, :] = xn
            hn_ref[r:r + NORM_ROWS, :] = _rms_bf16(xn, gn_ref[...])


def _ffn(x, g, g_next, w_gu, w_down, *, tm, tf):
    n, d = x.shape
    f = w_down.shape[0]
    nj = f // tf
    return pl.pallas_call(
        functools.partial(_ffn_kernel, nj=nj),
        out_shape=(jax.ShapeDtypeStruct((n, d), F32), jax.ShapeDtypeStruct((n, d), BF16)),
        grid=(n // tm, nj),
        in_specs=[
            pl.BlockSpec((tm, d), lambda i, j: (i, 0)),
            pl.BlockSpec((1, d), lambda i, j: (0, 0)),
            pl.BlockSpec((1, d), lambda i, j: (0, 0)),
            pl.BlockSpec((d, tf), lambda i, j: (0, j)),
            pl.BlockSpec((d, tf), lambda i, j: (0, j + nj)),
            pl.BlockSpec((tf, d), lambda i, j: (j, 0)),
        ],
        out_specs=(pl.BlockSpec((tm, d), lambda i, j: (i, 0)),
                   pl.BlockSpec((tm, d), lambda i, j: (i, 0))),
        scratch_shapes=[pltpu.VMEM((tm, d), BF16)],
        compiler_params=_cparams(("parallel", "arbitrary")),
        name="ffn",
    )(x, g, g_next, w_gu, w_gu, w_down)


def _inproj_kernel(h_ref, w_ref, cos_ref, sin_ref, o_ref, *, tn, nq, nrope, rc):
    j = pl.program_id(1)
    nr = h_ref.shape[0] // rc

    @pl.when(j >= nrope)
    def _():
        for r in range(nr):
            rows = pl.ds(r * rc, rc)
            o_ref[rows, :] = _dot(h_ref[rows, :], w_ref[...]).astype(BF16)

    @pl.when(j < nrope)
    def _():
        scale = jnp.where(j >= nq, RET_DK ** -0.5, 1.0).astype(F32)
        half = RET_DK // 2
        for r in range(nr):
            rows = pl.ds(r * rc, rc)
            acc = _dot(h_ref[rows, :], w_ref[...])
            c = cos_ref[rows, :] * scale
            s = sin_ref[rows, :] * scale
            for hh in range(tn // RET_DK):
                lo = hh * RET_DK
                x1 = acc[:, lo:lo + half]
                x2 = acc[:, lo + half:lo + RET_DK]
                o_ref[rows, lo:lo + half] = (x1 * c - x2 * s).astype(BF16)
                o_ref[rows, lo + half:lo + RET_DK] = (x2 * c + x1 * s).astype(BF16)


def _inproj(h, w_in, cos, sin, *, tm, tn):
    n, d = h.shape
    nrow = cos.shape[0] // tm
    return pl.pallas_call(
        functools.partial(_inproj_kernel, tn=tn, nq=RET_QK // tn, nrope=2 * RET_QK // tn, rc=min(ROW_CHUNK, tm)),
        out_shape=jax.ShapeDtypeStruct((n, N_BIG), BF16),
        grid=(n // tm, N_BIG // tn),
        in_specs=[
            pl.BlockSpec((tm, d), lambda i, j: (i, 0)),
            pl.BlockSpec((d, tn), lambda i, j: (0, j)),
            pl.BlockSpec((tm, RET_DK // 2), lambda i, j: (i % nrow, 0)),
            pl.BlockSpec((tm, RET_DK // 2), lambda i, j: (i % nrow, 0)),
        ],
        out_specs=pl.BlockSpec((tm, tn), lambda i, j: (i, j)),
        compiler_params=_cparams(("parallel", "arbitrary")),
        name="inproj",
    )(h, w_in, cos, sin)


def _softplus(x):
    return jnp.maximum(x, 0.0) + jnp.log1p(jnp.exp(-jnp.abs(x)))


def _dt_kernel(h_ref, w_ref, b_ref, a_ref, acc_ref, dtr_ref, acr_ref, *, q, seg):
    h = h_ref[...]
    tm = h.shape[0]
    dt_c = _softplus(_dot(h, w_ref[...]) + b_ref[...])
    dta_c = dt_c * (-jnp.exp(a_ref[...]) * LOG2E)
    i = lax.broadcasted_iota(jnp.int32, (q, q), 0)
    j = lax.broadcasted_iota(jnp.int32, (q, q), 1)
    tri = jnp.where(((i // seg) == (j // seg)) & (j <= i), 1.0, 0.0).astype(F32)
    for ci in range(tm // q):
        lo_r, hi_r = ci * q, (ci + 1) * q
        ac_c = jnp.dot(tri, dta_c[lo_r:hi_r, :], preferred_element_type=F32, precision=lax.Precision.HIGHEST)
        ac_r = ac_c.T
        dt_r = dt_c[lo_r:hi_r, :].T
        for g in range(SSM_GROUPS):
            lo = g * SSM_HPG
            acc_ref[g, lo_r:hi_r, :] = ac_c[:, lo:lo + SSM_HPG]
            dtr_ref[g, ci] = dt_r[lo:lo + SSM_HPG, :]
            acr_ref[g, ci] = ac_r[lo:lo + SSM_HPG, :]


def _dt_proj(h, w_dt, dt_bias, a_log, *, tm, q, seg):
    n, d = h.shape
    hh = SSM_HEADS
    col = jax.ShapeDtypeStruct((SSM_GROUPS, n, SSM_HPG), F32)
    row = jax.ShapeDtypeStruct((SSM_GROUPS, n // q, SSM_HPG, q), F32)
    return pl.pallas_call(
        functools.partial(_dt_kernel, q=q, seg=seg),
        out_shape=(col, row, row),
        grid=(n // tm,),
        in_specs=[
            pl.BlockSpec((tm, d), lambda i: (i, 0)),
            pl.BlockSpec((d, hh), lambda i: (0, 0)),
            pl.BlockSpec((1, hh), lambda i: (0, 0)),
            pl.BlockSpec((1, hh), lambda i: (0, 0)),
        ],
        out_specs=(
            pl.BlockSpec((SSM_GROUPS, tm, SSM_HPG), lambda i: (0, i, 0)),
            pl.BlockSpec((SSM_GROUPS, tm // q, SSM_HPG, q), lambda i: (0, i, 0, 0)),
            pl.BlockSpec((SSM_GROUPS, tm // q, SSM_HPG, q), lambda i: (0, i, 0, 0)),
        ),
        compiler_params=_cparams(("parallel",)),
        name="dt_proj",
    )(h, w_dt, dt_bias.reshape(1, hh), a_log.reshape(1, hh))


def _ret_kernel(*refs, c, seg, nck, has_s0):
    if has_s0:
        (q_ref, k_ref, v_ref, rg_ref, dm_ref, qd_ref, kd_ref, cd_ref, gg_ref, gb_ref, s0_ref,
         o_ref, s_ref) = refs
    else:
        (q_ref, k_ref, v_ref, rg_ref, dm_ref, qd_ref, kd_ref, cd_ref, gg_ref, gb_ref,
         o_ref, s_ref) = refs
    t = pl.program_id(2)
    spc = c // seg

    @pl.when(t == 0)
    def _():
        if has_s0:
            s_ref[...] = s0_ref[...]
        else:
            s_ref[...] = jnp.zeros_like(s_ref)

    dm = dm_ref[0]
    qd = qd_ref[0]
    cd = cd_ref[0]
    gg = gg_ref[...]
    gb = gb_ref[...]
    for ci in range(nck):
        sl = pl.ds(ci * c, c)
        q = q_ref[sl, :]
        k = k_ref[sl, :]
        v = v_ref[sl, :]
        att = _dot_nt(q, k) * dm
        o = _dot(att.astype(BF16), v)
        kf = k.astype(F32)
        inter = []
        for slot in range(spc):
            sidx = (ci * spc + slot) if spc > 1 else 0
            st = s_ref[sidx, 0]
            qs = q if spc == 1 else q[slot * seg:(slot + 1) * seg, :]
            inter.append(_dot(qs, st.astype(BF16)))
            s_ref[sidx, 0] = st * cd + _dot_tn((kf * kd_ref[0, slot]).astype(BF16), v)
        o = o + (inter[0] if spc == 1 else jnp.concatenate(inter, axis=0)) * qd
        mu = jnp.mean(o, axis=-1, keepdims=True)
        dev = o - mu
        var = jnp.mean(dev * dev, axis=-1, keepdims=True)
        on = dev * lax.rsqrt(var + EPS)
        rg = rg_ref[sl, :].astype(F32)
        o_ref[sl, :] = (_silu(rg) * (on * gg + gb)).astype(BF16)


def _ret_tables(c, seg):
    spc = c // seg
    hs = np.arange(RET_HEADS, dtype=np.float64)
    log_g = np.log1p(-np.exp2(-5.0 - hs))
    idx = np.arange(c)
    pos = (idx % seg).astype(np.float64)
    slot = idx // seg
    diff = pos[:, None] - pos[None, :]
    live = (slot[:, None] == slot[None, :]) & (diff >= 0)
    dmat = np.where(live[None], np.exp(np.maximum(diff, 0.0)[None] * log_g[:, None, None]), 0.0)
    q_dec = np.exp((pos[None, :] + 1.0) * log_g[:, None])
    k_dec = np.exp((seg - 1.0 - pos[None, :]) * log_g[:, None])
    c_dec = np.exp(seg * log_g)
    k_slot = np.where(slot[None, None, :] == np.arange(spc)[None, :, None], k_dec[:, None, :], 0.0)
    qd = np.broadcast_to(q_dec[:, :, None], (RET_HEADS, c, RET_DV))
    kd = np.broadcast_to(k_slot[:, :, :, None], (RET_HEADS, spc, c, RET_DK))
    cd = np.broadcast_to(c_dec[:, None, None], (RET_HEADS, 1, RET_DV))
    f = lambda a: jnp.asarray(np.ascontiguousarray(a), dtype=F32)
    return f(dmat), f(qd), f(kd), f(cd)


def _retention(big, gn_g, gn_b, s0, *, nseq, seqlen, c, seg, nck):
    n = big.shape[0]
    spc = c // seg
    rb = nck * c
    ns_step = spc * nck if spc > 1 else 1
    nt = 1 if spc > 1 else seqlen // rb
    dm, qd, kd, cd = _ret_tables(c, seg)
    has_s0 = s0 is not None
    kq, kk = COL_Q // RET_DK, COL_K // RET_DK
    kv, kr = COL_V // RET_DV, COL_RG // RET_DV
    row = lambda b, h, t: b * nt + t
    in_specs = [
        pl.BlockSpec((rb, RET_DK), lambda b, h, t: (row(b, h, t), kq + h)),
        pl.BlockSpec((rb, RET_DK), lambda b, h, t: (row(b, h, t), kk + h)),
        pl.BlockSpec((rb, RET_DV), lambda b, h, t: (row(b, h, t), kv + h)),
        pl.BlockSpec((rb, RET_DV), lambda b, h, t: (row(b, h, t), kr + h)),
        pl.BlockSpec((1, c, c), lambda b, h, t: (h, 0, 0)),
        pl.BlockSpec((1, c, RET_DV), lambda b, h, t: (h, 0, 0)),
        pl.BlockSpec((1, spc, c, RET_DK), lambda b, h, t: (h, 0, 0, 0)),
        pl.BlockSpec((1, 1, RET_DV), lambda b, h, t: (h, 0, 0)),
        pl.BlockSpec((1, RET_DV), lambda b, h, t: (0, h)),
        pl.BlockSpec((1, RET_DV), lambda b, h, t: (0, h)),
    ]
    args = [big, big, big, big, dm, qd, kd, cd, gn_g.reshape(1, RET_V), gn_b.reshape(1, RET_V)]
    st_spec = pl.BlockSpec((ns_step, 1, RET_DK, RET_DV), lambda b, h, t: (b, h, 0, 0))
    if has_s0:
        in_specs.append(st_spec)
        args.append(s0)
    return pl.pallas_call(
        functools.partial(_ret_kernel, c=c, seg=seg, nck=nck, has_s0=has_s0),
        out_shape=(jax.ShapeDtypeStruct((n, RET_V), BF16),
                   jax.ShapeDtypeStruct((nseq, RET_HEADS, RET_DK, RET_DV), F32)),
        grid=(nseq // ns_step, RET_HEADS, nt),
        in_specs=in_specs,
        out_specs=(pl.BlockSpec((rb, RET_DV), lambda b, h, t: (row(b, h, t), h)), st_spec),
        compiler_params=_cparams(("parallel", "parallel", "arbitrary")),
        name="retention",
    )(*args)


def _conv_silu(raw_scr, cw, bias, tb):
    taps = SSM_CONV
    acc = bias
    for s in range(taps):
        acc = acc + raw_scr[pl.ds(SUBLANES - s, tb), :] * cw[taps - 1 - s:taps - s, :]
    return _silu(acc)


def _bf16_terms(v):
    t1 = v.astype(BF16).astype(F32)
    r1 = v - t1
    t2 = r1.astype(BF16).astype(F32)
    t3 = (r1 - t2).astype(BF16).astype(F32)
    return t1, t2, t3


def _expand_matrix(heads, width):
    k = lax.broadcasted_iota(jnp.int32, (3 * heads, heads * width), 0) % heads
    c = lax.broadcasted_iota(jnp.int32, (3 * heads, heads * width), 1) // width
    return jnp.where(k == c, 1.0, 0.0).astype(F32)


def _head_expand(v, expand):
    return _dot_tn(jnp.concatenate(_bf16_terms(v), axis=0), expand)


def _ssd_kernel(*refs, q, seg, nck, has_s0):
    if has_s0:
        (xs_ref, b_ref, c_ref, z_ref, acc_ref, dtr_ref, acr_ref,
         cwx_ref, cwb_ref, cwc_ref, cbx_ref, cbb_ref, cbc_ref, dsk_ref, ng_ref,
         s0_ref, hx0_ref, hb0_ref, hc0_ref,
         y_ref, so_ref, hx, hb, hc, st_scr) = refs
        hist0 = (hx0_ref, hb0_ref, hc0_ref)
    else:
        (xs_ref, b_ref, c_ref, z_ref, acc_ref, dtr_ref, acr_ref,
         cwx_ref, cwb_ref, cwc_ref, cbx_ref, cbb_ref, cbc_ref, dsk_ref, ng_ref,
         y_ref, so_ref, hx, hb, hc, st_scr) = refs
        hist0 = (None, None, None)
    t = pl.program_id(2)
    spc = q // seg
    ns_step = st_scr.shape[0]
    conv_in = ((xs_ref, hx, hist0[0], cwx_ref, cbx_ref),
               (b_ref, hb, hist0[1], cwb_ref, cbb_ref),
               (c_ref, hc, hist0[2], cwc_ref, cbc_ref))

    def first_rows(raw_scr, h0_ref, sidx):
        if has_s0:
            raw_scr[0:SUBLANES, :] = h0_ref[sidx]
        else:
            raw_scr[0:SUBLANES, :] = jnp.zeros((SUBLANES, raw_scr.shape[1]), F32)

    @pl.when(t == 0)
    def _():
        for sidx in range(ns_step):
            if has_s0:
                st_scr[sidx] = s0_ref[sidx, 0].T
            else:
                st_scr[sidx] = jnp.zeros(st_scr.shape[1:], F32)
        if spc == 1:
            for _, raw_scr, h0_ref, _, _ in conv_in:
                first_rows(raw_scr, h0_ref, 0)

    ii = lax.broadcasted_iota(jnp.int32, (q, q), 0)
    jj = lax.broadcasted_iota(jnp.int32, (q, q), 1)
    causal = (ii >= jj) if spc == 1 else ((ii >= jj) & ((ii // seg) == (jj // seg)))
    lane = lax.broadcasted_iota(jnp.int32, (q, 2 * HALF_LANES), 1)
    row_slot = lax.broadcasted_iota(jnp.int32, (q, SSM_STATE), 0) // seg
    tok = lax.broadcasted_iota(jnp.int32, (SSM_HPG, q), 1)
    dsk = dsk_ref[...]
    ng = ng_ref[...]
    expand = _expand_matrix(SSM_HPG, SSM_HEADDIM)

    def conv_chunk(ci, sl):
        out = []
        for raw_ref, raw_scr, h0_ref, cw_ref, cb_ref in conv_in:
            if spc == 1:
                raw_scr[SUBLANES:SUBLANES + q, :] = raw_ref[sl, :].astype(F32)
                out.append(_conv_silu(raw_scr, cw_ref[...], cb_ref[...], q))
                raw_scr[0:SUBLANES, :] = raw_scr[q:q + SUBLANES, :]
            else:
                pieces = []
                for slot in range(spc):
                    first_rows(raw_scr, h0_ref, ci * spc + slot)
                    raw_scr[SUBLANES:SUBLANES + seg, :] = raw_ref[pl.ds(ci * q + slot * seg, seg), :].astype(F32)
                    pieces.append(_conv_silu(raw_scr, cw_ref[...], cb_ref[...], seg))
                out.append(jnp.concatenate(pieces, axis=0))
        return out

    def chunk(ci):
        sl = pl.ds(pl.multiple_of(ci * q, q), q)
        x, b_f32, c_f32 = conv_chunk(ci, sl)
        bq = b_f32.astype(BF16)
        cq = c_f32.astype(BF16)
        dtr = dtr_ref[0, ci]
        acr = acr_ref[0, ci]
        acc = acc_ref[0, sl, :]

        cb = jnp.where(causal, _dot_nt(cq, bq), 0.0)
        pairs = []
        for m in range(SSM_HPG // 2):
            ws = []
            for r in (2 * m, 2 * m + 1):
                seg_sum = acc[:, r:r + 1] - acr[r:r + 1, :]
                ws.append((jnp.exp2(jnp.minimum(seg_sum, 0.0)) * (cb * dtr[r:r + 1, :])).astype(BF16))
            xp = x[:, m * 2 * HALF_LANES:(m + 1) * 2 * HALF_LANES]
            x_lo = jnp.where(lane < HALF_LANES, xp, 0.0).astype(BF16)
            x_hi = jnp.where(lane >= HALF_LANES, xp, 0.0).astype(BF16)
            pairs.append(_dot(jnp.concatenate(ws, axis=1), jnp.concatenate([x_lo, x_hi], axis=0)))
        y = jnp.concatenate(pairs, axis=1)

        a_last = acr[:, q - 1:q]
        for slot in range(spc - 2, -1, -1):
            end = (slot + 1) * seg
            a_last = jnp.where(tok < end, acr[:, end - 1:end], a_last)
        ea = _head_expand(jnp.exp2(acr), expand)
        xw = (x * _head_expand(jnp.exp2(a_last - acr) * dtr, expand)).astype(BF16)
        inter = []
        for slot in range(spc):
            sidx = (ci * spc + slot) if spc > 1 else 0
            state = st_scr[sidx]
            end = (slot + 1) * seg
            cs = cq if spc == 1 else cq[slot * seg:end, :]
            bs = bq if spc == 1 else jnp.where(row_slot == slot, b_f32, 0.0).astype(BF16)
            inter.append(_dot(cs, state.astype(BF16)))
            st_scr[sidx] = state * ea[end - 1:end, :] + _dot_tn(bs, xw)
        y = y + (inter[0] if spc == 1 else jnp.concatenate(inter, axis=0)) * ea

        y = y + dsk * x
        y = y * _silu(z_ref[sl, :].astype(F32))
        ms = jnp.mean(y * y, axis=-1, keepdims=True)
        y_ref[sl, :] = (y * lax.rsqrt(ms + EPS) * ng).astype(BF16)

    if spc > 1 or nck == 1:
        for ci in range(nck):
            chunk(ci)
    else:
        lax.fori_loop(0, nck, lambda ci, carry: (chunk(ci), carry)[1], 0)

    @pl.when(t == pl.num_programs(2) - 1)
    def _():
        for sidx in range(ns_step):
            so_ref[sidx, 0] = st_scr[sidx].T


def _ssd(big, dts, conv_w, conv_b, d_skip_x, norm_g, s0, hist0, *, nseq, seqlen, q, seg, nck):
    n = big.shape[0]
    spc = q // seg
    rb = nck * q
    ns_step = spc * nck if spc > 1 else 1
    nt = 1 if spc > 1 else seqlen // rb
    has_s0 = s0 is not None
    acc, dtr, acr = dts
    gw, ns = SSM_GW, SSM_STATE
    kz, kx = COL_Z // gw, COL_XBC // gw
    kb = (COL_XBC + SSM_INNER) // ns
    kc = kb + SSM_GROUPS
    cb_b = SSM_INNER // ns
    cb_c = cb_b + SSM_GROUPS
    row = lambda b, g, t: b * nt + t
    in_specs = [
        pl.BlockSpec((rb, gw), lambda b, g, t: (row(b, g, t), kx + g)),
        pl.BlockSpec((rb, ns), lambda b, g, t: (row(b, g, t), kb + g)),
        pl.BlockSpec((rb, ns), lambda b, g, t: (row(b, g, t), kc + g)),
        pl.BlockSpec((rb, gw), lambda b, g, t: (row(b, g, t), kz + g)),
        pl.BlockSpec((1, rb, SSM_HPG), lambda b, g, t: (g, row(b, g, t), 0)),
        pl.BlockSpec((1, nck, SSM_HPG, q), lambda b, g, t: (g, row(b, g, t), 0, 0)),
        pl.BlockSpec((1, nck, SSM_HPG, q), lambda b, g, t: (g, row(b, g, t), 0, 0)),
        pl.BlockSpec((SSM_CONV, gw), lambda b, g, t: (0, g)),
        pl.BlockSpec((SSM_CONV, ns), lambda b, g, t: (0, cb_b + g)),
        pl.BlockSpec((SSM_CONV, ns), lambda b, g, t: (0, cb_c + g)),
        pl.BlockSpec((1, gw), lambda b, g, t: (0, g)),
        pl.BlockSpec((1, ns), lambda b, g, t: (0, cb_b + g)),
        pl.BlockSpec((1, ns), lambda b, g, t: (0, cb_c + g)),
        pl.BlockSpec((1, gw), lambda b, g, t: (0, g)),
        pl.BlockSpec((1, gw), lambda b, g, t: (0, g)),
    ]
    cbias = conv_b.reshape(1, SSM_CONV_DIM)
    args = [big, big, big, big, acc, dtr, acr, conv_w, conv_w, conv_w, cbias, cbias, cbias,
            d_skip_x, norm_g.reshape(1, SSM_INNER)]
    st_spec = pl.BlockSpec((ns_step, 1, gw, ns), lambda b, g, t: (b, g, 0, 0))
    if has_s0:
        in_specs += [
            st_spec,
            pl.BlockSpec((ns_step, SUBLANES, gw), lambda b, g, t: (b, 0, g)),
            pl.BlockSpec((ns_step, SUBLANES, ns), lambda b, g, t: (b, 0, cb_b + g)),
            pl.BlockSpec((ns_step, SUBLANES, ns), lambda b, g, t: (b, 0, cb_c + g)),
        ]
        args += [s0, hist0, hist0, hist0]
    return pl.pallas_call(
        functools.partial(_ssd_kernel, q=q, seg=seg, nck=nck, has_s0=has_s0),
        out_shape=(jax.ShapeDtypeStruct((n, SSM_INNER), BF16),
                   jax.ShapeDtypeStruct((nseq, SSM_GROUPS, gw, ns), F32)),
        grid=(nseq // ns_step, SSM_GROUPS, nt),
        in_specs=in_specs,
        out_specs=(pl.BlockSpec((rb, gw), lambda b, g, t: (row(b, g, t), g)), st_spec),
        scratch_shapes=[pltpu.VMEM((SUBLANES + q, gw), F32), pltpu.VMEM((SUBLANES + q, ns), F32),
                        pltpu.VMEM((SUBLANES + q, ns), F32), pltpu.VMEM((ns_step, ns, gw), F32)],
        compiler_params=_cparams(("parallel", "parallel", "arbitrary")),
        name="ssd",
    )(*args)


def _merge_kernel(h_ref, ret_ref, ssm_ref, wgr_ref, wgs_ref, wr_ref, ws_ref, o_ref):
    h = h_ref[...]
    gr = jax.nn.sigmoid(_dot(h, wgr_ref[...]))
    gs = jax.nn.sigmoid(_dot(h, wgs_ref[...]))
    a = _dot(ret_ref[...], wr_ref[...])
    b = _dot(ssm_ref[...], ws_ref[...])
    o_ref[...] = (gr * a + gs * b).astype(BF16)


def _merge(h, ret, ssm, w_gr, w_gs, w_r, w_s, *, tm, tn):
    n, d = h.shape
    kr, ks = ret.shape[1], ssm.shape[1]
    return pl.pallas_call(
        _merge_kernel,
        out_shape=jax.ShapeDtypeStruct((n, d), BF16),
        grid=(n // tm, d // tn),
        in_specs=[
            pl.BlockSpec((tm, d), lambda i, j: (i, 0)),
            pl.BlockSpec((tm, kr), lambda i, j: (i, 0)),
            pl.BlockSpec((tm, ks), lambda i, j: (i, 0)),
            pl.BlockSpec((d, tn), lambda i, j: (0, j)),
            pl.BlockSpec((d, tn), lambda i, j: (0, j)),
            pl.BlockSpec((kr, tn), lambda i, j: (0, j)),
            pl.BlockSpec((ks, tn), lambda i, j: (0, j)),
        ],
        out_specs=pl.BlockSpec((tm, tn), lambda i, j: (i, j)),
        compiler_params=_cparams(("parallel", "arbitrary")),
        name="merge",
    )(h, ret, ssm, w_gr, w_gs, w_r, w_s)


def _outproj_kernel(m_ref, w_ref, x_ref, o_ref, *, rc):
    for r in range(m_ref.shape[0] // rc):
        rows = pl.ds(r * rc, rc)
        o_ref[rows, :] = x_ref[rows, :] + _dot(m_ref[rows, :], w_ref[...])


def _outproj(m, w, x, *, tm, tn):
    n, d = x.shape
    k = m.shape[1]
    return pl.pallas_call(
        functools.partial(_outproj_kernel, rc=min(ROW_CHUNK, tm)),
        out_shape=jax.ShapeDtypeStruct((n, d), F32),
        grid=(n // tm, d // tn),
        in_specs=[
            pl.BlockSpec((tm, k), lambda i, j: (i, 0)),
            pl.BlockSpec((k, tn), lambda i, j: (0, j)),
            pl.BlockSpec((tm, tn), lambda i, j: (i, j)),
        ],
        out_specs=pl.BlockSpec((tm, tn), lambda i, j: (i, j)),
        compiler_params=_cparams(("parallel", "arbitrary")),
        name="outproj",
    )(m, w, x)


def _ple_kernel(x_ref, h_ref, p_ref, wp_ref, wg_ref, gf_ref, o_ref):
    pe = _dot(p_ref[...].astype(BF16), wp_ref[...])
    gt = jax.nn.sigmoid(_dot(h_ref[...], wg_ref[...]))
    x = x_ref[...] + pe * gt
    ms = jnp.mean(x * x, axis=-1, keepdims=True)
    o_ref[...] = x * lax.rsqrt(ms + EPS) * gf_ref[...]


def _ple(x, h, p, w_ple, w_gate, g_final, *, tm):
    n, d = x.shape
    pd = p.shape[1]
    return pl.pallas_call(
        _ple_kernel,
        out_shape=jax.ShapeDtypeStruct((n, d), F32),
        grid=(n // tm,),
        in_specs=[
            pl.BlockSpec((tm, d), lambda i: (i, 0)),
            pl.BlockSpec((tm, d), lambda i: (i, 0)),
            pl.BlockSpec((tm, pd), lambda i: (i, 0)),
            pl.BlockSpec((pd, d), lambda i: (0, 0)),
            pl.BlockSpec((d, d), lambda i: (0, 0)),
            pl.BlockSpec((1, d), lambda i: (0, 0)),
        ],
        out_specs=pl.BlockSpec((tm, d), lambda i: (i, 0)),
        compiler_params=_cparams(("parallel",)),
        name="ple",
    )(x, h, p, w_ple, w_gate, g_final)


def _rope_tables(pos0, seqlen, rows):
    half = RET_DK // 2
    inv = ROPE_THETA ** (-jnp.arange(half, dtype=F32) / half)
    pos = (pos0 + jnp.arange(seqlen, dtype=jnp.int32)).astype(F32)
    ang = pos[:, None] * inv[None, :]
    reps = max(rows // seqlen, 1)
    return jnp.tile(jnp.cos(ang), (reps, 1)), jnp.tile(jnp.sin(ang), (reps, 1))


def _pick(n, pref):
    t = pref
    while n % t:
        t //= 2
    return t


def _trunk(x, p, pos0, s_ret, s_ssm, s_conv, w, cfg):
    nseq, seqlen, d = x.shape
    n = nseq * seqlen
    x = x.reshape(n, d)
    p = p.reshape(n, p.shape[-1])
    tm = _pick(n, cfg["tm"])

    x1, h = _ffn(x, w["g_ffn1"], w["g_mix"], w["w1_gu"], w["w1_down"], tm=_pick(n, cfg["tm_ffn"]), tf=cfg["tf"])

    cos, sin = _rope_tables(pos0, seqlen, tm)
    big = _inproj(h, w["w_in"], cos, sin, tm=tm, tn=cfg["tn_in"])

    q = cfg["q_ssd"]
    if seqlen % q == 0:
        seg, nck_ssd = q, _pick(seqlen, cfg["tb_ssd"]) // q
        c = cfg["c_ret"] if seqlen % cfg["c_ret"] == 0 else q
        seg_ret, nck_ret = c, _pick(seqlen, cfg["tb_ret"]) // c
    else:
        assert q % seqlen == 0 and n % q == 0, "short sequences must pack into whole chunks"
        c, seg, seg_ret = q, seqlen, seqlen
        nck_ssd = nck_ret = _pick(n // q, cfg["pack_chunks"])

    dts = _dt_proj(h, w["w_dt"], w["dt_bias"], w["a_log"], tm=_pick(n, 512), q=q, seg=seg)
    ret, s_ret_new = _retention(big, w["ret_gn_g"], w["ret_gn_b"], s_ret, nseq=nseq, seqlen=seqlen,
                                c=c, seg=seg_ret, nck=nck_ret)

    if s_ssm is not None:
        s0 = s_ssm.reshape(nseq, SSM_GROUPS, SSM_GW, SSM_STATE)
        hist0 = jnp.pad(s_conv, ((0, 0), (SUBLANES - (SSM_CONV - 1), 0), (0, 0)))
    else:
        s0, hist0 = None, None
    ssm, s_ssm_new = _ssd(big, dts, w["conv_w"], w["conv_b"], w["d_skip_x"], w["ssm_norm_g"], s0, hist0,
                          nseq=nseq, seqlen=seqlen, q=q, seg=seg, nck=nck_ssd)
    s_ssm_new = s_ssm_new.reshape(nseq, SSM_HEADS, SSM_HEADDIM, SSM_STATE)

    keep = min(seqlen, SSM_CONV - 1)
    xbc_tail = big.reshape(nseq, seqlen, N_BIG)[:, seqlen - keep:, COL_XBC:].astype(F32)
    if keep < SSM_CONV - 1:
        prev = jnp.zeros((nseq, SSM_CONV - 1, SSM_CONV_DIM), F32) if s_conv is None else s_conv.astype(F32)
        xbc_tail = jnp.concatenate([prev, xbc_tail], axis=1)[:, -(SSM_CONV - 1):]

    merged = _merge(h, ret, ssm, w["w_gr"], w["w_gs"], w["w_br_ret"], w["w_br_ssm"], tm=_pick(n, cfg["tm_mg"]), tn=cfg["tn_mg"])
    x2 = _outproj(merged, w["w_out"], x1, tm=tm, tn=cfg["tn_out"])
    x3, h3 = _ffn(x2, w["g_ffn2"], w["g_ple"], w["w2_gu"], w["w2_down"], tm=_pick(n, cfg["tm_ffn"]), tf=cfg["tf"])
    y = _ple(x3, h3, p, w["w_ple"], w["w_ple_gate"], w["g_final"], tm=_pick(n, cfg["tm_ple"]))
    return y.reshape(nseq, seqlen, d), s_ret_new, s_ssm_new, xbc_tail


CFG = dict(tm=1024, tm_ffn=512, tf=512, tn_in=2048, q_ssd=128, c_ret=256, tb_ret=1024, tb_ssd=1024, pack_chunks=2,
           tm_mg=512, tn_mg=512, tn_out=1024, tm_ple=512)


def _prep_weights(g_ffn1, w1_gu, w1_down, g_mix, w_in, ret_gn_g, ret_gn_b, conv_w, conv_b, dt_bias, a_log,
                  d_skip, ssm_norm_g, w_br_ret, w_br_ssm, w_out, g_ffn2, w2_gu, w2_down, g_ple, w_ple,
                  w_ple_gate, g_final):
    b = lambda a: a[0].astype(BF16)
    r = lambda a: a[0].reshape(1, -1).astype(F32)
    win = w_in[0]
    w_dt = win[:, COL_DT:COL_DT + SSM_HEADS].astype(BF16)
    return dict(
        g_ffn1=r(g_ffn1), w1_gu=b(w1_gu), w1_down=b(w1_down), g_mix=r(g_mix),
        w_in=win.astype(BF16), w_dt=w_dt,
        w_gr=win[:, COL_GR:COL_GR + D_MODEL].astype(BF16), w_gs=win[:, COL_GS:COL_GS + D_MODEL].astype(BF16),
        ret_gn_g=ret_gn_g[0].astype(F32), ret_gn_b=ret_gn_b[0].astype(F32),
        conv_w=conv_w[0].astype(F32), conv_b=conv_b[0].astype(F32),
        dt_bias=dt_bias[0].astype(F32), a_log=a_log[0].astype(F32),
        d_skip_x=jnp.repeat(d_skip[0].astype(F32), SSM_HEADDIM).reshape(1, SSM_INNER),
        ssm_norm_g=ssm_norm_g[0].astype(F32),
        w_br_ret=b(w_br_ret), w_br_ssm=b(w_br_ssm), w_out=b(w_out),
        g_ffn2=r(g_ffn2), w2_gu=b(w2_gu), w2_down=b(w2_down), g_ple=r(g_ple),
        w_ple=b(w_ple), w_ple_gate=b(w_ple_gate), g_final=g_final.reshape(1, -1).astype(F32),
    )


def kernel(x_prompt, x_sample, state_ret, state_ssm, state_conv, p_prompt, p_sample, g_ffn1, w1_gu, w1_down, g_mix, w_in, ret_gn_g, ret_gn_b, conv_w, conv_b, dt_bias, a_log, d_skip, ssm_norm_g, w_br_ret, w_br_ssm, w_out, g_ffn2, w2_gu, w2_down, g_ple, w_ple, w_ple_gate, g_final):
    assert g_ffn1.shape[0] == 1, "single-layer trunk"
    w = _prep_weights(g_ffn1, w1_gu, w1_down, g_mix, w_in, ret_gn_g, ret_gn_b, conv_w, conv_b, dt_bias, a_log,
                      d_skip, ssm_norm_g, w_br_ret, w_br_ssm, w_out, g_ffn2, w2_gu, w2_down, g_ple, w_ple,
                      w_ple_gate, g_final)
    y_p, ret_p, ssm_p, conv_p = _trunk(x_prompt, p_prompt[0], 0, None, None, None, w, CFG)
    y_s, ret_s, ssm_s, conv_s = _trunk(x_sample, p_sample[0], PAST_LEN, state_ret[0], state_ssm[0],
                                       state_conv[0], w, CFG)
    e = lambda a: a[None]
    return (y_p, y_s, e(ret_p), e(ssm_p), e(conv_p), e(ret_s), e(ssm_s), e(conv_s))
```

```python
import functools

import numpy as np
import jax
import jax.numpy as jnp
from jax import lax
from jax.experimental import pallas as pl
from jax.experimental.pallas import tpu as pltpu

F32 = jnp.float32
BF16 = jnp.bfloat16

D_MODEL = 2048
PAST_LEN = 4096
EPS = 1e-6
LOG2E = 1.4426950408889634
RET_HEADS = 8
RET_DK = 256
RET_DV = 512
RET_QK = RET_HEADS * RET_DK
RET_V = RET_HEADS * RET_DV
ROPE_THETA = 10000.0
SSM_INNER = 2 * D_MODEL
SSM_HEADDIM = 64
SSM_HEADS = SSM_INNER // SSM_HEADDIM
SSM_GROUPS = 8
SSM_HPG = SSM_HEADS // SSM_GROUPS
SSM_GW = SSM_HPG * SSM_HEADDIM
SSM_STATE = 128
SSM_CONV = 4
SSM_CONV_DIM = SSM_INNER + 2 * SSM_GROUPS * SSM_STATE
PLE_DIM = 256

COL_Q = 0
COL_K = COL_Q + RET_QK
COL_V = COL_K + RET_QK
COL_RG = COL_V + RET_V
COL_Z = COL_RG + RET_V
COL_XBC = COL_Z + SSM_INNER
N_BIG = COL_XBC + SSM_CONV_DIM
COL_DT = N_BIG
COL_GR = COL_DT + SSM_HEADS
COL_GS = COL_GR + D_MODEL

SUBLANES = 8
HALF_LANES = 64
VMEM_LIMIT = 56 * 1024 * 1024
FFN_COLS = 256
NORM_ROWS = 128
ROW_CHUNK = 256


def _cparams(sem):
    return pltpu.CompilerParams(dimension_semantics=sem, vmem_limit_bytes=VMEM_LIMIT)


def _rms_bf16(x, g):
    ms = jnp.mean(x * x, axis=-1, keepdims=True)
    return (x * lax.rsqrt(ms + EPS) * g).astype(BF16)


def _silu(x):
    return x * jax.nn.sigmoid(x)


def _dot(a, b):
    return jnp.dot(a, b, preferred_element_type=F32)


def _dot_nt(a, b):
    return lax.dot_general(a, b, (((1,), (1,)), ((), ())), preferred_element_type=F32)


def _dot_tn(a, b):
    return lax.dot_general(a, b, (((0,), (0,)), ((), ())), preferred_element_type=F32)


def _ffn_kernel(x_ref, g_ref, gn_ref, wg_ref, wu_ref, wd_ref, o_ref, hn_ref, h_scr, *, nj):
    j = pl.program_id(1)

    tm = x_ref.shape[0]

    @pl.when(j == 0)
    def _():
        for r in range(0, tm, NORM_ROWS):
            h_scr[r:r + NORM_ROWS, :] = _rms_bf16(x_ref[r:r + NORM_ROWS, :], g_ref[...])
        o_ref[...] = jnp.zeros_like(o_ref)

    h = h_scr[...]
    tf = wg_ref.shape[1]
    acts = []
    for lo in range(0, tf, FFN_COLS):
        hi = min(lo + FFN_COLS, tf)
        gate = _dot(h, wg_ref[:, lo:hi])
        up = _dot(h, wu_ref[:, lo:hi])
        acts.append((_silu(gate) * up).astype(BF16))
    act = jnp.concatenate(acts, axis=1)
    d = o_ref.shape[1]
    for lo in range(0, d, d // 4):
        o_ref[:, lo:lo + d // 4] += _dot(act, wd_ref[:, lo:lo + d // 4])

    @pl.when(j == nj - 1)
    def _():
        for r in range(0, tm, NORM_ROWS):
            xn = x_ref[r:r + NORM_ROWS, :] + 0.5 * o_ref[r:r + NORM_ROWS, :]
            o_ref[r:r + NORM_ROWS, :] = xn
            hn_ref[r:r + NORM_ROWS, :] = _rms_bf16(xn, gn_ref[...])


def _ffn(x, g, g_next, w_gu, w_down, *, tm, tf):
    n, d = x.shape
    f = w_down.shape[0]
    nj = f // tf
    return pl.pallas_call(
        functools.partial(_ffn_kernel, nj=nj),
        out_shape=(jax.ShapeDtypeStruct((n, d), F32), jax.ShapeDtypeStruct((n, d), BF16)),
        grid=(n // tm, nj),
        in_specs=[
            pl.BlockSpec((tm, d), lambda i, j: (i, 0)),
            pl.BlockSpec((1, d), lambda i, j: (0, 0)),
            pl.BlockSpec((1, d), lambda i, j: (0, 0)),
            pl.BlockSpec((d, tf), lambda i, j: (0, j)),
            pl.BlockSpec((d, tf), lambda i, j: (0, j + nj)),
            pl.BlockSpec((tf, d), lambda i, j: (j, 0)),
        ],
        out_specs=(pl.BlockSpec((tm, d), lambda i, j: (i, 0)),
                   pl.BlockSpec((tm, d), lambda i, j: (i, 0))),
        scratch_shapes=[pltpu.VMEM((tm, d), BF16)],
        compiler_params=_cparams(("parallel", "arbitrary")),
        name="ffn",
    )(x, g, g_next, w_gu, w_gu, w_down)


def _inproj_kernel(h_ref, w_ref, cos_ref, sin_ref, o_ref, *, tn, nq, nrope, rc):
    j = pl.program_id(1)
    nr = h_ref.shape[0] // rc

    @pl.when(j >= nrope)
    def _():
        for r in range(nr):
            rows = pl.ds(r * rc, rc)
            o_ref[rows, :] = _dot(h_ref[rows, :], w_ref[...]).astype(BF16)

    @pl.when(j < nrope)
    def _():
        scale = jnp.where(j >= nq, RET_DK ** -0.5, 1.0).astype(F32)
        half = RET_DK // 2
        for r in range(nr):
            rows = pl.ds(r * rc, rc)
            acc = _dot(h_ref[rows, :], w_ref[...])
            c = cos_ref[rows, :] * scale
            s = sin_ref[rows, :] * scale
            for hh in range(tn // RET_DK):
                lo = hh * RET_DK
                x1 = acc[:, lo:lo + half]
                x2 = acc[:, lo + half:lo + RET_DK]
                o_ref[rows, lo:lo + half] = (x1 * c - x2 * s).astype(BF16)
                o_ref[rows, lo + half:lo + RET_DK] = (x2 * c + x1 * s).astype(BF16)


def _inproj(h, w_in, cos, sin, *, tm, tn):
    n, d = h.shape
    nrow = cos.shape[0] // tm
    return pl.pallas_call(
        functools.partial(_inproj_kernel, tn=tn, nq=RET_QK // tn, nrope=2 * RET_QK // tn, rc=min(ROW_CHUNK, tm)),
        out_shape=jax.ShapeDtypeStruct((n, N_BIG), BF16),
        grid=(n // tm, N_BIG // tn),
        in_specs=[
            pl.BlockSpec((tm, d), lambda i, j: (i, 0)),
            pl.BlockSpec((d, tn), lambda i, j: (0, j)),
            pl.BlockSpec((tm, RET_DK // 2), lambda i, j: (i % nrow, 0)),
            pl.BlockSpec((tm, RET_DK // 2), lambda i, j: (i % nrow, 0)),
        ],
        out_specs=pl.BlockSpec((tm, tn), lambda i, j: (i, j)),
        compiler_params=_cparams(("parallel", "arbitrary")),
        name="inproj",
    )(h, w_in, cos, sin)


def _softplus(x):
    return jnp.maximum(x, 0.0) + jnp.log1p(jnp.exp(-jnp.abs(x)))


def _dt_kernel(h_ref, w_ref, b_ref, a_ref, acc_ref, dtr_ref, acr_ref, *, q, seg):
    h = h_ref[...]
    tm = h.shape[0]
    dt_c = _softplus(_dot(h, w_ref[...]) + b_ref[...])
    dta_c = dt_c * (-jnp.exp(a_ref[...]) * LOG2E)
    i = lax.broadcasted_iota(jnp.int32, (q, q), 0)
    j = lax.broadcasted_iota(jnp.int32, (q, q), 1)
    tri = jnp.where(((i // seg) == (j // seg)) & (j <= i), 1.0, 0.0).astype(F32)
    for ci in range(tm // q):
        lo_r, hi_r = ci * q, (ci + 1) * q
        ac_c = jnp.dot(tri, dta_c[lo_r:hi_r, :], preferred_element_type=F32, precision=lax.Precision.HIGHEST)
        ac_r = ac_c.T
        dt_r = dt_c[lo_r:hi_r, :].T
        for g in range(SSM_GROUPS):
            lo = g * SSM_HPG
            acc_ref[g, lo_r:hi_r, :] = ac_c[:, lo:lo + SSM_HPG]
            dtr_ref[g, ci] = dt_r[lo:lo + SSM_HPG, :]
            acr_ref[g, ci] = ac_r[lo:lo + SSM_HPG, :]


def _dt_proj(h, w_dt, dt_bias, a_log, *, tm, q, seg):
    n, d = h.shape
    hh = SSM_HEADS
    col = jax.ShapeDtypeStruct((SSM_GROUPS, n, SSM_HPG), F32)
    row = jax.ShapeDtypeStruct((SSM_GROUPS, n // q, SSM_HPG, q), F32)
    return pl.pallas_call(
        functools.partial(_dt_kernel, q=q, seg=seg),
        out_shape=(col, row, row),
        grid=(n // tm,),
        in_specs=[
            pl.BlockSpec((tm, d), lambda i: (i, 0)),
            pl.BlockSpec((d, hh), lambda i: (0, 0)),
            pl.BlockSpec((1, hh), lambda i: (0, 0)),
            pl.BlockSpec((1, hh), lambda i: (0, 0)),
        ],
        out_specs=(
            pl.BlockSpec((SSM_GROUPS, tm, SSM_HPG), lambda i: (0, i, 0)),
            pl.BlockSpec((SSM_GROUPS, tm // q, SSM_HPG, q), lambda i: (0, i, 0, 0)),
            pl.BlockSpec((SSM_GROUPS, tm // q, SSM_HPG, q), lambda i: (0, i, 0, 0)),
        ),
        compiler_params=_cparams(("parallel",)),
        name="dt_proj",
    )(h, w_dt, dt_bias.reshape(1, hh), a_log.reshape(1, hh))


def _ret_kernel(*refs, c, seg, nck, has_s0):
    if has_s0:
        (q_ref, k_ref, v_ref, rg_ref, dm_ref, qd_ref, kd_ref, cd_ref, gg_ref, gb_ref, s0_ref,
         o_ref, s_ref) = refs
    else:
        (q_ref, k_ref, v_ref, rg_ref, dm_ref, qd_ref, kd_ref, cd_ref, gg_ref, gb_ref,
         o_ref, s_ref) = refs
    t = pl.program_id(2)
    spc = c // seg

    @pl.when(t == 0)
    def _():
        if has_s0:
            s_ref[...] = s0_ref[...]
        else:
            s_ref[...] = jnp.zeros_like(s_ref)

    dm = dm_ref[0]
    qd = qd_ref[0]
    cd = cd_ref[0]
    gg = gg_ref[...]
    gb = gb_ref[...]
    for ci in range(nck):
        sl = pl.ds(ci * c, c)
        q = q_ref[sl, :]
        k = k_ref[sl, :]
        v = v_ref[sl, :]
        att = _dot_nt(q, k) * dm
        o = _dot(att.astype(BF16), v)
        kf = k.astype(F32)
        inter = []
        for slot in range(spc):
            sidx = (ci * spc + slot) if spc > 1 else 0
            st = s_ref[sidx, 0]
            qs = q if spc == 1 else q[slot * seg:(slot + 1) * seg, :]
            inter.append(_dot(qs, st.astype(BF16)))
            s_ref[sidx, 0] = st * cd + _dot_tn((kf * kd_ref[0, slot]).astype(BF16), v)
        o = o + (inter[0] if spc == 1 else jnp.concatenate(inter, axis=0)) * qd
        mu = jnp.mean(o, axis=-1, keepdims=True)
        dev = o - mu
        var = jnp.mean(dev * dev, axis=-1, keepdims=True)
        on = dev * lax.rsqrt(var + EPS)
        rg = rg_ref[sl, :].astype(F32)
        o_ref[sl, :] = (_silu(rg) * (on * gg + gb)).astype(BF16)


def _ret_tables(c, seg):
    spc = c // seg
    hs = np.arange(RET_HEADS, dtype=np.float64)
    log_g = np.log1p(-np.exp2(-5.0 - hs))
    idx = np.arange(c)
    pos = (idx % seg).astype(np.float64)
    slot = idx // seg
    diff = pos[:, None] - pos[None, :]
    live = (slot[:, None] == slot[None, :]) & (diff >= 0)
    dmat = np.where(live[None], np.exp(np.maximum(diff, 0.0)[None] * log_g[:, None, None]), 0.0)
    q_dec = np.exp((pos[None, :] + 1.0) * log_g[:, None])
    k_dec = np.exp((seg - 1.0 - pos[None, :]) * log_g[:, None])
    c_dec = np.exp(seg * log_g)
    k_slot = np.where(slot[None, None, :] == np.arange(spc)[None, :, None], k_dec[:, None, :], 0.0)
    qd = np.broadcast_to(q_dec[:, :, None], (RET_HEADS, c, RET_DV))
    kd = np.broadcast_to(k_slot[:, :, :, None], (RET_HEADS, spc, c, RET_DK))
    cd = np.broadcast_to(c_dec[:, None, None], (RET_HEADS, 1, RET_DV))
    f = lambda a: jnp.asarray(np.ascontiguousarray(a), dtype=F32)
    return f(dmat), f(qd), f(kd), f(cd)


def _retention(big, gn_g, gn_b, s0, *, nseq, seqlen, c, seg, nck):
    n = big.shape[0]
    spc = c // seg
    rb = nck * c
    ns_step = spc * nck if spc > 1 else 1
    nt = 1 if spc > 1 else seqlen // rb
    dm, qd, kd, cd = _ret_tables(c, seg)
    has_s0 = s0 is not None
    kq, kk = COL_Q // RET_DK, COL_K // RET_DK
    kv, kr = COL_V // RET_DV, COL_RG // RET_DV
    row = lambda b, h, t: b * nt + t
    in_specs = [
        pl.BlockSpec((rb, RET_DK), lambda b, h, t: (row(b, h, t), kq + h)),
        pl.BlockSpec((rb, RET_DK), lambda b, h, t: (row(b, h, t), kk + h)),
        pl.BlockSpec((rb, RET_DV), lambda b, h, t: (row(b, h, t), kv + h)),
        pl.BlockSpec((rb, RET_DV), lambda b, h, t: (row(b, h, t), kr + h)),
        pl.BlockSpec((1, c, c), lambda b, h, t: (h, 0, 0)),
        pl.BlockSpec((1, c, RET_DV), lambda b, h, t: (h, 0, 0)),
        pl.BlockSpec((1, spc, c, RET_DK), lambda b, h, t: (h, 0, 0, 0)),
        pl.BlockSpec((1, 1, RET_DV), lambda b, h, t: (h, 0, 0)),
        pl.BlockSpec((1, RET_DV), lambda b, h, t: (0, h)),
        pl.BlockSpec((1, RET_DV), lambda b, h, t: (0, h)),
    ]
    args = [big, big, big, big, dm, qd, kd, cd, gn_g.reshape(1, RET_V), gn_b.reshape(1, RET_V)]
    st_spec = pl.BlockSpec((ns_step, 1, RET_DK, RET_DV), lambda b, h, t: (b, h, 0, 0))
    if has_s0:
        in_specs.append(st_spec)
        args.append(s0)
    return pl.pallas_call(
        functools.partial(_ret_kernel, c=c, seg=seg, nck=nck, has_s0=has_s0),
        out_shape=(jax.ShapeDtypeStruct((n, RET_V), BF16),
                   jax.ShapeDtypeStruct((nseq, RET_HEADS, RET_DK, RET_DV), F32)),
        grid=(nseq // ns_step, RET_HEADS, nt),
        in_specs=in_specs,
        out_specs=(pl.BlockSpec((rb, RET_DV), lambda b, h, t: (row(b, h, t), h)), st_spec),
        compiler_params=_cparams(("parallel", "parallel", "arbitrary")),
        name="retention",
    )(*args)


def _conv_silu(raw_scr, cw, bias, tb):
    taps = SSM_CONV
    acc = bias
    for s in range(taps):
        acc = acc + raw_scr[pl.ds(SUBLANES - s, tb), :] * cw[taps - 1 - s:taps - s, :]
    return _silu(acc)


def _bf16_terms(v):
    t1 = v.astype(BF16).astype(F32)
    r1 = v - t1
    t2 = r1.astype(BF16).astype(F32)
    t3 = (r1 - t2).astype(BF16).astype(F32)
    return t1, t2, t3


def _expand_matrix(heads, width):
    k = lax.broadcasted_iota(jnp.int32, (3 * heads, heads * width), 0) % heads
    c = lax.broadcasted_iota(jnp.int32, (3 * heads, heads * width), 1) // width
    return jnp.where(k == c, 1.0, 0.0).astype(F32)


def _head_expand(v, expand):
    return _dot_tn(jnp.concatenate(_bf16_terms(v), axis=0), expand)


def _ssd_kernel(*refs, q, seg, nck, has_s0):
    if has_s0:
        (xs_ref, b_ref, c_ref, z_ref, acc_ref, dtr_ref, acr_ref,
         cwx_ref, cwb_ref, cwc_ref, cbx_ref, cbb_ref, cbc_ref, dsk_ref, ng_ref,
         s0_ref, hx0_ref, hb0_ref, hc0_ref,
         y_ref, so_ref, hx, hb, hc, st_scr) = refs
        hist0 = (hx0_ref, hb0_ref, hc0_ref)
    else:
        (xs_ref, b_ref, c_ref, z_ref, acc_ref, dtr_ref, acr_ref,
         cwx_ref, cwb_ref, cwc_ref, cbx_ref, cbb_ref, cbc_ref, dsk_ref, ng_ref,
         y_ref, so_ref, hx, hb, hc, st_scr) = refs
        hist0 = (None, None, None)
    t = pl.program_id(2)
    spc = q // seg
    ns_step = st_scr.shape[0]
    conv_in = ((xs_ref, hx, hist0[0], cwx_ref, cbx_ref),
               (b_ref, hb, hist0[1], cwb_ref, cbb_ref),
               (c_ref, hc, hist0[2], cwc_ref, cbc_ref))

    def first_rows(raw_scr, h0_ref, sidx):
        if has_s0:
            raw_scr[0:SUBLANES, :] = h0_ref[sidx]
        else:
            raw_scr[0:SUBLANES, :] = jnp.zeros((SUBLANES, raw_scr.shape[1]), F32)

    @pl.when(t == 0)
    def _():
        for sidx in range(ns_step):
            if has_s0:
                st_scr[sidx] = s0_ref[sidx, 0].T
            else:
                st_scr[sidx] = jnp.zeros(st_scr.shape[1:], F32)
        if spc == 1:
            for _, raw_scr, h0_ref, _, _ in conv_in:
                first_rows(raw_scr, h0_ref, 0)

    ii = lax.broadcasted_iota(jnp.int32, (q, q), 0)
    jj = lax.broadcasted_iota(jnp.int32, (q, q), 1)
    causal = (ii >= jj) if spc == 1 else ((ii >= jj) & ((ii // seg) == (jj // seg)))
    lane = lax.broadcasted_iota(jnp.int32, (q, 2 * HALF_LANES), 1)
    row_slot = lax.broadcasted_iota(jnp.int32, (q, SSM_STATE), 0) // seg
    tok = lax.broadcasted_iota(jnp.int32, (SSM_HPG, q), 1)
    dsk = dsk_ref[...]
    ng = ng_ref[...]
    expand = _expand_matrix(SSM_HPG, SSM_HEADDIM)

    def conv_chunk(ci, sl):
        out = []
        for raw_ref, raw_scr, h0_ref, cw_ref, cb_ref in conv_in:
            if spc == 1:
                raw_scr[SUBLANES:SUBLANES + q, :] = raw_ref[sl, :].astype(F32)
                out.append(_conv_silu(raw_scr, cw_ref[...], cb_ref[...], q))
                raw_scr[0:SUBLANES, :] = raw_scr[q:q + SUBLANES, :]
            else:
                pieces = []
                for slot in range(spc):
                    first_rows(raw_scr, h0_ref, ci * spc + slot)
                    raw_scr[SUBLANES:SUBLANES + seg, :] = raw_ref[pl.ds(ci * q + slot * seg, seg), :].astype(F32)
                    pieces.append(_conv_silu(raw_scr, cw_ref[...], cb_ref[...], seg))
                out.append(jnp.concatenate(pieces, axis=0))
        return out

    def chunk(ci):
        sl = pl.ds(pl.multiple_of(ci * q, q), q)
        x, b_f32, c_f32 = conv_chunk(ci, sl)
        bq = b_f32.astype(BF16)
        cq = c_f32.astype(BF16)
        dtr = dtr_ref[0, ci]
        acr = acr_ref[0, ci]
        acc = acc_ref[0, sl, :]

        cb = jnp.where(causal, _dot_nt(cq, bq), 0.0)
        pairs = []
        for m in range(SSM_HPG // 2):
            ws = []
            for r in (2 * m, 2 * m + 1):
                seg_sum = acc[:, r:r + 1] - acr[r:r + 1, :]
                ws.append((jnp.exp2(jnp.minimum(seg_sum, 0.0)) * (cb * dtr[r:r + 1, :])).astype(BF16))
            xp = x[:, m * 2 * HALF_LANES:(m + 1) * 2 * HALF_LANES]
            x_lo = jnp.where(lane < HALF_LANES, xp, 0.0).astype(BF16)
            x_hi = jnp.where(lane >= HALF_LANES, xp, 0.0).astype(BF16)
            pairs.append(_dot(jnp.concatenate(ws, axis=1), jnp.concatenate([x_lo, x_hi], axis=0)))
        y = jnp.concatenate(pairs, axis=1)

        a_last = acr[:, q - 1:q]
        for slot in range(spc - 2, -1, -1):
            end = (slot + 1) * seg
            a_last = jnp.where(tok < end, acr[:, end - 1:end], a_last)
        ea = _head_expand(jnp.exp2(acr), expand)
        xw = (x * _head_expand(jnp.exp2(a_last - acr) * dtr, expand)).astype(BF16)
        inter = []
        for slot in range(spc):
            sidx = (ci * spc + slot) if spc > 1 else 0
            state = st_scr[sidx]
            end = (slot + 1) * seg
            cs = cq if spc == 1 else cq[slot * seg:end, :]
            bs = bq if spc == 1 else jnp.where(row_slot == slot, b_f32, 0.0).astype(BF16)
            inter.append(_dot(cs, state.astype(BF16)))
            st_scr[sidx] = state * ea[end - 1:end, :] + _dot_tn(bs, xw)
        y = y + (inter[0] if spc == 1 else jnp.concatenate(inter, axis=0)) * ea

        y = y + dsk * x
        y = y * _silu(z_ref[sl, :].astype(F32))
        ms = jnp.mean(y * y, axis=-1, keepdims=True)
        y_ref[sl, :] = (y * lax.rsqrt(ms + EPS) * ng).astype(BF16)

    if spc > 1 or nck == 1:
        for ci in range(nck):
            chunk(ci)
    else:
        lax.fori_loop(0, nck, lambda ci, carry: (chunk(ci), carry)[1], 0)

    @pl.when(t == pl.num_programs(2) - 1)
    def _():
        for sidx in range(ns_step):
            so_ref[sidx, 0] = st_scr[sidx].T


def _ssd(big, dts, conv_w, conv_b, d_skip_x, norm_g, s0, hist0, *, nseq, seqlen, q, seg, nck):
    n = big.shape[0]
    spc = q // seg
    rb = nck * q
    ns_step = spc * nck if spc > 1 else 1
    nt = 1 if spc > 1 else seqlen // rb
    has_s0 = s0 is not None
    acc, dtr, acr = dts
    gw, ns = SSM_GW, SSM_STATE
    kz, kx = COL_Z // gw, COL_XBC // gw
    kb = (COL_XBC + SSM_INNER) // ns
    kc = kb + SSM_GROUPS
    cb_b = SSM_INNER // ns
    cb_c = cb_b + SSM_GROUPS
    row = lambda b, g, t: b * nt + t
    in_specs = [
        pl.BlockSpec((rb, gw), lambda b, g, t: (row(b, g, t), kx + g)),
        pl.BlockSpec((rb, ns), lambda b, g, t: (row(b, g, t), kb + g)),
        pl.BlockSpec((rb, ns), lambda b, g, t: (row(b, g, t), kc + g)),
        pl.BlockSpec((rb, gw), lambda b, g, t: (row(b, g, t), kz + g)),
        pl.BlockSpec((1, rb, SSM_HPG), lambda b, g, t: (g, row(b, g, t), 0)),
        pl.BlockSpec((1, nck, SSM_HPG, q), lambda b, g, t: (g, row(b, g, t), 0, 0)),
        pl.BlockSpec((1, nck, SSM_HPG, q), lambda b, g, t: (g, row(b, g, t), 0, 0)),
        pl.BlockSpec((SSM_CONV, gw), lambda b, g, t: (0, g)),
        pl.BlockSpec((SSM_CONV, ns), lambda b, g, t: (0, cb_b + g)),
        pl.BlockSpec((SSM_CONV, ns), lambda b, g, t: (0, cb_c + g)),
        pl.BlockSpec((1, gw), lambda b, g, t: (0, g)),
        pl.BlockSpec((1, ns), lambda b, g, t: (0, cb_b + g)),
        pl.BlockSpec((1, ns), lambda b, g, t: (0, cb_c + g)),
        pl.BlockSpec((1, gw), lambda b, g, t: (0, g)),
        pl.BlockSpec((1, gw), lambda b, g, t: (0, g)),
    ]
    cbias = conv_b.reshape(1, SSM_CONV_DIM)
    args = [big, big, big, big, acc, dtr, acr, conv_w, conv_w, conv_w, cbias, cbias, cbias,
            d_skip_x, norm_g.reshape(1, SSM_INNER)]
    st_spec = pl.BlockSpec((ns_step, 1, gw, ns), lambda b, g, t: (b, g, 0, 0))
    if has_s0:
        in_specs += [
            st_spec,
            pl.BlockSpec((ns_step, SUBLANES, gw), lambda b, g, t: (b, 0, g)),
            pl.BlockSpec((ns_step, SUBLANES, ns), lambda b, g, t: (b, 0, cb_b + g)),
            pl.BlockSpec((ns_step, SUBLANES, ns), lambda b, g, t: (b, 0, cb_c + g)),
        ]
        args += [s0, hist0, hist0, hist0]
    return pl.pallas_call(
        functools.partial(_ssd_kernel, q=q, seg=seg, nck=nck, has_s0=has_s0),
        out_shape=(jax.ShapeDtypeStruct((n, SSM_INNER), BF16),
                   jax.ShapeDtypeStruct((nseq, SSM_GROUPS, gw, ns), F32)),
        grid=(nseq // ns_step, SSM_GROUPS, nt),
        in_specs=in_specs,
        out_specs=(pl.BlockSpec((rb, gw), lambda b, g, t: (row(b, g, t), g)), st_spec),
        scratch_shapes=[pltpu.VMEM((SUBLANES + q, gw), F32), pltpu.VMEM((SUBLANES + q, ns), F32),
                        pltpu.VMEM((SUBLANES + q, ns), F32), pltpu.VMEM((ns_step, ns, gw), F32)],
        compiler_params=_cparams(("parallel", "parallel", "arbitrary")),
        name="ssd",
    )(*args)


def _merge_kernel(h_ref, ret_ref, ssm_ref, wgr_ref, wgs_ref, wr_ref, ws_ref, o_ref):
    h = h_ref[...]
    gr = jax.nn.sigmoid(_dot(h, wgr_ref[...]))
    gs = jax.nn.sigmoid(_dot(h, wgs_ref[...]))
    a = _dot(ret_ref[...], wr_ref[...])
    b = _dot(ssm_ref[...], ws_ref[...])
    o_ref[...] = (gr * a + gs * b).astype(BF16)


def _merge(h, ret, ssm, w_gr, w_gs, w_r, w_s, *, tm, tn):
    n, d = h.shape
    kr, ks = ret.shape[1], ssm.shape[1]
    return pl.pallas_call(
        _merge_kernel,
        out_shape=jax.ShapeDtypeStruct((n, d), BF16),
        grid=(n // tm, d // tn),
        in_specs=[
            pl.BlockSpec((tm, d), lambda i, j: (i, 0)),
            pl.BlockSpec((tm, kr), lambda i, j: (i, 0)),
            pl.BlockSpec((tm, ks), lambda i, j: (i, 0)),
            pl.BlockSpec((d, tn), lambda i, j: (0, j)),
            pl.BlockSpec((d, tn), lambda i, j: (0, j)),
            pl.BlockSpec((kr, tn), lambda i, j: (0, j)),
            pl.BlockSpec((ks, tn), lambda i, j: (0, j)),
        ],
        out_specs=pl.BlockSpec((tm, tn), lambda i, j: (i, j)),
        compiler_params=_cparams(("parallel", "arbitrary")),
        name="merge",
    )(h, ret, ssm, w_gr, w_gs, w_r, w_s)


def _outproj_kernel(m_ref, w_ref, x_ref, o_ref, *, rc):
    for r in range(m_ref.shape[0] // rc):
        rows = pl.ds(r * rc, rc)
        o_ref[rows, :] = x_ref[rows, :] + _dot(m_ref[rows, :], w_ref[...])


def _outproj(m, w, x, *, tm, tn):
    n, d = x.shape
    k = m.shape[1]
    return pl.pallas_call(
        functools.partial(_outproj_kernel, rc=min(ROW_CHUNK, tm)),
        out_shape=jax.ShapeDtypeStruct((n, d), F32),
        grid=(n // tm, d // tn),
        in_specs=[
            pl.BlockSpec((tm, k), lambda i, j: (i, 0)),
            pl.BlockSpec((k, tn), lambda i, j: (0, j)),
            pl.BlockSpec((tm, tn), lambda i, j: (i, j)),
        ],
        out_specs=pl.BlockSpec((tm, tn), lambda i, j: (i, j)),
        compiler_params=_cparams(("parallel", "arbitrary")),
        name="outproj",
    )(m, w, x)


def _ple_kernel(x_ref, h_ref, p_ref, wp_ref, wg_ref, gf_ref, o_ref):
    pe = _dot(p_ref[...].astype(BF16), wp_ref[...])
    gt = jax.nn.sigmoid(_dot(h_ref[...], wg_ref[...]))
    x = x_ref[...] + pe * gt
    ms = jnp.mean(x * x, axis=-1, keepdims=True)
    o_ref[...] = x * lax.rsqrt(ms + EPS) * gf_ref[...]


def _ple(x, h, p, w_ple, w_gate, g_final, *, tm):
    n, d = x.shape
    pd = p.shape[1]
    return pl.pallas_call(
        _ple_kernel,
        out_shape=jax.ShapeDtypeStruct((n, d), F32),
        grid=(n // tm,),
        in_specs=[
            pl.BlockSpec((tm, d), lambda i: (i, 0)),
            pl.BlockSpec((tm, d), lambda i: (i, 0)),
            pl.BlockSpec((tm, pd), lambda i: (i, 0)),
            pl.BlockSpec((pd, d), lambda i: (0, 0)),
            pl.BlockSpec((d, d), lambda i: (0, 0)),
            pl.BlockSpec((1, d), lambda i: (0, 0)),
        ],
        out_specs=pl.BlockSpec((tm, d), lambda i: (i, 0)),
        compiler_params=_cparams(("parallel",)),
        name="ple",
    )(x, h, p, w_ple, w_gate, g_final)


def _rope_tables(pos0, seqlen, rows):
    half = RET_DK // 2
    inv = ROPE_THETA ** (-jnp.arange(half, dtype=F32) / half)
    pos = (pos0 + jnp.arange(seqlen, dtype=jnp.int32)).astype(F32)
    ang = pos[:, None] * inv[None, :]
    reps = max(rows // seqlen, 1)
    return jnp.tile(jnp.cos(ang), (reps, 1)), jnp.tile(jnp.sin(ang), (reps, 1))


def _pick(n, pref):
    t = pref
    while n % t:
        t //= 2
    return t


def _trunk(x, p, pos0, s_ret, s_ssm, s_conv, w, cfg):
    nseq, seqlen, d = x.shape
    n = nseq * seqlen
    x = x.reshape(n, d)
    p = p.reshape(n, p.shape[-1])
    tm = _pick(n, cfg["tm"])

    x1, h = _ffn(x, w["g_ffn1"], w["g_mix"], w["w1_gu"], w["w1_down"], tm=_pick(n, cfg["tm_ffn"]), tf=cfg["tf"])

    cos, sin = _rope_tables(pos0, seqlen, tm)
    big = _inproj(h, w["w_in"], cos, sin, tm=tm, tn=cfg["tn_in"])

    q = cfg["q_ssd"]
    if seqlen % q == 0:
        seg, nck_ssd = q, _pick(seqlen, cfg["tb_ssd"]) // q
        c = cfg["c_ret"] if seqlen % cfg["c_ret"] == 0 else q
        seg_ret, nck_ret = c, _pick(seqlen, cfg["tb_ret"]) // c
    else:
        assert q % seqlen == 0 and n % q == 0, "short sequences must pack into whole chunks"
        c, seg, seg_ret = q, seqlen, seqlen
        nck_ssd = nck_ret = _pick(n // q, cfg["pack_chunks"])

    dts = _dt_proj(h, w["w_dt"], w["dt_bias"], w["a_log"], tm=_pick(n, 512), q=q, seg=seg)
    ret, s_ret_new = _retention(big, w["ret_gn_g"], w["ret_gn_b"], s_ret, nseq=nseq, seqlen=seqlen,
                                c=c, seg=seg_ret, nck=nck_ret)

    if s_ssm is not None:
        s0 = s_ssm.reshape(nseq, SSM_GROUPS, SSM_GW, SSM_STATE)
        hist0 = jnp.pad(s_conv, ((0, 0), (SUBLANES - (SSM_CONV - 1), 0), (0, 0)))
    else:
        s0, hist0 = None, None
    ssm, s_ssm_new = _ssd(big, dts, w["conv_w"], w["conv_b"], w["d_skip_x"], w["ssm_norm_g"], s0, hist0,
                          nseq=nseq, seqlen=seqlen, q=q, seg=seg, nck=nck_ssd)
    s_ssm_new = s_ssm_new.reshape(nseq, SSM_HEADS, SSM_HEADDIM, SSM_STATE)

    keep = min(seqlen, SSM_CONV - 1)
    xbc_tail = big.reshape(nseq, seqlen, N_BIG)[:, seqlen - keep:, COL_XBC:].astype(F32)
    if keep < SSM_CONV - 1:
        prev = jnp.zeros((nseq, SSM_CONV - 1, SSM_CONV_DIM), F32) if s_conv is None else s_conv.astype(F32)
        xbc_tail = jnp.concatenate([prev, xbc_tail], axis=1)[:, -(SSM_CONV - 1):]

    merged = _merge(h, ret, ssm, w["w_gr"], w["w_gs"], w["w_br_ret"], w["w_br_ssm"], tm=_pick(n, cfg["tm_mg"]), tn=cfg["tn_mg"])
    x2 = _outproj(merged, w["w_out"], x1, tm=_pick(n, cfg["tm_out"]), tn=cfg["tn_out"])
    x3, h3 = _ffn(x2, w["g_ffn2"], w["g_ple"], w["w2_gu"], w["w2_down"], tm=_pick(n, cfg["tm_ffn"]), tf=cfg["tf"])
    y = _ple(x3, h3, p, w["w_ple"], w["w_ple_gate"], w["g_final"], tm=_pick(n, cfg["tm_ple"]))
    return y.reshape(nseq, seqlen, d), s_ret_new, s_ssm_new, xbc_tail


CFG = dict(tm=1024, tm_ffn=512, tf=512, tn_in=2048, q_ssd=128, c_ret=256, tb_ret=1024, tb_ssd=1024, pack_chunks=2,
           tm_mg=512, tn_mg=512, tm_out=512, tn_out=2048, tm_ple=512)


def _prep_weights(g_ffn1, w1_gu, w1_down, g_mix, w_in, ret_gn_g, ret_gn_b, conv_w, conv_b, dt_bias, a_log,
                  d_skip, ssm_norm_g, w_br_ret, w_br_ssm, w_out, g_ffn2, w2_gu, w2_down, g_ple, w_ple,
                  w_ple_gate, g_final):
    b = lambda a: a[0].astype(BF16)
    r = lambda a: a[0].reshape(1, -1).astype(F32)
    win = w_in[0]
    w_dt = win[:, COL_DT:COL_DT + SSM_HEADS].astype(BF16)
    return dict(
        g_ffn1=r(g_ffn1), w1_gu=b(w1_gu), w1_down=b(w1_down), g_mix=r(g_mix),
        w_in=win.astype(BF16), w_dt=w_dt,
        w_gr=win[:, COL_GR:COL_GR + D_MODEL].astype(BF16), w_gs=win[:, COL_GS:COL_GS + D_MODEL].astype(BF16),
        ret_gn_g=ret_gn_g[0].astype(F32), ret_gn_b=ret_gn_b[0].astype(F32),
        conv_w=conv_w[0].astype(F32), conv_b=conv_b[0].astype(F32),
        dt_bias=dt_bias[0].astype(F32), a_log=a_log[0].astype(F32),
        d_skip_x=jnp.repeat(d_skip[0].astype(F32), SSM_HEADDIM).reshape(1, SSM_INNER),
        ssm_norm_g=ssm_norm_g[0].astype(F32),
        w_br_ret=b(w_br_ret), w_br_ssm=b(w_br_ssm), w_out=b(w_out),
        g_ffn2=r(g_ffn2), w2_gu=b(w2_gu), w2_down=b(w2_down), g_ple=r(g_ple),
        w_ple=b(w_ple), w_ple_gate=b(w_ple_gate), g_final=g_final.reshape(1, -1).astype(F32),
    )


def kernel(x_prompt, x_sample, state_ret, state_ssm, state_conv, p_prompt, p_sample, g_ffn1, w1_gu, w1_down, g_mix, w_in, ret_gn_g, ret_gn_b, conv_w, conv_b, dt_bias, a_log, d_skip, ssm_norm_g, w_br_ret, w_br_ssm, w_out, g_ffn2, w2_gu, w2_down, g_ple, w_ple, w_ple_gate, g_final):
    assert g_ffn1.shape[0] == 1, "single-layer trunk"
    w = _prep_weights(g_ffn1, w1_gu, w1_down, g_mix, w_in, ret_gn_g, ret_gn_b, conv_w, conv_b, dt_bias, a_log,
                      d_skip, ssm_norm_g, w_br_ret, w_br_ssm, w_out, g_ffn2, w2_gu, w2_down, g_ple, w_ple,
                      w_ple_gate, g_final)
    y_p, ret_p, ssm_p, conv_p = _trunk(x_prompt, p_prompt[0], 0, None, None, None, w, CFG)
    y_s, ret_s, ssm_s, conv_s = _trunk(x_sample, p_sample[0], PAST_LEN, state_ret[0], state_ssm[0],
                                       state_conv[0], w, CFG)
    e = lambda a: a[None]
    return (y_p, y_s, e(ret_p), e(ssm_p), e(conv_p), e(ret_s), e(ssm_s), e(conv_s))
```

```python
import functools

import numpy as np
import jax
import jax.numpy as jnp
from jax import lax
from jax.experimental import pallas as pl
from jax.experimental.pallas import tpu as pltpu

F32 = jnp.float32
BF16 = jnp.bfloat16

D_MODEL = 2048
PAST_LEN = 4096
EPS = 1e-6
LOG2E = 1.4426950408889634
RET_HEADS = 8
RET_DK = 256
RET_DV = 512
RET_QK = RET_HEADS * RET_DK
RET_V = RET_HEADS * RET_DV
ROPE_THETA = 10000.0
SSM_INNER = 2 * D_MODEL
SSM_HEADDIM = 64
SSM_HEADS = SSM_INNER // SSM_HEADDIM
SSM_GROUPS = 8
SSM_HPG = SSM_HEADS // SSM_GROUPS
SSM_GW = SSM_HPG * SSM_HEADDIM
SSM_STATE = 128
SSM_CONV = 4
SSM_CONV_DIM = SSM_INNER + 2 * SSM_GROUPS * SSM_STATE
PLE_DIM = 256

COL_Q = 0
COL_K = COL_Q + RET_QK
COL_V = COL_K + RET_QK
COL_RG = COL_V + RET_V
COL_Z = COL_RG + RET_V
COL_XBC = COL_Z + SSM_INNER
N_BIG = COL_XBC + SSM_CONV_DIM
COL_DT = N_BIG
COL_GR = COL_DT + SSM_HEADS
COL_GS = COL_GR + D_MODEL

SUBLANES = 8
HALF_LANES = 64
VMEM_LIMIT = 56 * 1024 * 1024
FFN_COLS = 256
NORM_ROWS = 128
ROW_CHUNK = 256


def _cparams(sem):
    return pltpu.CompilerParams(dimension_semantics=sem, vmem_limit_bytes=VMEM_LIMIT)


def _rms_bf16(x, g):
    ms = jnp.mean(x * x, axis=-1, keepdims=True)
    return (x * lax.rsqrt(ms + EPS) * g).astype(BF16)


def _silu(x):
    return x * jax.nn.sigmoid(x)


def _dot(a, b):
    return jnp.dot(a, b, preferred_element_type=F32)


def _dot_nt(a, b):
    return lax.dot_general(a, b, (((1,), (1,)), ((), ())), preferred_element_type=F32)


def _dot_tn(a, b):
    return lax.dot_general(a, b, (((0,), (0,)), ((), ())), preferred_element_type=F32)


def _ffn_kernel(x_ref, g_ref, gn_ref, wg_ref, wu_ref, wd_ref, o_ref, hn_ref, h_scr, *, nj):
    j = pl.program_id(1)

    tm = x_ref.shape[0]

    @pl.when(j == 0)
    def _():
        for r in range(0, tm, NORM_ROWS):
            h_scr[r:r + NORM_ROWS, :] = _rms_bf16(x_ref[r:r + NORM_ROWS, :], g_ref[...])
        o_ref[...] = jnp.zeros_like(o_ref)

    h = h_scr[...]
    tf = wg_ref.shape[1]
    acts = []
    for lo in range(0, tf, FFN_COLS):
        hi = min(lo + FFN_COLS, tf)
        gate = _dot(h, wg_ref[:, lo:hi])
        up = _dot(h, wu_ref[:, lo:hi])
        acts.append((_silu(gate) * up).astype(BF16))
    act = jnp.concatenate(acts, axis=1)
    d = o_ref.shape[1]
    for lo in range(0, d, d // 4):
        o_ref[:, lo:lo + d // 4] += _dot(act, wd_ref[:, lo:lo + d // 4])

    @pl.when(j == nj - 1)
    def _():
        for r in range(0, tm, NORM_ROWS):
            xn = x_ref[r:r + NORM_ROWS, :] + 0.5 * o_ref[r:r + NORM_ROWS, :]
            o_ref[r:r + NORM_ROWS, :] = xn
            hn_ref[r:r + NORM_ROWS, :] = _rms_bf16(xn, gn_ref[...])


def _ffn(x, g, g_next, w_gu, w_down, *, tm, tf):
    n, d = x.shape
    f = w_down.shape[0]
    nj = f // tf
    return pl.pallas_call(
        functools.partial(_ffn_kernel, nj=nj),
        out_shape=(jax.ShapeDtypeStruct((n, d), F32), jax.ShapeDtypeStruct((n, d), BF16)),
        grid=(n // tm, nj),
        in_specs=[
            pl.BlockSpec((tm, d), lambda i, j: (i, 0)),
            pl.BlockSpec((1, d), lambda i, j: (0, 0)),
            pl.BlockSpec((1, d), lambda i, j: (0, 0)),
            pl.BlockSpec((d, tf), lambda i, j: (0, j)),
            pl.BlockSpec((d, tf), lambda i, j: (0, j + nj)),
            pl.BlockSpec((tf, d), lambda i, j: (j, 0)),
        ],
        out_specs=(pl.BlockSpec((tm, d), lambda i, j: (i, 0)),
                   pl.BlockSpec((tm, d), lambda i, j: (i, 0))),
        scratch_shapes=[pltpu.VMEM((tm, d), BF16)],
        compiler_params=_cparams(("parallel", "arbitrary")),
        name="ffn",
    )(x, g, g_next, w_gu, w_gu, w_down)


def _inproj_kernel(h_ref, w_ref, cos_ref, sin_ref, o_ref, *, tn, nq, nrope, rc):
    j = pl.program_id(1)
    nr = h_ref.shape[0] // rc

    @pl.when(j >= nrope)
    def _():
        for r in range(nr):
            rows = pl.ds(r * rc, rc)
            o_ref[rows, :] = _dot(h_ref[rows, :], w_ref[...]).astype(BF16)

    @pl.when(j < nrope)
    def _():
        scale = jnp.where(j >= nq, RET_DK ** -0.5, 1.0).astype(F32)
        half = RET_DK // 2
        for r in range(nr):
            rows = pl.ds(r * rc, rc)
            acc = _dot(h_ref[rows, :], w_ref[...])
            c = cos_ref[rows, :] * scale
            s = sin_ref[rows, :] * scale
            for hh in range(tn // RET_DK):
                lo = hh * RET_DK
                x1 = acc[:, lo:lo + half]
                x2 = acc[:, lo + half:lo + RET_DK]
                o_ref[rows, lo:lo + half] = (x1 * c - x2 * s).astype(BF16)
                o_ref[rows, lo + half:lo + RET_DK] = (x2 * c + x1 * s).astype(BF16)


def _inproj(h, w_in, cos, sin, *, tm, tn):
    n, d = h.shape
    nrow = cos.shape[0] // tm
    return pl.pallas_call(
        functools.partial(_inproj_kernel, tn=tn, nq=RET_QK // tn, nrope=2 * RET_QK // tn, rc=min(ROW_CHUNK, tm)),
        out_shape=jax.ShapeDtypeStruct((n, N_BIG), BF16),
        grid=(n // tm, N_BIG // tn),
        in_specs=[
            pl.BlockSpec((tm, d), lambda i, j: (i, 0)),
            pl.BlockSpec((d, tn), lambda i, j: (0, j)),
            pl.BlockSpec((tm, RET_DK // 2), lambda i, j: (i % nrow, 0)),
            pl.BlockSpec((tm, RET_DK // 2), lambda i, j: (i % nrow, 0)),
        ],
        out_specs=pl.BlockSpec((tm, tn), lambda i, j: (i, j)),
        compiler_params=_cparams(("parallel", "arbitrary")),
        name="inproj",
    )(h, w_in, cos, sin)


def _softplus(x):
    return jnp.maximum(x, 0.0) + jnp.log1p(jnp.exp(-jnp.abs(x)))


def _dt_kernel(h_ref, w_ref, b_ref, a_ref, acc_ref, dtr_ref, acr_ref, *, q, seg):
    h = h_ref[...]
    tm = h.shape[0]
    dt_c = _softplus(_dot(h, w_ref[...]) + b_ref[...])
    dta_c = dt_c * (-jnp.exp(a_ref[...]) * LOG2E)
    i = lax.broadcasted_iota(jnp.int32, (q, q), 0)
    j = lax.broadcasted_iota(jnp.int32, (q, q), 1)
    tri = jnp.where(((i // seg) == (j // seg)) & (j <= i), 1.0, 0.0).astype(F32)
    for ci in range(tm // q):
        lo_r, hi_r = ci * q, (ci + 1) * q
        ac_c = jnp.dot(tri, dta_c[lo_r:hi_r, :], preferred_element_type=F32, precision=lax.Precision.HIGHEST)
        ac_r = ac_c.T
        dt_r = dt_c[lo_r:hi_r, :].T
        for g in range(SSM_GROUPS):
            lo = g * SSM_HPG
            acc_ref[g, lo_r:hi_r, :] = ac_c[:, lo:lo + SSM_HPG]
            dtr_ref[g, ci] = dt_r[lo:lo + SSM_HPG, :]
            acr_ref[g, ci] = ac_r[lo:lo + SSM_HPG, :]


def _dt_proj(h, w_dt, dt_bias, a_log, *, tm, q, seg):
    n, d = h.shape
    hh = SSM_HEADS
    col = jax.ShapeDtypeStruct((SSM_GROUPS, n, SSM_HPG), F32)
    row = jax.ShapeDtypeStruct((SSM_GROUPS, n // q, SSM_HPG, q), F32)
    return pl.pallas_call(
        functools.partial(_dt_kernel, q=q, seg=seg),
        out_shape=(col, row, row),
        grid=(n // tm,),
        in_specs=[
            pl.BlockSpec((tm, d), lambda i: (i, 0)),
            pl.BlockSpec((d, hh), lambda i: (0, 0)),
            pl.BlockSpec((1, hh), lambda i: (0, 0)),
            pl.BlockSpec((1, hh), lambda i: (0, 0)),
        ],
        out_specs=(
            pl.BlockSpec((SSM_GROUPS, tm, SSM_HPG), lambda i: (0, i, 0)),
            pl.BlockSpec((SSM_GROUPS, tm // q, SSM_HPG, q), lambda i: (0, i, 0, 0)),
            pl.BlockSpec((SSM_GROUPS, tm // q, SSM_HPG, q), lambda i: (0, i, 0, 0)),
        ),
        compiler_params=_cparams(("parallel",)),
        name="dt_proj",
    )(h, w_dt, dt_bias.reshape(1, hh), a_log.reshape(1, hh))


def _ret_kernel(*refs, c, seg, nck, has_s0):
    if has_s0:
        (q_ref, k_ref, v_ref, rg_ref, dm_ref, qd_ref, kd_ref, cd_ref, gg_ref, gb_ref, s0_ref,
         o_ref, s_ref) = refs
    else:
        (q_ref, k_ref, v_ref, rg_ref, dm_ref, qd_ref, kd_ref, cd_ref, gg_ref, gb_ref,
         o_ref, s_ref) = refs
    t = pl.program_id(2)
    spc = c // seg

    @pl.when(t == 0)
    def _():
        if has_s0:
            s_ref[...] = s0_ref[...]
        else:
            s_ref[...] = jnp.zeros_like(s_ref)

    dm = dm_ref[0]
    qd = qd_ref[0]
    cd = cd_ref[0]
    gg = gg_ref[...]
    gb = gb_ref[...]
    for ci in range(nck):
        sl = pl.ds(ci * c, c)
        q = q_ref[sl, :]
        k = k_ref[sl, :]
        v = v_ref[sl, :]
        att = _dot_nt(q, k) * dm
        o = _dot(att.astype(BF16), v)
        kf = k.astype(F32)
        inter = []
        for slot in range(spc):
            sidx = (ci * spc + slot) if spc > 1 else 0
            st = s_ref[sidx, 0]
            qs = q if spc == 1 else q[slot * seg:(slot + 1) * seg, :]
            inter.append(_dot(qs, st.astype(BF16)))
            s_ref[sidx, 0] = st * cd + _dot_tn((kf * kd_ref[0, slot]).astype(BF16), v)
        o = o + (inter[0] if spc == 1 else jnp.concatenate(inter, axis=0)) * qd
        mu = jnp.mean(o, axis=-1, keepdims=True)
        dev = o - mu
        var = jnp.mean(dev * dev, axis=-1, keepdims=True)
        on = dev * lax.rsqrt(var + EPS)
        rg = rg_ref[sl, :].astype(F32)
        o_ref[sl, :] = (_silu(rg) * (on * gg + gb)).astype(BF16)


def _ret_tables(c, seg):
    spc = c // seg
    hs = np.arange(RET_HEADS, dtype=np.float64)
    log_g = np.log1p(-np.exp2(-5.0 - hs))
    idx = np.arange(c)
    pos = (idx % seg).astype(np.float64)
    slot = idx // seg
    diff = pos[:, None] - pos[None, :]
    live = (slot[:, None] == slot[None, :]) & (diff >= 0)
    dmat = np.where(live[None], np.exp(np.maximum(diff, 0.0)[None] * log_g[:, None, None]), 0.0)
    q_dec = np.exp((pos[None, :] + 1.0) * log_g[:, None])
    k_dec = np.exp((seg - 1.0 - pos[None, :]) * log_g[:, None])
    c_dec = np.exp(seg * log_g)
    k_slot = np.where(slot[None, None, :] == np.arange(spc)[None, :, None], k_dec[:, None, :], 0.0)
    qd = np.broadcast_to(q_dec[:, :, None], (RET_HEADS, c, RET_DV))
    kd = np.broadcast_to(k_slot[:, :, :, None], (RET_HEADS, spc, c, RET_DK))
    cd = np.broadcast_to(c_dec[:, None, None], (RET_HEADS, 1, RET_DV))
    f = lambda a: jnp.asarray(np.ascontiguousarray(a), dtype=F32)
    return f(dmat), f(qd), f(kd), f(cd)


def _retention(big, gn_g, gn_b, s0, *, nseq, seqlen, c, seg, nck):
    n = big.shape[0]
    spc = c // seg
    rb = nck * c
    ns_step = spc * nck if spc > 1 else 1
    nt = 1 if spc > 1 else seqlen // rb
    dm, qd, kd, cd = _ret_tables(c, seg)
    has_s0 = s0 is not None
    kq, kk = COL_Q // RET_DK, COL_K // RET_DK
    kv, kr = COL_V // RET_DV, COL_RG // RET_DV
    row = lambda b, h, t: b * nt + t
    in_specs = [
        pl.BlockSpec((rb, RET_DK), lambda b, h, t: (row(b, h, t), kq + h)),
        pl.BlockSpec((rb, RET_DK), lambda b, h, t: (row(b, h, t), kk + h)),
        pl.BlockSpec((rb, RET_DV), lambda b, h, t: (row(b, h, t), kv + h)),
        pl.BlockSpec((rb, RET_DV), lambda b, h, t: (row(b, h, t), kr + h)),
        pl.BlockSpec((1, c, c), lambda b, h, t: (h, 0, 0)),
        pl.BlockSpec((1, c, RET_DV), lambda b, h, t: (h, 0, 0)),
        pl.BlockSpec((1, spc, c, RET_DK), lambda b, h, t: (h, 0, 0, 0)),
        pl.BlockSpec((1, 1, RET_DV), lambda b, h, t: (h, 0, 0)),
        pl.BlockSpec((1, RET_DV), lambda b, h, t: (0, h)),
        pl.BlockSpec((1, RET_DV), lambda b, h, t: (0, h)),
    ]
    args = [big, big, big, big, dm, qd, kd, cd, gn_g.reshape(1, RET_V), gn_b.reshape(1, RET_V)]
    st_spec = pl.BlockSpec((ns_step, 1, RET_DK, RET_DV), lambda b, h, t: (b, h, 0, 0))
    if has_s0:
        in_specs.append(st_spec)
        args.append(s0)
    return pl.pallas_call(
        functools.partial(_ret_kernel, c=c, seg=seg, nck=nck, has_s0=has_s0),
        out_shape=(jax.ShapeDtypeStruct((n, RET_V), BF16),
                   jax.ShapeDtypeStruct((nseq, RET_HEADS, RET_DK, RET_DV), F32)),
        grid=(nseq // ns_step, RET_HEADS, nt),
        in_specs=in_specs,
        out_specs=(pl.BlockSpec((rb, RET_DV), lambda b, h, t: (row(b, h, t), h)), st_spec),
        compiler_params=_cparams(("parallel", "parallel", "arbitrary")),
        name="retention",
    )(*args)


def _conv_silu(raw_scr, cw, bias, tb):
    taps = SSM_CONV
    acc = bias
    for s in range(taps):
        acc = acc + raw_scr[pl.ds(SUBLANES - s, tb), :] * cw[taps - 1 - s:taps - s, :]
    return _silu(acc)


def _bf16_terms(v):
    t1 = v.astype(BF16).astype(F32)
    r1 = v - t1
    t2 = r1.astype(BF16).astype(F32)
    t3 = (r1 - t2).astype(BF16).astype(F32)
    return t1, t2, t3


def _expand_matrix(heads, width):
    k = lax.broadcasted_iota(jnp.int32, (3 * heads, heads * width), 0) % heads
    c = lax.broadcasted_iota(jnp.int32, (3 * heads, heads * width), 1) // width
    return jnp.where(k == c, 1.0, 0.0).astype(F32)


def _head_expand(v, expand):
    return _dot_tn(jnp.concatenate(_bf16_terms(v), axis=0), expand)


def _ssd_kernel(*refs, q, seg, nck, has_s0):
    if has_s0:
        (xs_ref, b_ref, c_ref, z_ref, acc_ref, dtr_ref, acr_ref,
         cwx_ref, cwb_ref, cwc_ref, cbx_ref, cbb_ref, cbc_ref, dsk_ref, ng_ref,
         s0_ref, hx0_ref, hb0_ref, hc0_ref,
         y_ref, so_ref, hx, hb, hc, st_scr) = refs
        hist0 = (hx0_ref, hb0_ref, hc0_ref)
    else:
        (xs_ref, b_ref, c_ref, z_ref, acc_ref, dtr_ref, acr_ref,
         cwx_ref, cwb_ref, cwc_ref, cbx_ref, cbb_ref, cbc_ref, dsk_ref, ng_ref,
         y_ref, so_ref, hx, hb, hc, st_scr) = refs
        hist0 = (None, None, None)
    t = pl.program_id(2)
    spc = q // seg
    ns_step = st_scr.shape[0]
    conv_in = ((xs_ref, hx, hist0[0], cwx_ref, cbx_ref),
               (b_ref, hb, hist0[1], cwb_ref, cbb_ref),
               (c_ref, hc, hist0[2], cwc_ref, cbc_ref))

    def first_rows(raw_scr, h0_ref, sidx):
        if has_s0:
            raw_scr[0:SUBLANES, :] = h0_ref[sidx]
        else:
            raw_scr[0:SUBLANES, :] = jnp.zeros((SUBLANES, raw_scr.shape[1]), F32)

    @pl.when(t == 0)
    def _():
        for sidx in range(ns_step):
            if has_s0:
                st_scr[sidx] = s0_ref[sidx, 0].T
            else:
                st_scr[sidx] = jnp.zeros(st_scr.shape[1:], F32)
        if spc == 1:
            for _, raw_scr, h0_ref, _, _ in conv_in:
                first_rows(raw_scr, h0_ref, 0)

    ii = lax.broadcasted_iota(jnp.int32, (q, q), 0)
    jj = lax.broadcasted_iota(jnp.int32, (q, q), 1)
    causal = (ii >= jj) if spc == 1 else ((ii >= jj) & ((ii // seg) == (jj // seg)))
    lane = lax.broadcasted_iota(jnp.int32, (q, 2 * HALF_LANES), 1)
    row_slot = lax.broadcasted_iota(jnp.int32, (q, SSM_STATE), 0) // seg
    tok = lax.broadcasted_iota(jnp.int32, (SSM_HPG, q), 1)
    dsk = dsk_ref[...]
    ng = ng_ref[...]
    expand = _expand_matrix(SSM_HPG, SSM_HEADDIM)

    def conv_chunk(ci, sl):
        out = []
        for raw_ref, raw_scr, h0_ref, cw_ref, cb_ref in conv_in:
            if spc == 1:
                raw_scr[SUBLANES:SUBLANES + q, :] = raw_ref[sl, :].astype(F32)
                out.append(_conv_silu(raw_scr, cw_ref[...], cb_ref[...], q))
                raw_scr[0:SUBLANES, :] = raw_scr[q:q + SUBLANES, :]
            else:
                pieces = []
                for slot in range(spc):
                    first_rows(raw_scr, h0_ref, ci * spc + slot)
                    raw_scr[SUBLANES:SUBLANES + seg, :] = raw_ref[pl.ds(ci * q + slot * seg, seg), :].astype(F32)
                    pieces.append(_conv_silu(raw_scr, cw_ref[...], cb_ref[...], seg))
                out.append(jnp.concatenate(pieces, axis=0))
        return out

    def chunk(ci):
        sl = pl.ds(pl.multiple_of(ci * q, q), q)
        x, b_f32, c_f32 = conv_chunk(ci, sl)
        bq = b_f32.astype(BF16)
        cq = c_f32.astype(BF16)
        dtr = dtr_ref[0, ci]
        acr = acr_ref[0, ci]
        acc = acc_ref[0, sl, :]

        cb = jnp.where(causal, _dot_nt(cq, bq), 0.0)
        pairs = []
        for m in range(SSM_HPG // 2):
            ws = []
            for r in (2 * m, 2 * m + 1):
                seg_sum = acc[:, r:r + 1] - acr[r:r + 1, :]
                ws.append((jnp.exp2(jnp.minimum(seg_sum, 0.0)) * (cb * dtr[r:r + 1, :])).astype(BF16))
            xp = x[:, m * 2 * HALF_LANES:(m + 1) * 2 * HALF_LANES]
            x_lo = jnp.where(lane < HALF_LANES, xp, 0.0).astype(BF16)
            x_hi = jnp.where(lane >= HALF_LANES, xp, 0.0).astype(BF16)
            pairs.append(_dot(jnp.concatenate(ws, axis=1), jnp.concatenate([x_lo, x_hi], axis=0)))
        y = jnp.concatenate(pairs, axis=1)

        a_last = acr[:, q - 1:q]
        for slot in range(spc - 2, -1, -1):
            end = (slot + 1) * seg
            a_last = jnp.where(tok < end, acr[:, end - 1:end], a_last)
        ea = _head_expand(jnp.exp2(acr), expand)
        xw = (x * _head_expand(jnp.exp2(a_last - acr) * dtr, expand)).astype(BF16)
        inter = []
        for slot in range(spc):
            sidx = (ci * spc + slot) if spc > 1 else 0
            state = st_scr[sidx]
            end = (slot + 1) * seg
            cs = cq if spc == 1 else cq[slot * seg:end, :]
            bs = bq if spc == 1 else jnp.where(row_slot == slot, b_f32, 0.0).astype(BF16)
            inter.append(_dot(cs, state.astype(BF16)))
            st_scr[sidx] = state * ea[end - 1:end, :] + _dot_tn(bs, xw)
        y = y + (inter[0] if spc == 1 else jnp.concatenate(inter, axis=0)) * ea

        y = y + dsk * x
        y = y * _silu(z_ref[sl, :].astype(F32))
        ms = jnp.mean(y * y, axis=-1, keepdims=True)
        y_ref[sl, :] = (y * lax.rsqrt(ms + EPS) * ng).astype(BF16)

    if spc > 1 or nck == 1:
        for ci in range(nck):
            chunk(ci)
    else:
        lax.fori_loop(0, nck, lambda ci, carry: (chunk(ci), carry)[1], 0)

    @pl.when(t == pl.num_programs(2) - 1)
    def _():
        for sidx in range(ns_step):
            so_ref[sidx, 0] = st_scr[sidx].T


def _ssd(big, dts, conv_w, conv_b, d_skip_x, norm_g, s0, hist0, *, nseq, seqlen, q, seg, nck):
    n = big.shape[0]
    spc = q // seg
    rb = nck * q
    ns_step = spc * nck if spc > 1 else 1
    nt = 1 if spc > 1 else seqlen // rb
    has_s0 = s0 is not None
    acc, dtr, acr = dts
    gw, ns = SSM_GW, SSM_STATE
    kz, kx = COL_Z // gw, COL_XBC // gw
    kb = (COL_XBC + SSM_INNER) // ns
    kc = kb + SSM_GROUPS
    cb_b = SSM_INNER // ns
    cb_c = cb_b + SSM_GROUPS
    row = lambda b, g, t: b * nt + t
    in_specs = [
        pl.BlockSpec((rb, gw), lambda b, g, t: (row(b, g, t), kx + g)),
        pl.BlockSpec((rb, ns), lambda b, g, t: (row(b, g, t), kb + g)),
        pl.BlockSpec((rb, ns), lambda b, g, t: (row(b, g, t), kc + g)),
        pl.BlockSpec((rb, gw), lambda b, g, t: (row(b, g, t), kz + g)),
        pl.BlockSpec((1, rb, SSM_HPG), lambda b, g, t: (g, row(b, g, t), 0)),
        pl.BlockSpec((1, nck, SSM_HPG, q), lambda b, g, t: (g, row(b, g, t), 0, 0)),
        pl.BlockSpec((1, nck, SSM_HPG, q), lambda b, g, t: (g, row(b, g, t), 0, 0)),
        pl.BlockSpec((SSM_CONV, gw), lambda b, g, t: (0, g)),
        pl.BlockSpec((SSM_CONV, ns), lambda b, g, t: (0, cb_b + g)),
        pl.BlockSpec((SSM_CONV, ns), lambda b, g, t: (0, cb_c + g)),
        pl.BlockSpec((1, gw), lambda b, g, t: (0, g)),
        pl.BlockSpec((1, ns), lambda b, g, t: (0, cb_b + g)),
        pl.BlockSpec((1, ns), lambda b, g, t: (0, cb_c + g)),
        pl.BlockSpec((1, gw), lambda b, g, t: (0, g)),
        pl.BlockSpec((1, gw), lambda b, g, t: (0, g)),
    ]
    cbias = conv_b.reshape(1, SSM_CONV_DIM)
    args = [big, big, big, big, acc, dtr, acr, conv_w, conv_w, conv_w, cbias, cbias, cbias,
            d_skip_x, norm_g.reshape(1, SSM_INNER)]
    st_spec = pl.BlockSpec((ns_step, 1, gw, ns), lambda b, g, t: (b, g, 0, 0))
    if has_s0:
        in_specs += [
            st_spec,
            pl.BlockSpec((ns_step, SUBLANES, gw), lambda b, g, t: (b, 0, g)),
            pl.BlockSpec((ns_step, SUBLANES, ns), lambda b, g, t: (b, 0, cb_b + g)),
            pl.BlockSpec((ns_step, SUBLANES, ns), lambda b, g, t: (b, 0, cb_c + g)),
        ]
        args += [s0, hist0, hist0, hist0]
    return pl.pallas_call(
        functools.partial(_ssd_kernel, q=q, seg=seg, nck=nck, has_s0=has_s0),
        out_shape=(jax.ShapeDtypeStruct((n, SSM_INNER), BF16),
                   jax.ShapeDtypeStruct((nseq, SSM_GROUPS, gw, ns), F32)),
        grid=(nseq // ns_step, SSM_GROUPS, nt),
        in_specs=in_specs,
        out_specs=(pl.BlockSpec((rb, gw), lambda b, g, t: (row(b, g, t), g)), st_spec),
        scratch_shapes=[pltpu.VMEM((SUBLANES + q, gw), F32), pltpu.VMEM((SUBLANES + q, ns), F32),
                        pltpu.VMEM((SUBLANES + q, ns), F32), pltpu.VMEM((ns_step, ns, gw), F32)],
        compiler_params=_cparams(("parallel", "parallel", "arbitrary")),
        name="ssd",
    )(*args)


def _merge_kernel(h_ref, ret_ref, ssm_ref, wgr_ref, wgs_ref, wr_ref, ws_ref, o_ref):
    h = h_ref[...]
    gr = jax.nn.sigmoid(_dot(h, wgr_ref[...]))
    gs = jax.nn.sigmoid(_dot(h, wgs_ref[...]))
    a = _dot(ret_ref[...], wr_ref[...])
    b = _dot(ssm_ref[...], ws_ref[...])
    o_ref[...] = (gr * a + gs * b).astype(BF16)


def _merge(h, ret, ssm, w_gr, w_gs, w_r, w_s, *, tm, tn):
    n, d = h.shape
    kr, ks = ret.shape[1], ssm.shape[1]
    return pl.pallas_call(
        _merge_kernel,
        out_shape=jax.ShapeDtypeStruct((n, d), BF16),
        grid=(n // tm, d // tn),
        in_specs=[
            pl.BlockSpec((tm, d), lambda i, j: (i, 0)),
            pl.BlockSpec((tm, kr), lambda i, j: (i, 0)),
            pl.BlockSpec((tm, ks), lambda i, j: (i, 0)),
            pl.BlockSpec((d, tn), lambda i, j: (0, j)),
            pl.BlockSpec((d, tn), lambda i, j: (0, j)),
            pl.BlockSpec((kr, tn), lambda i, j: (0, j)),
            pl.BlockSpec((ks, tn), lambda i, j: (0, j)),
        ],
        out_specs=pl.BlockSpec((tm, tn), lambda i, j: (i, j)),
        compiler_params=_cparams(("parallel", "arbitrary")),
        name="merge",
    )(h, ret, ssm, w_gr, w_gs, w_r, w_s)


def _outproj_kernel(m_ref, w_ref, x_ref, o_ref, *, rc):
    for r in range(m_ref.shape[0] // rc):
        rows = pl.ds(r * rc, rc)
        o_ref[rows, :] = x_ref[rows, :] + _dot(m_ref[rows, :], w_ref[...])


def _outproj(m, w, x, *, tm, tn):
    n, d = x.shape
    k = m.shape[1]
    return pl.pallas_call(
        functools.partial(_outproj_kernel, rc=min(ROW_CHUNK, tm)),
        out_shape=jax.ShapeDtypeStruct((n, d), F32),
        grid=(n // tm, d // tn),
        in_specs=[
            pl.BlockSpec((tm, k), lambda i, j: (i, 0)),
            pl.BlockSpec((k, tn), lambda i, j: (0, j)),
            pl.BlockSpec((tm, tn), lambda i, j: (i, j)),
        ],
        out_specs=pl.BlockSpec((tm, tn), lambda i, j: (i, j)),
        compiler_params=_cparams(("parallel", "arbitrary")),
        name="outproj",
    )(m, w, x)


def _ple_kernel(x_ref, h_ref, p_ref, wp_ref, wg_ref, gf_ref, o_ref):
    pb = p_ref[...].astype(BF16)
    h = h_ref[...]
    d = o_ref.shape[1]
    ssq = 0.0
    for lo in range(0, d, d // 2):
        cols = slice(lo, lo + d // 2)
        pe = _dot(pb, wp_ref[:, cols])
        gt = jax.nn.sigmoid(_dot(h, wg_ref[:, cols]))
        x = x_ref[:, cols] + pe * gt
        o_ref[:, cols] = x
        ssq = ssq + jnp.sum(x * x, axis=-1, keepdims=True)
    scale = lax.rsqrt(ssq * (1.0 / d) + EPS)
    o_ref[...] = o_ref[...] * scale * gf_ref[...]


def _ple(x, h, p, w_ple, w_gate, g_final, *, tm):
    n, d = x.shape
    pd = p.shape[1]
    return pl.pallas_call(
        _ple_kernel,
        out_shape=jax.ShapeDtypeStruct((n, d), F32),
        grid=(n // tm,),
        in_specs=[
            pl.BlockSpec((tm, d), lambda i: (i, 0)),
            pl.BlockSpec((tm, d), lambda i: (i, 0)),
            pl.BlockSpec((tm, pd), lambda i: (i, 0)),
            pl.BlockSpec((pd, d), lambda i: (0, 0)),
            pl.BlockSpec((d, d), lambda i: (0, 0)),
            pl.BlockSpec((1, d), lambda i: (0, 0)),
        ],
        out_specs=pl.BlockSpec((tm, d), lambda i: (i, 0)),
        compiler_params=_cparams(("parallel",)),
        name="ple",
    )(x, h, p, w_ple, w_gate, g_final)


def _rope_tables(pos0, seqlen, rows):
    half = RET_DK // 2
    inv = ROPE_THETA ** (-jnp.arange(half, dtype=F32) / half)
    pos = (pos0 + jnp.arange(seqlen, dtype=jnp.int32)).astype(F32)
    ang = pos[:, None] * inv[None, :]
    reps = max(rows // seqlen, 1)
    return jnp.tile(jnp.cos(ang), (reps, 1)), jnp.tile(jnp.sin(ang), (reps, 1))


def _pick(n, pref):
    t = pref
    while n % t:
        t //= 2
    return t


def _trunk(x, p, pos0, s_ret, s_ssm, s_conv, w, cfg):
    nseq, seqlen, d = x.shape
    n = nseq * seqlen
    x = x.reshape(n, d)
    p = p.reshape(n, p.shape[-1])
    tm = _pick(n, cfg["tm"])

    x1, h = _ffn(x, w["g_ffn1"], w["g_mix"], w["w1_gu"], w["w1_down"], tm=_pick(n, cfg["tm_ffn"]), tf=cfg["tf"])

    cos, sin = _rope_tables(pos0, seqlen, tm)
    big = _inproj(h, w["w_in"], cos, sin, tm=tm, tn=cfg["tn_in"])

    q = cfg["q_ssd"]
    if seqlen % q == 0:
        seg, nck_ssd = q, _pick(seqlen, cfg["tb_ssd"]) // q
        c = cfg["c_ret"] if seqlen % cfg["c_ret"] == 0 else q
        seg_ret, nck_ret = c, _pick(seqlen, cfg["tb_ret"]) // c
    else:
        assert q % seqlen == 0 and n % q == 0, "short sequences must pack into whole chunks"
        c, seg, seg_ret = q, seqlen, seqlen
        nck_ssd = nck_ret = _pick(n // q, cfg["pack_chunks"])

    dts = _dt_proj(h, w["w_dt"], w["dt_bias"], w["a_log"], tm=_pick(n, 512), q=q, seg=seg)
    ret, s_ret_new = _retention(big, w["ret_gn_g"], w["ret_gn_b"], s_ret, nseq=nseq, seqlen=seqlen,
                                c=c, seg=seg_ret, nck=nck_ret)

    if s_ssm is not None:
        s0 = s_ssm.reshape(nseq, SSM_GROUPS, SSM_GW, SSM_STATE)
        hist0 = jnp.pad(s_conv, ((0, 0), (SUBLANES - (SSM_CONV - 1), 0), (0, 0)))
    else:
        s0, hist0 = None, None
    ssm, s_ssm_new = _ssd(big, dts, w["conv_w"], w["conv_b"], w["d_skip_x"], w["ssm_norm_g"], s0, hist0,
                          nseq=nseq, seqlen=seqlen, q=q, seg=seg, nck=nck_ssd)
    s_ssm_new = s_ssm_new.reshape(nseq, SSM_HEADS, SSM_HEADDIM, SSM_STATE)

    keep = min(seqlen, SSM_CONV - 1)
    xbc_tail = big.reshape(nseq, seqlen, N_BIG)[:, seqlen - keep:, COL_XBC:].astype(F32)
    if keep < SSM_CONV - 1:
        prev = jnp.zeros((nseq, SSM_CONV - 1, SSM_CONV_DIM), F32) if s_conv is None else s_conv.astype(F32)
        xbc_tail = jnp.concatenate([prev, xbc_tail], axis=1)[:, -(SSM_CONV - 1):]

    merged = _merge(h, ret, ssm, w["w_gr"], w["w_gs"], w["w_br_ret"], w["w_br_ssm"], tm=_pick(n, cfg["tm_mg"]), tn=cfg["tn_mg"])
    x2 = _outproj(merged, w["w_out"], x1, tm=_pick(n, cfg["tm_out"]), tn=cfg["tn_out"])
    x3, h3 = _ffn(x2, w["g_ffn2"], w["g_ple"], w["w2_gu"], w["w2_down"], tm=_pick(n, cfg["tm_ffn"]), tf=cfg["tf"])
    y = _ple(x3, h3, p, w["w_ple"], w["w_ple_gate"], w["g_final"], tm=_pick(n, cfg["tm_ple"]))
    return y.reshape(nseq, seqlen, d), s_ret_new, s_ssm_new, xbc_tail


CFG = dict(tm=1024, tm_ffn=512, tf=512, tn_in=2048, q_ssd=128, c_ret=256, tb_ret=1024, tb_ssd=2048, pack_chunks=2,
           tm_mg=512, tn_mg=512, tm_out=512, tn_out=2048, tm_ple=512)


def _prep_weights(g_ffn1, w1_gu, w1_down, g_mix, w_in, ret_gn_g, ret_gn_b, conv_w, conv_b, dt_bias, a_log,
                  d_skip, ssm_norm_g, w_br_ret, w_br_ssm, w_out, g_ffn2, w2_gu, w2_down, g_ple, w_ple,
                  w_ple_gate, g_final):
    b = lambda a: a[0].astype(BF16)
    r = lambda a: a[0].reshape(1, -1).astype(F32)
    win = w_in[0]
    w_dt = win[:, COL_DT:COL_DT + SSM_HEADS].astype(BF16)
    return dict(
        g_ffn1=r(g_ffn1), w1_gu=b(w1_gu), w1_down=b(w1_down), g_mix=r(g_mix),
        w_in=win.astype(BF16), w_dt=w_dt,
        w_gr=win[:, COL_GR:COL_GR + D_MODEL].astype(BF16), w_gs=win[:, COL_GS:COL_GS + D_MODEL].astype(BF16),
        ret_gn_g=ret_gn_g[0].astype(F32), ret_gn_b=ret_gn_b[0].astype(F32),
        conv_w=conv_w[0].astype(F32), conv_b=conv_b[0].astype(F32),
        dt_bias=dt_bias[0].astype(F32), a_log=a_log[0].astype(F32),
        d_skip_x=jnp.repeat(d_skip[0].astype(F32), SSM_HEADDIM).reshape(1, SSM_INNER),
        ssm_norm_g=ssm_norm_g[0].astype(F32),
        w_br_ret=b(w_br_ret), w_br_ssm=b(w_br_ssm), w_out=b(w_out),
        g_ffn2=r(g_ffn2), w2_gu=b(w2_gu), w2_down=b(w2_down), g_ple=r(g_ple),
        w_ple=b(w_ple), w_ple_gate=b(w_ple_gate), g_final=g_final.reshape(1, -1).astype(F32),
    )


def kernel(x_prompt, x_sample, state_ret, state_ssm, state_conv, p_prompt, p_sample, g_ffn1, w1_gu, w1_down, g_mix, w_in, ret_gn_g, ret_gn_b, conv_w, conv_b, dt_bias, a_log, d_skip, ssm_norm_g, w_br_ret, w_br_ssm, w_out, g_ffn2, w2_gu, w2_down, g_ple, w_ple, w_ple_gate, g_final):
    assert g_ffn1.shape[0] == 1, "single-layer trunk"
    w = _prep_weights(g_ffn1, w1_gu, w1_down, g_mix, w_in, ret_gn_g, ret_gn_b, conv_w, conv_b, dt_bias, a_log,
                      d_skip, ssm_norm_g, w_br_ret, w_br_ssm, w_out, g_ffn2, w2_gu, w2_down, g_ple, w_ple,
                      w_ple_gate, g_final)
    y_p, ret_p, ssm_p, conv_p = _trunk(x_prompt, p_prompt[0], 0, None, None, None, w, CFG)
    y_s, ret_s, ssm_s, conv_s = _trunk(x_sample, p_sample[0], PAST_LEN, state_ret[0], state_ssm[0],
                                       state_conv[0], w, CFG)
    e = lambda a: a[None]
    return (y_p, y_s, e(ret_p), e(ssm_p), e(conv_p), e(ret_s), e(ssm_s), e(conv_s))
```

```python
import functools

import numpy as np
import jax
import jax.numpy as jnp
from jax import lax
from jax.experimental import pallas as pl
from jax.experimental.pallas import tpu as pltpu

F32 = jnp.float32
BF16 = jnp.bfloat16

D_MODEL = 2048
PAST_LEN = 4096
EPS = 1e-6
LOG2E = 1.4426950408889634
RET_HEADS = 8
RET_DK = 256
RET_DV = 512
RET_QK = RET_HEADS * RET_DK
RET_V = RET_HEADS * RET_DV
ROPE_THETA = 10000.0
SSM_INNER = 2 * D_MODEL
SSM_HEADDIM = 64
SSM_HEADS = SSM_INNER // SSM_HEADDIM
SSM_GROUPS = 8
SSM_HPG = SSM_HEADS // SSM_GROUPS
SSM_GW = SSM_HPG * SSM_HEADDIM
SSM_STATE = 128
SSM_CONV = 4
SSM_CONV_DIM = SSM_INNER + 2 * SSM_GROUPS * SSM_STATE
PLE_DIM = 256

COL_Q = 0
COL_K = COL_Q + RET_QK
COL_V = COL_K + RET_QK
COL_RG = COL_V + RET_V
COL_Z = COL_RG + RET_V
COL_XBC = COL_Z + SSM_INNER
N_BIG = COL_XBC + SSM_CONV_DIM
COL_DT = N_BIG
COL_GR = COL_DT + SSM_HEADS
COL_GS = COL_GR + D_MODEL

SUBLANES = 8
HALF_LANES = 64
VMEM_LIMIT = 56 * 1024 * 1024
FFN_COLS = 256
NORM_ROWS = 128
ROW_CHUNK = 256


def _cparams(sem):
    return pltpu.CompilerParams(dimension_semantics=sem, vmem_limit_bytes=VMEM_LIMIT)


def _rms_bf16(x, g):
    ms = jnp.mean(x * x, axis=-1, keepdims=True)
    return (x * lax.rsqrt(ms + EPS) * g).astype(BF16)


def _silu(x):
    return x * jax.nn.sigmoid(x)


def _dot(a, b):
    return jnp.dot(a, b, preferred_element_type=F32)


def _dot_nt(a, b):
    return lax.dot_general(a, b, (((1,), (1,)), ((), ())), preferred_element_type=F32)


def _dot_tn(a, b):
    return lax.dot_general(a, b, (((0,), (0,)), ((), ())), preferred_element_type=F32)


def _ffn_kernel(x_ref, g_ref, gn_ref, wg_ref, wu_ref, wd_ref, o_ref, hn_ref, h_scr, *, nj):
    j = pl.program_id(1)

    tm = x_ref.shape[0]

    @pl.when(j == 0)
    def _():
        for r in range(0, tm, NORM_ROWS):
            h_scr[r:r + NORM_ROWS, :] = _rms_bf16(x_ref[r:r + NORM_ROWS, :], g_ref[...])
        o_ref[...] = jnp.zeros_like(o_ref)

    h = h_scr[...]
    tf = wg_ref.shape[1]
    acts = []
    for lo in range(0, tf, FFN_COLS):
        hi = min(lo + FFN_COLS, tf)
        gate = _dot(h, wg_ref[:, lo:hi])
        up = _dot(h, wu_ref[:, lo:hi])
        acts.append((_silu(gate) * up).astype(BF16))
    act = jnp.concatenate(acts, axis=1)
    d = o_ref.shape[1]
    for lo in range(0, d, d // 4):
        o_ref[:, lo:lo + d // 4] += _dot(act, wd_ref[:, lo:lo + d // 4])

    @pl.when(j == nj - 1)
    def _():
        for r in range(0, tm, NORM_ROWS):
            xn = x_ref[r:r + NORM_ROWS, :] + 0.5 * o_ref[r:r + NORM_ROWS, :]
            o_ref[r:r + NORM_ROWS, :] = xn
            hn_ref[r:r + NORM_ROWS, :] = _rms_bf16(xn, gn_ref[...])


def _ffn(x, g, g_next, w_gu, w_down, *, tm, tf):
    n, d = x.shape
    f = w_down.shape[0]
    nj = f // tf
    return pl.pallas_call(
        functools.partial(_ffn_kernel, nj=nj),
        out_shape=(jax.ShapeDtypeStruct((n, d), F32), jax.ShapeDtypeStruct((n, d), BF16)),
        grid=(n // tm, nj),
        in_specs=[
            pl.BlockSpec((tm, d), lambda i, j: (i, 0)),
            pl.BlockSpec((1, d), lambda i, j: (0, 0)),
            pl.BlockSpec((1, d), lambda i, j: (0, 0)),
            pl.BlockSpec((d, tf), lambda i, j: (0, j)),
            pl.BlockSpec((d, tf), lambda i, j: (0, j + nj)),
            pl.BlockSpec((tf, d), lambda i, j: (j, 0)),
        ],
        out_specs=(pl.BlockSpec((tm, d), lambda i, j: (i, 0)),
                   pl.BlockSpec((tm, d), lambda i, j: (i, 0))),
        scratch_shapes=[pltpu.VMEM((tm, d), BF16)],
        compiler_params=_cparams(("parallel", "arbitrary")),
        name="ffn",
    )(x, g, g_next, w_gu, w_gu, w_down)


def _inproj_kernel(h_ref, w_ref, cos_ref, sin_ref, o_ref, *, tn, nq, nrope, rc):
    j = pl.program_id(1)
    nr = h_ref.shape[0] // rc

    @pl.when(j >= nrope)
    def _():
        for r in range(nr):
            rows = pl.ds(r * rc, rc)
            o_ref[rows, :] = _dot_nt(h_ref[rows, :], w_ref[...]).astype(BF16)

    @pl.when(j < nrope)
    def _():
        scale = jnp.where(j >= nq, RET_DK ** -0.5, 1.0).astype(F32)
        half = RET_DK // 2
        for r in range(nr):
            rows = pl.ds(r * rc, rc)
            acc = _dot_nt(h_ref[rows, :], w_ref[...])
            c = cos_ref[rows, :] * scale
            s = sin_ref[rows, :] * scale
            for hh in range(tn // RET_DK):
                lo = hh * RET_DK
                x1 = acc[:, lo:lo + half]
                x2 = acc[:, lo + half:lo + RET_DK]
                o_ref[rows, lo:lo + half] = (x1 * c - x2 * s).astype(BF16)
                o_ref[rows, lo + half:lo + RET_DK] = (x2 * c + x1 * s).astype(BF16)


def _inproj(h, w_in_t, cos, sin, *, tm, tn):
    n, d = h.shape
    nrow = cos.shape[0] // tm
    return pl.pallas_call(
        functools.partial(_inproj_kernel, tn=tn, nq=RET_QK // tn, nrope=2 * RET_QK // tn, rc=min(ROW_CHUNK, tm)),
        out_shape=jax.ShapeDtypeStruct((n, N_BIG), BF16),
        grid=(n // tm, N_BIG // tn),
        in_specs=[
            pl.BlockSpec((tm, d), lambda i, j: (i, 0)),
            pl.BlockSpec((tn, d), lambda i, j: (j, 0)),
            pl.BlockSpec((tm, RET_DK // 2), lambda i, j: (i % nrow, 0)),
            pl.BlockSpec((tm, RET_DK // 2), lambda i, j: (i % nrow, 0)),
        ],
        out_specs=pl.BlockSpec((tm, tn), lambda i, j: (i, j)),
        compiler_params=_cparams(("parallel", "arbitrary")),
        name="inproj",
    )(h, w_in_t, cos, sin)


def _softplus(x):
    return jnp.maximum(x, 0.0) + jnp.log1p(jnp.exp(-jnp.abs(x)))


def _dt_kernel(h_ref, w_ref, b_ref, a_ref, acc_ref, dtr_ref, acr_ref, *, q, seg):
    h = h_ref[...]
    tm = h.shape[0]
    dt_c = _softplus(_dot(h, w_ref[...]) + b_ref[...])
    dta_c = dt_c * (-jnp.exp(a_ref[...]) * LOG2E)
    i = lax.broadcasted_iota(jnp.int32, (q, q), 0)
    j = lax.broadcasted_iota(jnp.int32, (q, q), 1)
    tri = jnp.where(((i // seg) == (j // seg)) & (j <= i), 1.0, 0.0).astype(F32)
    for ci in range(tm // q):
        lo_r, hi_r = ci * q, (ci + 1) * q
        ac_c = jnp.dot(tri, dta_c[lo_r:hi_r, :], preferred_element_type=F32, precision=lax.Precision.HIGHEST)
        ac_r = ac_c.T
        dt_r = dt_c[lo_r:hi_r, :].T
        for g in range(SSM_GROUPS):
            lo = g * SSM_HPG
            acc_ref[g, lo_r:hi_r, :] = ac_c[:, lo:lo + SSM_HPG]
            dtr_ref[g, ci] = dt_r[lo:lo + SSM_HPG, :]
            acr_ref[g, ci] = ac_r[lo:lo + SSM_HPG, :]


def _dt_proj(h, w_dt, dt_bias, a_log, *, tm, q, seg):
    n, d = h.shape
    hh = SSM_HEADS
    col = jax.ShapeDtypeStruct((SSM_GROUPS, n, SSM_HPG), F32)
    row = jax.ShapeDtypeStruct((SSM_GROUPS, n // q, SSM_HPG, q), F32)
    return pl.pallas_call(
        functools.partial(_dt_kernel, q=q, seg=seg),
        out_shape=(col, row, row),
        grid=(n // tm,),
        in_specs=[
            pl.BlockSpec((tm, d), lambda i: (i, 0)),
            pl.BlockSpec((d, hh), lambda i: (0, 0)),
            pl.BlockSpec((1, hh), lambda i: (0, 0)),
            pl.BlockSpec((1, hh), lambda i: (0, 0)),
        ],
        out_specs=(
            pl.BlockSpec((SSM_GROUPS, tm, SSM_HPG), lambda i: (0, i, 0)),
            pl.BlockSpec((SSM_GROUPS, tm // q, SSM_HPG, q), lambda i: (0, i, 0, 0)),
            pl.BlockSpec((SSM_GROUPS, tm // q, SSM_HPG, q), lambda i: (0, i, 0, 0)),
        ),
        compiler_params=_cparams(("parallel",)),
        name="dt_proj",
    )(h, w_dt, dt_bias.reshape(1, hh), a_log.reshape(1, hh))


def _ret_kernel(*refs, c, seg, nck, has_s0):
    if has_s0:
        (q_ref, k_ref, v_ref, rg_ref, dm_ref, qd_ref, kd_ref, cd_ref, gg_ref, gb_ref, s0_ref,
         o_ref, s_ref) = refs
    else:
        (q_ref, k_ref, v_ref, rg_ref, dm_ref, qd_ref, kd_ref, cd_ref, gg_ref, gb_ref,
         o_ref, s_ref) = refs
    t = pl.program_id(2)
    spc = c // seg

    @pl.when(t == 0)
    def _():
        if has_s0:
            s_ref[...] = s0_ref[...]
        else:
            s_ref[...] = jnp.zeros_like(s_ref)

    dm = dm_ref[0]
    qd = qd_ref[0]
    cd = cd_ref[0]
    gg = gg_ref[...]
    gb = gb_ref[...]
    for ci in range(nck):
        sl = pl.ds(ci * c, c)
        q = q_ref[sl, :]
        k = k_ref[sl, :]
        v = v_ref[sl, :]
        att = _dot_nt(q, k) * dm
        o = _dot(att.astype(BF16), v)
        kf = k.astype(F32)
        inter = []
        for slot in range(spc):
            sidx = (ci * spc + slot) if spc > 1 else 0
            st = s_ref[sidx, 0]
            qs = q if spc == 1 else q[slot * seg:(slot + 1) * seg, :]
            inter.append(_dot(qs, st.astype(BF16)))
            s_ref[sidx, 0] = st * cd + _dot_tn((kf * kd_ref[0, slot]).astype(BF16), v)
        o = o + (inter[0] if spc == 1 else jnp.concatenate(inter, axis=0)) * qd
        mu = jnp.mean(o, axis=-1, keepdims=True)
        dev = o - mu
        var = jnp.mean(dev * dev, axis=-1, keepdims=True)
        on = dev * lax.rsqrt(var + EPS)
        rg = rg_ref[sl, :].astype(F32)
        o_ref[sl, :] = (_silu(rg) * (on * gg + gb)).astype(BF16)


def _ret_tables(c, seg):
    spc = c // seg
    hs = np.arange(RET_HEADS, dtype=np.float64)
    log_g = np.log1p(-np.exp2(-5.0 - hs))
    idx = np.arange(c)
    pos = (idx % seg).astype(np.float64)
    slot = idx // seg
    diff = pos[:, None] - pos[None, :]
    live = (slot[:, None] == slot[None, :]) & (diff >= 0)
    dmat = np.where(live[None], np.exp(np.maximum(diff, 0.0)[None] * log_g[:, None, None]), 0.0)
    q_dec = np.exp((pos[None, :] + 1.0) * log_g[:, None])
    k_dec = np.exp((seg - 1.0 - pos[None, :]) * log_g[:, None])
    c_dec = np.exp(seg * log_g)
    k_slot = np.where(slot[None, None, :] == np.arange(spc)[None, :, None], k_dec[:, None, :], 0.0)
    qd = np.broadcast_to(q_dec[:, :, None], (RET_HEADS, c, RET_DV))
    kd = np.broadcast_to(k_slot[:, :, :, None], (RET_HEADS, spc, c, RET_DK))
    cd = np.broadcast_to(c_dec[:, None, None], (RET_HEADS, 1, RET_DV))
    f = lambda a: jnp.asarray(np.ascontiguousarray(a), dtype=F32)
    return f(dmat), f(qd), f(kd), f(cd)


def _retention(big, gn_g, gn_b, s0, *, nseq, seqlen, c, seg, nck):
    n = big.shape[0]
    spc = c // seg
    rb = nck * c
    ns_step = spc * nck if spc > 1 else 1
    nt = 1 if spc > 1 else seqlen // rb
    dm, qd, kd, cd = _ret_tables(c, seg)
    has_s0 = s0 is not None
    kq, kk = COL_Q // RET_DK, COL_K // RET_DK
    kv, kr = COL_V // RET_DV, COL_RG // RET_DV
    row = lambda b, h, t: b * nt + t
    in_specs = [
        pl.BlockSpec((rb, RET_DK), lambda b, h, t: (row(b, h, t), kq + h)),
        pl.BlockSpec((rb, RET_DK), lambda b, h, t: (row(b, h, t), kk + h)),
        pl.BlockSpec((rb, RET_DV), lambda b, h, t: (row(b, h, t), kv + h)),
        pl.BlockSpec((rb, RET_DV), lambda b, h, t: (row(b, h, t), kr + h)),
        pl.BlockSpec((1, c, c), lambda b, h, t: (h, 0, 0)),
        pl.BlockSpec((1, c, RET_DV), lambda b, h, t: (h, 0, 0)),
        pl.BlockSpec((1, spc, c, RET_DK), lambda b, h, t: (h, 0, 0, 0)),
        pl.BlockSpec((1, 1, RET_DV), lambda b, h, t: (h, 0, 0)),
        pl.BlockSpec((1, RET_DV), lambda b, h, t: (0, h)),
        pl.BlockSpec((1, RET_DV), lambda b, h, t: (0, h)),
    ]
    args = [big, big, big, big, dm, qd, kd, cd, gn_g.reshape(1, RET_V), gn_b.reshape(1, RET_V)]
    st_spec = pl.BlockSpec((ns_step, 1, RET_DK, RET_DV), lambda b, h, t: (b, h, 0, 0))
    if has_s0:
        in_specs.append(st_spec)
        args.append(s0)
    return pl.pallas_call(
        functools.partial(_ret_kernel, c=c, seg=seg, nck=nck, has_s0=has_s0),
        out_shape=(jax.ShapeDtypeStruct((n, RET_V), BF16),
                   jax.ShapeDtypeStruct((nseq, RET_HEADS, RET_DK, RET_DV), F32)),
        grid=(nseq // ns_step, RET_HEADS, nt),
        in_specs=in_specs,
        out_specs=(pl.BlockSpec((rb, RET_DV), lambda b, h, t: (row(b, h, t), h)), st_spec),
        compiler_params=_cparams(("parallel", "parallel", "arbitrary")),
        name="retention",
    )(*args)


def _conv_silu(raw_scr, cw, bias, tb):
    taps = SSM_CONV
    acc = bias
    for s in range(taps):
        acc = acc + raw_scr[pl.ds(SUBLANES - s, tb), :] * cw[taps - 1 - s:taps - s, :]
    return _silu(acc)


def _bf16_terms(v):
    t1 = v.astype(BF16).astype(F32)
    r1 = v - t1
    t2 = r1.astype(BF16).astype(F32)
    t3 = (r1 - t2).astype(BF16).astype(F32)
    return t1, t2, t3


def _expand_matrix(heads, width):
    k = lax.broadcasted_iota(jnp.int32, (3 * heads, heads * width), 0) % heads
    c = lax.broadcasted_iota(jnp.int32, (3 * heads, heads * width), 1) // width
    return jnp.where(k == c, 1.0, 0.0).astype(F32)


def _head_expand(v, expand):
    return _dot_tn(jnp.concatenate(_bf16_terms(v), axis=0), expand)


def _ssd_kernel(*refs, q, seg, nck, has_s0):
    if has_s0:
        (xs_ref, b_ref, c_ref, z_ref, acc_ref, dtr_ref, acr_ref,
         cwx_ref, cwb_ref, cwc_ref, cbx_ref, cbb_ref, cbc_ref, dsk_ref, ng_ref,
         s0_ref, hx0_ref, hb0_ref, hc0_ref,
         y_ref, so_ref, hx, hb, hc, st_scr) = refs
        hist0 = (hx0_ref, hb0_ref, hc0_ref)
    else:
        (xs_ref, b_ref, c_ref, z_ref, acc_ref, dtr_ref, acr_ref,
         cwx_ref, cwb_ref, cwc_ref, cbx_ref, cbb_ref, cbc_ref, dsk_ref, ng_ref,
         y_ref, so_ref, hx, hb, hc, st_scr) = refs
        hist0 = (None, None, None)
    t = pl.program_id(2)
    spc = q // seg
    ns_step = st_scr.shape[0]
    conv_in = ((xs_ref, hx, hist0[0], cwx_ref, cbx_ref),
               (b_ref, hb, hist0[1], cwb_ref, cbb_ref),
               (c_ref, hc, hist0[2], cwc_ref, cbc_ref))

    def first_rows(raw_scr, h0_ref, sidx):
        if has_s0:
            raw_scr[0:SUBLANES, :] = h0_ref[sidx]
        else:
            raw_scr[0:SUBLANES, :] = jnp.zeros((SUBLANES, raw_scr.shape[1]), F32)

    @pl.when(t == 0)
    def _():
        for sidx in range(ns_step):
            if has_s0:
                st_scr[sidx] = s0_ref[sidx, 0].T
            else:
                st_scr[sidx] = jnp.zeros(st_scr.shape[1:], F32)
        if spc == 1:
            for _, raw_scr, h0_ref, _, _ in conv_in:
                first_rows(raw_scr, h0_ref, 0)

    ii = lax.broadcasted_iota(jnp.int32, (q, q), 0)
    jj = lax.broadcasted_iota(jnp.int32, (q, q), 1)
    causal = (ii >= jj) if spc == 1 else ((ii >= jj) & ((ii // seg) == (jj // seg)))
    lane = lax.broadcasted_iota(jnp.int32, (q, 2 * HALF_LANES), 1)
    row_slot = lax.broadcasted_iota(jnp.int32, (q, SSM_STATE), 0) // seg
    tok = lax.broadcasted_iota(jnp.int32, (SSM_HPG, q), 1)
    dsk = dsk_ref[...]
    ng = ng_ref[...]
    expand = _expand_matrix(SSM_HPG, SSM_HEADDIM)

    def conv_chunk(ci, sl):
        out = []
        for raw_ref, raw_scr, h0_ref, cw_ref, cb_ref in conv_in:
            if spc == 1:
                raw_scr[SUBLANES:SUBLANES + q, :] = raw_ref[sl, :].astype(F32)
                out.append(_conv_silu(raw_scr, cw_ref[...], cb_ref[...], q))
                raw_scr[0:SUBLANES, :] = raw_scr[q:q + SUBLANES, :]
            else:
                pieces = []
                for slot in range(spc):
                    first_rows(raw_scr, h0_ref, ci * spc + slot)
                    raw_scr[SUBLANES:SUBLANES + seg, :] = raw_ref[pl.ds(ci * q + slot * seg, seg), :].astype(F32)
                    pieces.append(_conv_silu(raw_scr, cw_ref[...], cb_ref[...], seg))
                out.append(jnp.concatenate(pieces, axis=0))
        return out

    def chunk(ci):
        sl = pl.ds(pl.multiple_of(ci * q, q), q)
        x, b_f32, c_f32 = conv_chunk(ci, sl)
        bq = b_f32.astype(BF16)
        cq = c_f32.astype(BF16)
        dtr = dtr_ref[0, ci]
        acr = acr_ref[0, ci]
        acc = acc_ref[0, sl, :]

        cb = jnp.where(causal, _dot_nt(cq, bq), 0.0)
        pairs = []
        for m in range(SSM_HPG // 2):
            ws = []
            for r in (2 * m, 2 * m + 1):
                seg_sum = acc[:, r:r + 1] - acr[r:r + 1, :]
                ws.append((jnp.exp2(jnp.minimum(seg_sum, 0.0)) * (cb * dtr[r:r + 1, :])).astype(BF16))
            xp = x[:, m * 2 * HALF_LANES:(m + 1) * 2 * HALF_LANES]
            x_lo = jnp.where(lane < HALF_LANES, xp, 0.0).astype(BF16)
            x_hi = jnp.where(lane >= HALF_LANES, xp, 0.0).astype(BF16)
            pairs.append(_dot(jnp.concatenate(ws, axis=1), jnp.concatenate([x_lo, x_hi], axis=0)))
        y = jnp.concatenate(pairs, axis=1)

        a_last = acr[:, q - 1:q]
        for slot in range(spc - 2, -1, -1):
            end = (slot + 1) * seg
            a_last = jnp.where(tok < end, acr[:, end - 1:end], a_last)
        ea = _head_expand(jnp.exp2(acr), expand)
        xw = (x * _head_expand(jnp.exp2(a_last - acr) * dtr, expand)).astype(BF16)
        inter = []
        for slot in range(spc):
            sidx = (ci * spc + slot) if spc > 1 else 0
            state = st_scr[sidx]
            end = (slot + 1) * seg
            cs = cq if spc == 1 else cq[slot * seg:end, :]
            bs = bq if spc == 1 else jnp.where(row_slot == slot, b_f32, 0.0).astype(BF16)
            inter.append(_dot(cs, state.astype(BF16)))
            st_scr[sidx] = state * ea[end - 1:end, :] + _dot_tn(bs, xw)
        y = y + (inter[0] if spc == 1 else jnp.concatenate(inter, axis=0)) * ea

        y = y + dsk * x
        y = y * _silu(z_ref[sl, :].astype(F32))
        ms = jnp.mean(y * y, axis=-1, keepdims=True)
        y_ref[sl, :] = (y * lax.rsqrt(ms + EPS) * ng).astype(BF16)

    if spc > 1 or nck == 1:
        for ci in range(nck):
            chunk(ci)
    else:
        lax.fori_loop(0, nck, lambda ci, carry: (chunk(ci), carry)[1], 0)

    @pl.when(t == pl.num_programs(2) - 1)
    def _():
        for sidx in range(ns_step):
            so_ref[sidx, 0] = st_scr[sidx].T


def _ssd(big, dts, conv_w, conv_b, d_skip_x, norm_g, s0, hist0, *, nseq, seqlen, q, seg, nck):
    n = big.shape[0]
    spc = q // seg
    rb = nck * q
    ns_step = spc * nck if spc > 1 else 1
    nt = 1 if spc > 1 else seqlen // rb
    has_s0 = s0 is not None
    acc, dtr, acr = dts
    gw, ns = SSM_GW, SSM_STATE
    kz, kx = COL_Z // gw, COL_XBC // gw
    kb = (COL_XBC + SSM_INNER) // ns
    kc = kb + SSM_GROUPS
    cb_b = SSM_INNER // ns
    cb_c = cb_b + SSM_GROUPS
    row = lambda b, g, t: b * nt + t
    in_specs = [
        pl.BlockSpec((rb, gw), lambda b, g, t: (row(b, g, t), kx + g)),
        pl.BlockSpec((rb, ns), lambda b, g, t: (row(b, g, t), kb + g)),
        pl.BlockSpec((rb, ns), lambda b, g, t: (row(b, g, t), kc + g)),
        pl.BlockSpec((rb, gw), lambda b, g, t: (row(b, g, t), kz + g)),
        pl.BlockSpec((1, rb, SSM_HPG), lambda b, g, t: (g, row(b, g, t), 0)),
        pl.BlockSpec((1, nck, SSM_HPG, q), lambda b, g, t: (g, row(b, g, t), 0, 0)),
        pl.BlockSpec((1, nck, SSM_HPG, q), lambda b, g, t: (g, row(b, g, t), 0, 0)),
        pl.BlockSpec((SSM_CONV, gw), lambda b, g, t: (0, g)),
        pl.BlockSpec((SSM_CONV, ns), lambda b, g, t: (0, cb_b + g)),
        pl.BlockSpec((SSM_CONV, ns), lambda b, g, t: (0, cb_c + g)),
        pl.BlockSpec((1, gw), lambda b, g, t: (0, g)),
        pl.BlockSpec((1, ns), lambda b, g, t: (0, cb_b + g)),
        pl.BlockSpec((1, ns), lambda b, g, t: (0, cb_c + g)),
        pl.BlockSpec((1, gw), lambda b, g, t: (0, g)),
        pl.BlockSpec((1, gw), lambda b, g, t: (0, g)),
    ]
    cbias = conv_b.reshape(1, SSM_CONV_DIM)
    args = [big, big, big, big, acc, dtr, acr, conv_w, conv_w, conv_w, cbias, cbias, cbias,
            d_skip_x, norm_g.reshape(1, SSM_INNER)]
    st_spec = pl.BlockSpec((ns_step, 1, gw, ns), lambda b, g, t: (b, g, 0, 0))
    if has_s0:
        in_specs += [
            st_spec,
            pl.BlockSpec((ns_step, SUBLANES, gw), lambda b, g, t: (b, 0, g)),
            pl.BlockSpec((ns_step, SUBLANES, ns), lambda b, g, t: (b, 0, cb_b + g)),
            pl.BlockSpec((ns_step, SUBLANES, ns), lambda b, g, t: (b, 0, cb_c + g)),
        ]
        args += [s0, hist0, hist0, hist0]
    return pl.pallas_call(
        functools.partial(_ssd_kernel, q=q, seg=seg, nck=nck, has_s0=has_s0),
        out_shape=(jax.ShapeDtypeStruct((n, SSM_INNER), BF16),
                   jax.ShapeDtypeStruct((nseq, SSM_GROUPS, gw, ns), F32)),
        grid=(nseq // ns_step, SSM_GROUPS, nt),
        in_specs=in_specs,
        out_specs=(pl.BlockSpec((rb, gw), lambda b, g, t: (row(b, g, t), g)), st_spec),
        scratch_shapes=[pltpu.VMEM((SUBLANES + q, gw), F32), pltpu.VMEM((SUBLANES + q, ns), F32),
                        pltpu.VMEM((SUBLANES + q, ns), F32), pltpu.VMEM((ns_step, ns, gw), F32)],
        compiler_params=_cparams(("parallel", "parallel", "arbitrary")),
        name="ssd",
    )(*args)


def _merge_kernel(h_ref, ret_ref, ssm_ref, wgr_ref, wgs_ref, wr_ref, ws_ref, o_ref):
    h = h_ref[...]
    gr = jax.nn.sigmoid(_dot(h, wgr_ref[...]))
    gs = jax.nn.sigmoid(_dot(h, wgs_ref[...]))
    a = _dot(ret_ref[...], wr_ref[...])
    b = _dot(ssm_ref[...], ws_ref[...])
    o_ref[...] = (gr * a + gs * b).astype(BF16)


def _merge(h, ret, ssm, w_gr, w_gs, w_r, w_s, *, tm, tn):
    n, d = h.shape
    kr, ks = ret.shape[1], ssm.shape[1]
    return pl.pallas_call(
        _merge_kernel,
        out_shape=jax.ShapeDtypeStruct((n, d), BF16),
        grid=(n // tm, d // tn),
        in_specs=[
            pl.BlockSpec((tm, d), lambda i, j: (i, 0)),
            pl.BlockSpec((tm, kr), lambda i, j: (i, 0)),
            pl.BlockSpec((tm, ks), lambda i, j: (i, 0)),
            pl.BlockSpec((d, tn), lambda i, j: (0, j)),
            pl.BlockSpec((d, tn), lambda i, j: (0, j)),
            pl.BlockSpec((kr, tn), lambda i, j: (0, j)),
            pl.BlockSpec((ks, tn), lambda i, j: (0, j)),
        ],
        out_specs=pl.BlockSpec((tm, tn), lambda i, j: (i, j)),
        compiler_params=_cparams(("parallel", "arbitrary")),
        name="merge",
    )(h, ret, ssm, w_gr, w_gs, w_r, w_s)


def _outproj_kernel(m_ref, w_ref, x_ref, o_ref, *, rc):
    for r in range(m_ref.shape[0] // rc):
        rows = pl.ds(r * rc, rc)
        o_ref[rows, :] = x_ref[rows, :] + _dot(m_ref[rows, :], w_ref[...])


def _outproj(m, w, x, *, tm, tn):
    n, d = x.shape
    k = m.shape[1]
    return pl.pallas_call(
        functools.partial(_outproj_kernel, rc=min(ROW_CHUNK, tm)),
        out_shape=jax.ShapeDtypeStruct((n, d), F32),
        grid=(n // tm, d // tn),
        in_specs=[
            pl.BlockSpec((tm, k), lambda i, j: (i, 0)),
            pl.BlockSpec((k, tn), lambda i, j: (0, j)),
            pl.BlockSpec((tm, tn), lambda i, j: (i, j)),
        ],
        out_specs=pl.BlockSpec((tm, tn), lambda i, j: (i, j)),
        compiler_params=_cparams(("parallel", "arbitrary")),
        name="outproj",
    )(m, w, x)


def _ple_kernel(x_ref, h_ref, p_ref, wp_ref, wg_ref, gf_ref, o_ref):
    pe = _dot(p_ref[...].astype(BF16), wp_ref[...])
    gt = jax.nn.sigmoid(_dot(h_ref[...], wg_ref[...]))
    x = x_ref[...] + pe * gt
    ms = jnp.mean(x * x, axis=-1, keepdims=True)
    o_ref[...] = x * lax.rsqrt(ms + EPS) * gf_ref[...]


def _ple(x, h, p, w_ple, w_gate, g_final, *, tm):
    n, d = x.shape
    pd = p.shape[1]
    return pl.pallas_call(
        _ple_kernel,
        out_shape=jax.ShapeDtypeStruct((n, d), F32),
        grid=(n // tm,),
        in_specs=[
            pl.BlockSpec((tm, d), lambda i: (i, 0)),
            pl.BlockSpec((tm, d), lambda i: (i, 0)),
            pl.BlockSpec((tm, pd), lambda i: (i, 0)),
            pl.BlockSpec((pd, d), lambda i: (0, 0)),
            pl.BlockSpec((d, d), lambda i: (0, 0)),
            pl.BlockSpec((1, d), lambda i: (0, 0)),
        ],
        out_specs=pl.BlockSpec((tm, d), lambda i: (i, 0)),
        compiler_params=_cparams(("parallel",)),
        name="ple",
    )(x, h, p, w_ple, w_gate, g_final)


def _rope_tables(pos0, seqlen, rows):
    half = RET_DK // 2
    inv = ROPE_THETA ** (-jnp.arange(half, dtype=F32) / half)
    pos = (pos0 + jnp.arange(seqlen, dtype=jnp.int32)).astype(F32)
    ang = pos[:, None] * inv[None, :]
    reps = max(rows // seqlen, 1)
    return jnp.tile(jnp.cos(ang), (reps, 1)), jnp.tile(jnp.sin(ang), (reps, 1))


def _pick(n, pref):
    t = pref
    while n % t:
        t //= 2
    return t


def _trunk(x, p, pos0, s_ret, s_ssm, s_conv, w, cfg):
    nseq, seqlen, d = x.shape
    n = nseq * seqlen
    x = x.reshape(n, d)
    p = p.reshape(n, p.shape[-1])
    tm = _pick(n, cfg["tm"])

    x1, h = _ffn(x, w["g_ffn1"], w["g_mix"], w["w1_gu"], w["w1_down"], tm=_pick(n, cfg["tm_ffn"]), tf=cfg["tf"])

    cos, sin = _rope_tables(pos0, seqlen, tm)
    big = _inproj(h, w["w_in_t"], cos, sin, tm=tm, tn=cfg["tn_in"])

    q = cfg["q_ssd"]
    if seqlen % q == 0:
        seg, nck_ssd = q, _pick(seqlen, cfg["tb_ssd"]) // q
        c = cfg["c_ret"] if seqlen % cfg["c_ret"] == 0 else q
        seg_ret, nck_ret = c, _pick(seqlen, cfg["tb_ret"]) // c
    else:
        assert q % seqlen == 0 and n % q == 0, "short sequences must pack into whole chunks"
        c, seg, seg_ret = q, seqlen, seqlen
        nck_ssd = nck_ret = _pick(n // q, cfg["pack_chunks"])

    dts = _dt_proj(h, w["w_dt"], w["dt_bias"], w["a_log"], tm=_pick(n, 512), q=q, seg=seg)
    ret, s_ret_new = _retention(big, w["ret_gn_g"], w["ret_gn_b"], s_ret, nseq=nseq, seqlen=seqlen,
                                c=c, seg=seg_ret, nck=nck_ret)

    if s_ssm is not None:
        s0 = s_ssm.reshape(nseq, SSM_GROUPS, SSM_GW, SSM_STATE)
        hist0 = jnp.pad(s_conv, ((0, 0), (SUBLANES - (SSM_CONV - 1), 0), (0, 0)))
    else:
        s0, hist0 = None, None
    ssm, s_ssm_new = _ssd(big, dts, w["conv_w"], w["conv_b"], w["d_skip_x"], w["ssm_norm_g"], s0, hist0,
                          nseq=nseq, seqlen=seqlen, q=q, seg=seg, nck=nck_ssd)
    s_ssm_new = s_ssm_new.reshape(nseq, SSM_HEADS, SSM_HEADDIM, SSM_STATE)

    keep = min(seqlen, SSM_CONV - 1)
    xbc_tail = big.reshape(nseq, seqlen, N_BIG)[:, seqlen - keep:, COL_XBC:].astype(F32)
    if keep < SSM_CONV - 1:
        prev = jnp.zeros((nseq, SSM_CONV - 1, SSM_CONV_DIM), F32) if s_conv is None else s_conv.astype(F32)
        xbc_tail = jnp.concatenate([prev, xbc_tail], axis=1)[:, -(SSM_CONV - 1):]

    merged = _merge(h, ret, ssm, w["w_gr"], w["w_gs"], w["w_br_ret"], w["w_br_ssm"], tm=_pick(n, cfg["tm_mg"]), tn=cfg["tn_mg"])
    x2 = _outproj(merged, w["w_out"], x1, tm=_pick(n, cfg["tm_out"]), tn=cfg["tn_out"])
    x3, h3 = _ffn(x2, w["g_ffn2"], w["g_ple"], w["w2_gu"], w["w2_down"], tm=_pick(n, cfg["tm_ffn"]), tf=cfg["tf"])
    y = _ple(x3, h3, p, w["w_ple"], w["w_ple_gate"], w["g_final"], tm=_pick(n, cfg["tm_ple"]))
    return y.reshape(nseq, seqlen, d), s_ret_new, s_ssm_new, xbc_tail


CFG = dict(tm=1024, tm_ffn=512, tf=512, tn_in=2048, q_ssd=128, c_ret=256, tb_ret=1024, tb_ssd=4096, pack_chunks=2,
           tm_mg=512, tn_mg=512, tm_out=512, tn_out=2048, tm_ple=512)


def _prep_weights(g_ffn1, w1_gu, w1_down, g_mix, w_in, ret_gn_g, ret_gn_b, conv_w, conv_b, dt_bias, a_log,
                  d_skip, ssm_norm_g, w_br_ret, w_br_ssm, w_out, g_ffn2, w2_gu, w2_down, g_ple, w_ple,
                  w_ple_gate, g_final):
    b = lambda a: a[0].astype(BF16)
    r = lambda a: a[0].reshape(1, -1).astype(F32)
    win = w_in[0]
    w_dt = win[:, COL_DT:COL_DT + SSM_HEADS].astype(BF16)
    return dict(
        g_ffn1=r(g_ffn1), w1_gu=b(w1_gu), w1_down=b(w1_down), g_mix=r(g_mix),
        w_in_t=win.T.astype(BF16), w_dt=w_dt,
        w_gr=win[:, COL_GR:COL_GR + D_MODEL].astype(BF16), w_gs=win[:, COL_GS:COL_GS + D_MODEL].astype(BF16),
        ret_gn_g=ret_gn_g[0].astype(F32), ret_gn_b=ret_gn_b[0].astype(F32),
        conv_w=conv_w[0].astype(F32), conv_b=conv_b[0].astype(F32),
        dt_bias=dt_bias[0].astype(F32), a_log=a_log[0].astype(F32),
        d_skip_x=jnp.repeat(d_skip[0].astype(F32), SSM_HEADDIM).reshape(1, SSM_INNER),
        ssm_norm_g=ssm_norm_g[0].astype(F32),
        w_br_ret=b(w_br_ret), w_br_ssm=b(w_br_ssm), w_out=b(w_out),
        g_ffn2=r(g_ffn2), w2_gu=b(w2_gu), w2_down=b(w2_down), g_ple=r(g_ple),
        w_ple=b(w_ple), w_ple_gate=b(w_ple_gate), g_final=g_final.reshape(1, -1).astype(F32),
    )


def kernel(x_prompt, x_sample, state_ret, state_ssm, state_conv, p_prompt, p_sample, g_ffn1, w1_gu, w1_down, g_mix, w_in, ret_gn_g, ret_gn_b, conv_w, conv_b, dt_bias, a_log, d_skip, ssm_norm_g, w_br_ret, w_br_ssm, w_out, g_ffn2, w2_gu, w2_down, g_ple, w_ple, w_ple_gate, g_final):
    assert g_ffn1.shape[0] == 1, "single-layer trunk"
    w = _prep_weights(g_ffn1, w1_gu, w1_down, g_mix, w_in, ret_gn_g, ret_gn_b, conv_w, conv_b, dt_bias, a_log,
                      d_skip, ssm_norm_g, w_br_ret, w_br_ssm, w_out, g_ffn2, w2_gu, w2_down, g_ple, w_ple,
                      w_ple_gate, g_final)
    y_p, ret_p, ssm_p, conv_p = _trunk(x_prompt, p_prompt[0], 0, None, None, None, w, CFG)
    y_s, ret_s, ssm_s, conv_s = _trunk(x_sample, p_sample[0], PAST_LEN, state_ret[0], state_ssm[0],
                                       state_conv[0], w, CFG)
    e = lambda a: a[None]
    return (y_p, y_s, e(ret_p), e(ssm_p), e(conv_p), e(ret_s), e(ssm_s), e(conv_s))
```
